```python
import math
import jax, jax.numpy as jnp
from jax import lax
import numpy as np

D_MODEL = 2048
BATCH = 4
SEQ = 2048
DEPTH = 2
DEC_BATCH = 128
DEC_SEQ = 8
PAST_LEN = 16384
PAGE_SIZE = 128

N_EVEN = (DEPTH + 1) // 2
N_ODD = DEPTH // 2
EPS = 1e-6
CONV_W = 4
W_LRU = D_MODEL // 2
LRU_BLOCKS = 16
LRU_BS = W_LRU // LRU_BLOCKS
LRU_C = 8.0
H_GLA = 4
DK_GLA = D_MODEL // (4 * H_GLA)
DV_GLA = D_MODEL // (2 * H_GLA)
GLA_RANK = 16
GLA_TAU = 16.0
GLA_CHUNK = 64
_HK = H_GLA * DK_GLA
_HV = H_GLA * DV_GLA
SPLIT0 = (W_LRU, 2 * W_LRU, 2 * W_LRU + _HK, 2 * W_LRU + 2 * _HK, 2 * W_LRU + 2 * _HK + _HV, 2 * W_LRU + 2 * _HK + 2 * _HV)
IN0_COLS = 2 * W_LRU + 2 * _HK + 2 * _HV + GLA_RANK
MIX0 = W_LRU + _HV
M_PROJ = 2
D_INNER = M_PROJ * D_MODEL
H_M = 4
DH_M = D_INNER // H_M
QKV_BS = 4
M_CHUNK = 64
N_GROUPS = 4
E_PER_GROUP = 4
N_EXPERTS = N_GROUPS * E_PER_GROUP
TOP_K_FINE = 2
D_EXPERT = 256

kernel_name = 'hawk_gla_mlstm_hmoe_step'


def rmsnorm(x, g):
    xf = x.astype(jnp.float32)
    y = xf * lax.rsqrt(jnp.mean(xf * xf, axis=-1, keepdims=True) + EPS)
    return (y * g.astype(jnp.float32)).astype(x.dtype)


def causal_conv(x, buf, w, b):
    T = x.shape[1]
    xp = jnp.concatenate([buf.astype(x.dtype), x], axis=1)
    y = b.astype(x.dtype)
    for j in range(CONV_W):
        y = y + xp[:, j:j + T] * w[j]
    return y, xp[:, T:]


def blockdiag(x, w):
    nb, bi, bo = w.shape
    xb = x.reshape(x.shape[:-1] + (nb, bi))
    return jnp.einsum('btni,nio->btno', xb, w).reshape(x.shape[:-1] + (nb * bo,))


def to_chunks(a, L):
    nc = a.shape[2] // L
    return jnp.moveaxis(a.reshape(a.shape[:2] + (nc, L) + a.shape[3:]), 2, 0)


def from_chunks(a):
    a = jnp.moveaxis(a, 0, 2)
    return a.reshape(a.shape[:2] + (a.shape[2] * a.shape[3],) + a.shape[4:])


def head_rmsnorm(o, g):
    B, H, T, Dh = o.shape
    on = o * lax.rsqrt(jnp.mean(o * o, axis=-1, keepdims=True) + EPS)
    return jnp.transpose(on, (0, 2, 1, 3)).reshape(B, T, H * Dh) * g.astype(jnp.float32)


def head_layernorm(h, g):
    B, H, T, Dh = h.shape
    mu = jnp.mean(h, axis=-1, keepdims=True)
    hc = h - mu
    hn = hc * lax.rsqrt(jnp.mean(hc * hc, axis=-1, keepdims=True) + EPS)
    return jnp.transpose(hn, (0, 2, 1, 3)).reshape(B, T, H * Dh) * g.astype(jnp.float32)


def rg_lru(xc, h0, wa, ba, wx, bx, lam):
    xf = xc.astype(jnp.float32)
    r = jax.nn.sigmoid(blockdiag(xc, wa).astype(jnp.float32) + ba.astype(jnp.float32))
    i = jax.nn.sigmoid(blockdiag(xc, wx).astype(jnp.float32) + bx.astype(jnp.float32))
    log_a = -LRU_C * r * jax.nn.softplus(-lam.astype(jnp.float32))
    a = jnp.exp(log_a)
    b = jnp.sqrt(-jnp.expm1(2.0 * log_a)) * (i * xf)
    b = b.at[:, 0].add(a[:, 0] * h0.astype(jnp.float32))

    def combine(left, right):
        a1, b1 = left
        a2, b2 = right
        return a1 * a2, a2 * b1 + b2

    _, h = lax.associative_scan(combine, (a, b), axis=1)
    return h, h[:, -1]


def gla_chunked(q, k, v, lg, S0):
    T = q.shape[2]
    L = math.gcd(T, GLA_CHUNK)
    mask = jnp.tril(jnp.ones((L, L), dtype=bool))[:, :, None]

    def step(S, inp):
        qc, kc, vc, gc = inp
        bc = jnp.cumsum(gc, axis=2)
        o = jnp.einsum('bhtd,bhde->bhte', qc * jnp.exp(bc), S)
        decay = jnp.exp(jnp.where(mask, bc[:, :, :, None, :] - bc[:, :, None, :, :], -jnp.inf))
        att = jnp.einsum('bhtd,bhsd,bhtsd->bhts', qc, kc, decay)
        o = o + jnp.einsum('bhts,bhse->bhte', att, vc)
        bl = bc[:, :, -1:, :]
        S = jnp.exp(bl[:, :, 0, :])[..., None] * S + jnp.einsum('bhsd,bhse->bhde', kc * jnp.exp(bl - bc), vc)
        return S, o

    S, o = lax.scan(step, S0, (to_chunks(q, L), to_chunks(k, L), to_chunks(v, L), to_chunks(lg, L)))
    return from_chunks(o), S


def mlstm_chunked(q, k, v, ig, lf, C0, n0, m0):
    T = q.shape[2]
    L = math.gcd(T, M_CHUNK)
    mask = jnp.tril(jnp.ones((L, L), dtype=bool))

    def step(carry, inp):
        C, n, m = carry
        qc, kc, vc, ic, fc = inp
        F = jnp.cumsum(fc, axis=-1)
        Dm = jnp.where(mask, F[..., :, None] - F[..., None, :] + ic[..., None, :], -jnp.inf)
        prev = m[..., None] + F
        mt = jnp.maximum(prev, jnp.max(Dm, axis=-1))
        wprev = jnp.exp(prev - mt)
        s = jnp.einsum('bhtd,bhsd->bhts', qc, kc) * jnp.exp(Dm - mt[..., None])
        num = wprev[..., None] * jnp.einsum('bhtd,bhde->bhte', qc, C) + jnp.einsum('bhts,bhse->bhte', s, vc)
        den = wprev * jnp.einsum('bhtd,bhd->bht', qc, n) + jnp.sum(s, axis=-1)
        h = num / jnp.maximum(jnp.abs(den), jnp.exp(-mt))[..., None]
        mL = mt[..., -1]
        wC = jnp.exp(m + F[..., -1] - mL)
        wS = jnp.exp(F[..., -1:] - F + ic - mL[..., None])
        C = wC[..., None, None] * C + jnp.einsum('bhsd,bhse->bhde', kc * wS[..., None], vc)
        n = wC[..., None] * n + jnp.einsum('bhs,bhsd->bhd', wS, kc)
        return (C, n, mL), h

    (C, n, m), h = lax.scan(step, (C0, n0, m0), (to_chunks(q, L), to_chunks(k, L), to_chunks(v, L), to_chunks(ig, L), to_chunks(lf, L)))
    return from_chunks(h), C, n, m


def rglru_gla_mixer(xn, lru_buf, lru_h, gla_S, w_in, lru_conv_w, lru_conv_b, lru_wa, lru_ba, lru_wx, lru_bx, lru_lam, gla_wa2, gla_ba2, gla_norm_g, w_out):
    B, T, _ = xn.shape
    proj = xn @ w_in
    xa, ga, q, k, v, g, alr = jnp.split(proj, SPLIT0, axis=-1)
    xc, new_buf = causal_conv(xa, lru_buf, lru_conv_w, lru_conv_b)
    h, h_last = rg_lru(xc, lru_h, lru_wa, lru_ba, lru_wx, lru_bx, lru_lam)
    ya = h * jax.nn.gelu(ga.astype(jnp.float32))
    lg = jax.nn.log_sigmoid((alr @ gla_wa2).astype(jnp.float32) + gla_ba2.astype(jnp.float32)) / GLA_TAU

    def heads(a, d):
        return jnp.transpose(a.reshape(B, T, H_GLA, d), (0, 2, 1, 3)).astype(jnp.float32)

    o, S_new = gla_chunked(heads(q, DK_GLA) * (DK_GLA ** -0.5), heads(k, DK_GLA), heads(v, DV_GLA), heads(lg, DK_GLA), gla_S.astype(jnp.float32))
    yb = head_rmsnorm(o, gla_norm_g) * jax.nn.silu(g.astype(jnp.float32))
    y = jnp.concatenate([ya, yb], axis=-1).astype(xn.dtype) @ w_out
    return y, new_buf.astype(lru_buf.dtype), h_last.astype(lru_h.dtype), S_new.astype(gla_S.dtype)


def mlstm_mixer(xn, conv_buf, C0, n0, m0, w_up, conv_w, conv_b, wq, wk, wv, w_ig, b_ig, w_fg, b_fg, skip, norm_g, w_down):
    B, T, _ = xn.shape
    xm, z = jnp.split(xn @ w_up, 2, axis=-1)
    xc, new_buf = causal_conv(xm, conv_buf, conv_w, conv_b)
    xc = jax.nn.silu(xc)
    q = blockdiag(xc, wq)
    k = blockdiag(xc, wk)
    v = blockdiag(xm, wv)
    qkv = jnp.concatenate([q, k, v], axis=-1)
    ig = (qkv @ w_ig).astype(jnp.float32) + b_ig.astype(jnp.float32)
    lf = jax.nn.log_sigmoid((qkv @ w_fg).astype(jnp.float32) + b_fg.astype(jnp.float32))

    def heads(a):
        return jnp.transpose(a.reshape(B, T, H_M, DH_M), (0, 2, 1, 3)).astype(jnp.float32)

    h, C, n, m = mlstm_chunked(heads(q), heads(k) * (DH_M ** -0.5), heads(v), jnp.transpose(ig, (0, 2, 1)), jnp.transpose(lf, (0, 2, 1)), C0.astype(jnp.float32), n0.astype(jnp.float32), m0.astype(jnp.float32))
    hn = head_layernorm(h, norm_g)
    out = (hn + skip.astype(jnp.float32) * xc.astype(jnp.float32)) * jax.nn.silu(z.astype(jnp.float32))
    y = out.astype(xn.dtype) @ w_down
    return y, new_buf.astype(conv_buf.dtype), C.astype(C0.dtype), n.astype(n0.dtype), m.astype(m0.dtype)


def hier_moe(xn, w_rg, b_rg, w_re, b_re, w_gate, w_up, w_down):
    B, T, D = xn.shape
    xf = xn.reshape(B * T, D)
    gp = jax.nn.softmax((xf @ w_rg).astype(jnp.float32) + b_rg.astype(jnp.float32), axis=-1)
    p_g, g_idx = lax.top_k(gp, 1)
    g_oh = jax.nn.one_hot(g_idx[:, 0], N_GROUPS, dtype=jnp.float32)
    el = ((xf @ w_re).astype(jnp.float32) + b_re.astype(jnp.float32)).reshape(B * T, N_GROUPS, E_PER_GROUP)
    el_sel = jnp.einsum('nge,ng->ne', el, g_oh)
    ep = jax.nn.softmax(el_sel, axis=-1)
    tv, ti = lax.top_k(ep, TOP_K_FINE)
    tv = tv / jnp.sum(tv, axis=-1, keepdims=True) * p_g
    eid = g_idx * E_PER_GROUP + ti
    gates = jnp.sum(jax.nn.one_hot(eid, N_EXPERTS, dtype=jnp.float32) * tv[..., None], axis=1)
    h = jax.nn.silu(jnp.einsum('nd,edf->nef', xf, w_gate)) * jnp.einsum('nd,edf->nef', xf, w_up)
    y = jnp.einsum('nef,efd->nd', h * gates[:, :, None].astype(h.dtype), w_down)
    return y.reshape(B, T, D)


def trunk(x, st_lru_conv, st_lru_h, st_gla_S, st_m_conv, st_m_C, st_m_n, st_m_m,
          norm_mix_g, norm_ffn_g, norm_final_g,
          l0_w_in, l0_lru_conv_w, l0_lru_conv_b, l0_lru_wa, l0_lru_ba, l0_lru_wx, l0_lru_bx, l0_lru_lam, l0_gla_wa2, l0_gla_ba2, l0_gla_norm_g, l0_w_out,
          l1_w_up, l1_conv_w, l1_conv_b, l1_wq, l1_wk, l1_wv, l1_w_ig, l1_b_ig, l1_w_fg, l1_b_fg, l1_skip, l1_norm_g, l1_w_down,
          moe_w_rg, moe_b_rg, moe_w_re, moe_b_re, moe_w_gate, moe_w_up, moe_w_down):
    o_lru_conv, o_lru_h, o_gla_S, o_m_conv, o_m_C, o_m_n, o_m_m = [], [], [], [], [], [], []
    for layer in range(DEPTH):
        j = layer // 2
        xn = rmsnorm(x, norm_mix_g[layer])
        if layer % 2 == 0:
            y, nb, nh, nS = rglru_gla_mixer(xn, st_lru_conv[j], st_lru_h[j], st_gla_S[j], l0_w_in[j], l0_lru_conv_w[j], l0_lru_conv_b[j], l0_lru_wa[j], l0_lru_ba[j], l0_lru_wx[j], l0_lru_bx[j], l0_lru_lam[j], l0_gla_wa2[j], l0_gla_ba2[j], l0_gla_norm_g[j], l0_w_out[j])
            o_lru_conv.append(nb)
            o_lru_h.append(nh)
            o_gla_S.append(nS)
        else:
            y, nb, nC, nn, nm = mlstm_mixer(xn, st_m_conv[j], st_m_C[j], st_m_n[j], st_m_m[j], l1_w_up[j], l1_conv_w[j], l1_conv_b[j], l1_wq[j], l1_wk[j], l1_wv[j], l1_w_ig[j], l1_b_ig[j], l1_w_fg[j], l1_b_fg[j], l1_skip[j], l1_norm_g[j], l1_w_down[j])
            o_m_conv.append(nb)
            o_m_C.append(nC)
            o_m_n.append(nn)
            o_m_m.append(nm)
        x = x + y.astype(x.dtype)
        x = x + hier_moe(rmsnorm(x, norm_ffn_g[layer]), moe_w_rg[layer], moe_b_rg[layer], moe_w_re[layer], moe_b_re[layer], moe_w_gate[layer], moe_w_up[layer], moe_w_down[layer]).astype(x.dtype)
    y = rmsnorm(x, norm_final_g)
    return (y, jnp.stack(o_lru_conv), jnp.stack(o_lru_h), jnp.stack(o_gla_S), jnp.stack(o_m_conv), jnp.stack(o_m_C), jnp.stack(o_m_n), jnp.stack(o_m_m))


def setup_inputs(seed: int = 0) -> dict:
    key = jax.random.key(seed)
    ks = iter(jax.random.split(key, 64))
    f32 = jnp.float32

    def nrm(shape, s):
        return jax.random.normal(next(ks), shape, f32) * s

    u = jax.random.uniform(next(ks), (N_EVEN, W_LRU), f32, minval=0.9, maxval=0.999)
    sa = u ** (1.0 / LRU_C)
    lam = jnp.log(sa) - jnp.log1p(-sa)
    return {
        'x_prompt': nrm((BATCH, SEQ, D_MODEL), 1.0),
        'x_sample': nrm((DEC_BATCH, DEC_SEQ, D_MODEL), 1.0),
        'state_lru_conv': nrm((N_EVEN, DEC_BATCH, CONV_W - 1, W_LRU), 1.0),
        'state_lru_h': nrm((N_EVEN, DEC_BATCH, W_LRU), 0.5),
        'state_gla_S': nrm((N_EVEN, DEC_BATCH, H_GLA, DK_GLA, DV_GLA), 1.0),
        'state_mlstm_conv': nrm((N_ODD, DEC_BATCH, CONV_W - 1, D_INNER), 1.0),
        'state_mlstm_C': nrm((N_ODD, DEC_BATCH, H_M, DH_M, DH_M), 1.0),
        'state_mlstm_n': nrm((N_ODD, DEC_BATCH, H_M, DH_M), 1.0),
        'state_mlstm_m': nrm((N_ODD, DEC_BATCH, H_M), 2.0),
        'norm_mix_g': 1.0 + nrm((DEPTH, D_MODEL), 0.02),
        'norm_ffn_g': 1.0 + nrm((DEPTH, D_MODEL), 0.02),
        'norm_final_g': 1.0 + nrm((D_MODEL,), 0.02),
        'l0_w_in': nrm((N_EVEN, D_MODEL, IN0_COLS), D_MODEL ** -0.5),
        'l0_lru_conv_w': nrm((N_EVEN, CONV_W, W_LRU), CONV_W ** -0.5),
        'l0_lru_conv_b': nrm((N_EVEN, W_LRU), 0.01),
        'l0_lru_wa': nrm((N_EVEN, LRU_BLOCKS, LRU_BS, LRU_BS), LRU_BS ** -0.5),
        'l0_lru_ba': nrm((N_EVEN, W_LRU), 0.01),
        'l0_lru_wx': nrm((N_EVEN, LRU_BLOCKS, LRU_BS, LRU_BS), LRU_BS ** -0.5),
        'l0_lru_bx': nrm((N_EVEN, W_LRU), 0.01),
        'l0_lru_lam': lam,
        'l0_gla_wa2': nrm((N_EVEN, GLA_RANK, _HK), GLA_RANK ** -0.5),
        'l0_gla_ba2': nrm((N_EVEN, _HK), 0.1),
        'l0_gla_norm_g': 1.0 + nrm((N_EVEN, _HV), 0.02),
        'l0_w_out': nrm((N_EVEN, MIX0, D_MODEL), MIX0 ** -0.5),
        'l1_w_up': nrm((N_ODD, D_MODEL, 2 * D_INNER), D_MODEL ** -0.5),
        'l1_conv_w': nrm((N_ODD, CONV_W, D_INNER), CONV_W ** -0.5),
        'l1_conv_b': nrm((N_ODD, D_INNER), 0.01),
        'l1_wq': nrm((N_ODD, D_INNER // QKV_BS, QKV_BS, QKV_BS), QKV_BS ** -0.5),
        'l1_wk': nrm((N_ODD, D_INNER // QKV_BS, QKV_BS, QKV_BS), QKV_BS ** -0.5),
        'l1_wv': nrm((N_ODD, D_INNER // QKV_BS, QKV_BS, QKV_BS), QKV_BS ** -0.5),
        'l1_w_ig': nrm((N_ODD, 3 * D_INNER, H_M), 0.1 * (3 * D_INNER) ** -0.5),
        'l1_b_ig': nrm((N_ODD, H_M), 0.1),
        'l1_w_fg': nrm((N_ODD, 3 * D_INNER, H_M), 0.1 * (3 * D_INNER) ** -0.5),
        'l1_b_fg': jnp.linspace(3.0, 6.0, H_M, dtype=f32)[None, :] + nrm((N_ODD, H_M), 0.1),
        'l1_skip': 1.0 + nrm((N_ODD, D_INNER), 0.02),
        'l1_norm_g': 1.0 + nrm((N_ODD, D_INNER), 0.02),
        'l1_w_down': nrm((N_ODD, D_INNER, D_MODEL), D_INNER ** -0.5),
        'moe_w_rg': nrm((DEPTH, D_MODEL, N_GROUPS), D_MODEL ** -0.5),
        'moe_b_rg': nrm((DEPTH, N_GROUPS), 0.01),
        'moe_w_re': nrm((DEPTH, D_MODEL, N_EXPERTS), D_MODEL ** -0.5),
        'moe_b_re': nrm((DEPTH, N_EXPERTS), 0.01),
        'moe_w_gate': nrm((DEPTH, N_EXPERTS, D_MODEL, D_EXPERT), D_MODEL ** -0.5),
        'moe_w_up': nrm((DEPTH, N_EXPERTS, D_MODEL, D_EXPERT), D_MODEL ** -0.5),
        'moe_w_down': nrm((DEPTH, N_EXPERTS, D_EXPERT, D_MODEL), D_EXPERT ** -0.5),
    }


def reference(x_prompt, x_sample, state_lru_conv, state_lru_h, state_gla_S, state_mlstm_conv, state_mlstm_C, state_mlstm_n, state_mlstm_m,
              norm_mix_g, norm_ffn_g, norm_final_g,
              l0_w_in, l0_lru_conv_w, l0_lru_conv_b, l0_lru_wa, l0_lru_ba, l0_lru_wx, l0_lru_bx, l0_lru_lam, l0_gla_wa2, l0_gla_ba2, l0_gla_norm_g, l0_w_out,
              l1_w_up, l1_conv_w, l1_conv_b, l1_wq, l1_wk, l1_wv, l1_w_ig, l1_b_ig, l1_w_fg, l1_b_fg, l1_skip, l1_norm_g, l1_w_down,
              moe_w_rg, moe_b_rg, moe_w_re, moe_b_re, moe_w_gate, moe_w_up, moe_w_down):
    weights = (norm_mix_g, norm_ffn_g, norm_final_g,
               l0_w_in, l0_lru_conv_w, l0_lru_conv_b, l0_lru_wa, l0_lru_ba, l0_lru_wx, l0_lru_bx, l0_lru_lam, l0_gla_wa2, l0_gla_ba2, l0_gla_norm_g, l0_w_out,
               l1_w_up, l1_conv_w, l1_conv_b, l1_wq, l1_wk, l1_wv, l1_w_ig, l1_b_ig, l1_w_fg, l1_b_fg, l1_skip, l1_norm_g, l1_w_down,
               moe_w_rg, moe_b_rg, moe_w_re, moe_b_re, moe_w_gate, moe_w_up, moe_w_down)
    bp = x_prompt.shape[0]

    def zero_state(s):
        return jnp.zeros((s.shape[0], bp) + s.shape[2:], s.dtype)

    (y_prompt, p_lru_conv, p_lru_h, p_gla_S, p_m_conv, p_m_C, p_m_n, p_m_m) = trunk(
        x_prompt, zero_state(state_lru_conv), zero_state(state_lru_h), zero_state(state_gla_S), zero_state(state_mlstm_conv),
        zero_state(state_mlstm_C), zero_state(state_mlstm_n), zero_state(state_mlstm_m), *weights)
    (y_sample, s_lru_conv, s_lru_h, s_gla_S, s_m_conv, s_m_C, s_m_n, s_m_m) = trunk(
        x_sample, state_lru_conv, state_lru_h, state_gla_S, state_mlstm_conv, state_mlstm_C, state_mlstm_n, state_mlstm_m, *weights)
    return (y_prompt, y_sample, p_lru_conv, p_lru_h, p_gla_S, p_m_conv, p_m_C, p_m_n, p_m_m, s_lru_conv, s_lru_h, s_gla_S, s_m_conv, s_m_C, s_m_n, s_m_m)
```

```python
import functools

import jax
import jax.numpy as jnp
from jax import lax
from jax.experimental import pallas as pl
from jax.experimental.pallas import tpu as pltpu

EPS = 1e-6
CONV_W = 4
LRU_C = 8.0
GLA_TAU = 16.0
GLA_CHUNK = 64
GLA_SUB = 16
M_CHUNK = 64
N_GROUPS = 4
E_PER_GROUP = 4
N_EXPERTS = N_GROUPS * E_PER_GROUP

V7X_VMEM_BYTES = 64 * 1024 * 1024
VMEM_LIMIT_BYTES = V7X_VMEM_BYTES - 8 * 1024 * 1024
SUBLANES = 8
LANES = 128
MXU_WIDTH = 256

F32 = jnp.float32
BF16 = jnp.bfloat16


def _params(*semantics):
    return pltpu.CompilerParams(dimension_semantics=semantics, vmem_limit_bytes=VMEM_LIMIT_BYTES)


def _dot(a, b):
    return jnp.dot(a.astype(BF16), b.astype(BF16), preferred_element_type=F32)


def _dot_nt(a, b):
    return lax.dot_general(a.astype(BF16), b.astype(BF16), (((1,), (1,)), ((), ())), preferred_element_type=F32)


def _dot_tn(a, b):
    return lax.dot_general(a.astype(BF16), b.astype(BF16), (((0,), (0,)), ((), ())), preferred_element_type=F32)


def _dot_split(m01, y):
    y_hi = y.astype(BF16)
    y_lo = (y - y_hi.astype(F32)).astype(BF16)
    m = m01.astype(BF16)
    return jnp.dot(m, y_hi, preferred_element_type=F32) + jnp.dot(m, y_lo, preferred_element_type=F32)


def _softplus(x):
    return jnp.maximum(x, 0.0) + jnp.log1p(jnp.exp(-jnp.abs(x)))


def _log_sigmoid(x):
    return -_softplus(-x)


def _rmsnorm_rows(x, g):
    return x * lax.rsqrt(jnp.mean(x * x, axis=-1, keepdims=True) + EPS) * g


def _iota(shape, dim):
    return lax.broadcasted_iota(jnp.int32, shape, dim)


def _lane_column(x, lane_index):
    return jnp.sum(jnp.where(_iota(x.shape, 1) == lane_index, x, 0.0), axis=1, keepdims=True)


def _row_tile(n, target):
    t = min(n, target)
    assert n % t == 0
    return t


def _linear_kernel(*refs, n_lhs, has_norm, has_res):
    lhs_refs = refs[:n_lhs]
    pos = n_lhs
    g_ref = refs[pos] if has_norm else None
    pos += int(has_norm)
    w_ref = refs[pos]
    pos += 1
    res_ref = refs[pos] if has_res else None
    pos += int(has_res)
    o_ref, lhs_scr = refs[pos], refs[pos + 1]

    @pl.when(pl.program_id(1) == 0)
    def _():
        off = 0
        for a_ref in lhs_refs:
            a = a_ref[...]
            if has_norm:
                a = _rmsnorm_rows(a, g_ref[...])
            lhs_scr[:, off:off + a.shape[1]] = a.astype(BF16)
            off += a.shape[1]

    acc = jnp.dot(lhs_scr[...], w_ref[...].astype(BF16), preferred_element_type=F32)
    if has_res:
        acc = res_ref[...] + acc
    o_ref[...] = acc


def fused_linear(lhs_list, w, *, n_out, gain=None, res=None, tm=512, tn=512):
    n = lhs_list[0].shape[0]
    ks = [a.shape[1] for a in lhs_list]
    ktot = sum(ks)
    assert w.shape[0] == ktot
    tm = _row_tile(n, tm)
    tn = _row_tile(n_out, tn)
    in_specs = [pl.BlockSpec((tm, k), lambda i, j: (i, 0)) for k in ks]
    args = list(lhs_list)
    if gain is not None:
        in_specs.append(pl.BlockSpec((1, ktot), lambda i, j: (0, 0)))
        args.append(gain.reshape(1, ktot))
    in_specs.append(pl.BlockSpec((ktot, tn), lambda i, j: (0, j)))
    args.append(w)
    if res is not None:
        in_specs.append(pl.BlockSpec((tm, tn), lambda i, j: (i, j)))
        args.append(res)
    kern = functools.partial(_linear_kernel, n_lhs=len(lhs_list), has_norm=gain is not None, has_res=res is not None)
    return pl.pallas_call(
        kern,
        grid=(n // tm, n_out // tn),
        in_specs=in_specs,
        out_specs=pl.BlockSpec((tm, tn), lambda i, j: (i, j)),
        out_shape=jax.ShapeDtypeStruct((n, n_out), F32),
        scratch_shapes=[pltpu.VMEM((tm, ktot), BF16)],
        compiler_params=_params("parallel", "arbitrary"),
    )(*args)


def _gla_decay_kernel(x_ref, g_ref, walr_ref, wa2_ref, ba2_ref, o_ref):
    xn = _rmsnorm_rows(x_ref[...], g_ref[...])
    alr = _dot(xn, walr_ref[...])
    o_ref[...] = _log_sigmoid(_dot(alr, wa2_ref[...]) + ba2_ref[...]) * (1.0 / GLA_TAU)


def gla_decay(x, gain, w_alr_pad, wa2_pad, ba2, tm=512):
    n, d = x.shape
    hk = wa2_pad.shape[1]
    tm = _row_tile(n, tm)
    return pl.pallas_call(
        _gla_decay_kernel,
        grid=(n // tm,),
        in_specs=[
            pl.BlockSpec((tm, d), lambda i: (i, 0)),
            pl.BlockSpec((1, d), lambda i: (0, 0)),
            pl.BlockSpec((d, LANES), lambda i: (0, 0)),
            pl.BlockSpec((LANES, hk), lambda i: (0, 0)),
            pl.BlockSpec((1, hk), lambda i: (0, 0)),
        ],
        out_specs=pl.BlockSpec((tm, hk), lambda i: (i, 0)),
        out_shape=jax.ShapeDtypeStruct((n, hk), F32),
        compiler_params=_params("parallel"),
    )(x, gain.reshape(1, d), w_alr_pad, wa2_pad, ba2.reshape(1, hk))


def _causal_conv(x, prev, w_ref, b_ref, seq_rows):
    r = x.shape[0]
    row = _iota((r, 1), 0)
    per_seq = prev.shape[0] == r
    pos = row & (seq_rows - 1) if per_seq else row
    acc = b_ref[...] + x * w_ref[CONV_W - 1:CONV_W, :]
    for j in range(1, CONV_W):
        if per_seq:
            hist = pltpu.roll(prev, (j - SUBLANES) % r, 0)
        else:
            hist = jnp.concatenate([pltpu.roll(prev, j, 0), x[SUBLANES:]], axis=0)
        shifted = jnp.where(pos < j, hist, pltpu.roll(x, j, 0))
        acc = acc + shifted * w_ref[CONV_W - 1 - j:CONV_W - j, :]
    return acc


def _lru_kernel(xa_ref, ga_ref, prev_ref, h0_ref, cw_ref, cb_ref, wa_ref, ba_ref, wx_ref, bx_ref, lam_ref,
                ya_ref, hl_ref, prev_scr, h_scr, a_scr, b_scr, *, nb, tc):
    c = pl.program_id(1)

    @pl.when(c == 0)
    def _():
        prev_scr[...] = prev_ref[...]
        h_scr[...] = h0_ref[...]

    x = xa_ref[...]
    xc = _causal_conv(x, prev_scr[...], cw_ref, cb_ref, tc)
    if nb == 1:
        prev_scr[...] = x[x.shape[0] - SUBLANES:]
    nblk = x.shape[1] // MXU_WIDTH
    xcb = xc.astype(BF16)
    r_parts, i_parts = [], []
    for t in range(nblk):
        sl = slice(t * MXU_WIDTH, (t + 1) * MXU_WIDTH)
        r_parts.append(jnp.dot(xcb[:, sl], wa_ref[t], preferred_element_type=F32))
        i_parts.append(jnp.dot(xcb[:, sl], wx_ref[t], preferred_element_type=F32))
    r_gate = jax.nn.sigmoid(jnp.concatenate(r_parts, axis=1) + ba_ref[...])
    i_gate = jax.nn.sigmoid(jnp.concatenate(i_parts, axis=1) + bx_ref[...])
    log_a = (-LRU_C) * r_gate * _softplus(-lam_ref[...])
    a = jnp.exp(log_a)
    a_scr[...] = a
    b_scr[...] = jnp.sqrt(-jnp.tanh(log_a) * (a * a + 1.0)) * (i_gate * xc)

    def seq_body(s, carry):
        def tile_body(tl, h):
            base = pl.multiple_of(s * tc + tl * SUBLANES, SUBLANES)
            for i in range(SUBLANES):
                h = a_scr[pl.ds(base + i, 1), :] * h + b_scr[pl.ds(base + i, 1), :]
                b_scr[pl.ds(base + i, 1), :] = h
            return h

        h_scr[s] = lax.fori_loop(0, tc // SUBLANES, tile_body, h_scr[s])
        return carry

    lax.fori_loop(0, nb, seq_body, 0)
    ya_ref[...] = b_scr[...] * jax.nn.gelu(ga_ref[...])

    @pl.when(c == pl.num_programs(1) - 1)
    def _():
        hl_ref[...] = h_scr[...]


def lru_branch(proj, prev8, h0, conv_w, conv_b, wa_t, ba, wx_t, bx, lam, *, nseq, t, nb, tc):
    n = proj.shape[0]
    w = h0.shape[1]
    nchunk = t // tc
    assert nseq % nb == 0 and t % tc == 0 and (nb == 1 or tc == t == SUBLANES)
    r = nb * tc
    nblk = w // MXU_WIDTH
    rows = lambda s, c: s * nchunk + c
    full2 = lambda s, c: (0, 0)
    full3 = lambda s, c: (0, 0, 0)
    kern = functools.partial(_lru_kernel, nb=nb, tc=tc)
    ya, hl = pl.pallas_call(
        kern,
        grid=(nseq // nb, nchunk),
        in_specs=[
            pl.BlockSpec((r, w), lambda s, c: (rows(s, c), 0)),
            pl.BlockSpec((r, w), lambda s, c: (rows(s, c), 1)),
            pl.BlockSpec((nb * SUBLANES, w), lambda s, c: (s, 0)),
            pl.BlockSpec((nb, 1, w), lambda s, c: (s, 0, 0)),
            pl.BlockSpec((CONV_W, w), full2),
            pl.BlockSpec((1, w), full2),
            pl.BlockSpec((nblk, MXU_WIDTH, MXU_WIDTH), full3),
            pl.BlockSpec((1, w), full2),
            pl.BlockSpec((nblk, MXU_WIDTH, MXU_WIDTH), full3),
            pl.BlockSpec((1, w), full2),
            pl.BlockSpec((1, w), full2),
        ],
        out_specs=[
            pl.BlockSpec((r, w), lambda s, c: (rows(s, c), 0)),
            pl.BlockSpec((nb, 1, w), lambda s, c: (s, 0, 0)),
        ],
        out_shape=[jax.ShapeDtypeStruct((n, w), F32), jax.ShapeDtypeStruct((nseq, 1, w), F32)],
        scratch_shapes=[
            pltpu.VMEM((nb * SUBLANES, w), F32),
            pltpu.VMEM((nb, 1, w), F32),
            pltpu.VMEM((r, w), F32),
            pltpu.VMEM((r, w), F32),
        ],
        compiler_params=_params("parallel", "arbitrary"),
    )(proj, proj, prev8, h0.reshape(nseq, 1, w), conv_w, conv_b.reshape(1, w), wa_t, ba.reshape(1, w), wx_t,
      bx.reshape(1, w), lam.reshape(1, w))
    return ya, hl.reshape(nseq, w)


def _gla_chunk(qs, k, v, lg, s_state, sub):
    l, dk = qs.shape
    row = _iota((l, l), 0)
    col = _iota((l, l), 1)
    bc = _dot_split((row >= col).astype(F32), lg)
    o = _dot(qs * jnp.exp(bc), s_state)
    rowi = _iota((l, 1), 0)
    att = jnp.zeros((l, l), F32)
    for j in range(l // sub - 1):
        end = (j + 1) * sub
        e_j = bc[end - 1:end, :]
        in_blk = jnp.logical_and(rowi >= j * sub, rowi < end)
        later = rowi >= end
        kp = jnp.where(in_blk, k * jnp.exp(jnp.where(in_blk, e_j - bc, 0.0)), 0.0)
        qp = jnp.where(later, qs * jnp.exp(jnp.where(later, bc - e_j, 0.0)), 0.0)
        att = att + _dot_nt(qp, kp)
    posr = rowi & (sub - 1)
    for d in range(sub):
        kd = k if d == 0 else pltpu.roll(k, d, 0)
        bcd = bc if d == 0 else pltpu.roll(bc, d, 0)
        valid = posr >= d
        prod = qs * kd * jnp.exp(jnp.where(valid, bc - bcd, 0.0))
        diag = jnp.sum(jnp.where(valid, prod, 0.0), axis=1, keepdims=True)
        att = att + jnp.where(col == row - d, diag, 0.0)
    o = o + _dot(att, v)
    bl = bc[l - 1:l, :]
    eye = _iota((dk, dk), 0) == _iota((dk, dk), 1)
    decay_col = jnp.sum(jnp.where(eye, jnp.exp(bl), 0.0), axis=1, keepdims=True)
    s_new = decay_col * s_state + _dot_tn(k * jnp.exp(bl - bc), v)
    return o, s_new


def _gla_kernel(q_ref, k_ref, v_ref, g_ref, lg_ref, s0_ref, gn_ref, yb_ref, sn_ref, s_scr, *, nb, chunk, sub):
    c = pl.program_id(2)
    scale = q_ref.shape[1] ** -0.5

    def run(rows, s_state):
        o, s_new = _gla_chunk(q_ref[rows, :] * scale, k_ref[rows, :], v_ref[rows, :], lg_ref[rows, :], s_state, sub)
        on = o * lax.rsqrt(jnp.mean(o * o, axis=-1, keepdims=True) + EPS) * gn_ref[...]
        yb_ref[rows, :] = on * jax.nn.silu(g_ref[rows, :])
        return s_new

    if nb == 1:
        @pl.when(c == 0)
        def _():
            s_scr[...] = s0_ref[0, 0]

        def body(i, carry):
            s_scr[...] = run(pl.ds(pl.multiple_of(i * chunk, chunk), chunk), s_scr[...])
            return carry

        lax.fori_loop(0, q_ref.shape[0] // chunk, body, 0)

        @pl.when(c == pl.num_programs(2) - 1)
        def _():
            sn_ref[0, 0] = s_scr[...]
    else:
        def body(j, carry):
            sn_ref[j, 0] = run(pl.ds(pl.multiple_of(j * chunk, chunk), chunk), s0_ref[j, 0])
            return carry

        lax.fori_loop(0, nb, body, 0)


def gla_branch(proj, lg, s0, gnorm, *, nseq, t, nb, rows_blk, chunk, sub, q_off, k_off, v_off, g_off):
    n = proj.shape[0]
    _, nh, dk, dv = s0.shape
    ntb = t // rows_blk if nb == 1 else 1
    r = rows_blk if nb == 1 else nb * t
    assert (nb == 1 and t % rows_blk == 0 and rows_blk % chunk == 0) or (chunk == t and nseq % nb == 0)
    rows = lambda s, h, c: s * ntb + c
    kern = functools.partial(_gla_kernel, nb=nb, chunk=chunk, sub=sub)
    yb, sn = pl.pallas_call(
        kern,
        grid=(nseq // nb, nh, ntb),
        in_specs=[
            pl.BlockSpec((r, dk), lambda s, h, c: (rows(s, h, c), q_off // dk + h)),
            pl.BlockSpec((r, dk), lambda s, h, c: (rows(s, h, c), k_off // dk + h)),
            pl.BlockSpec((r, dv), lambda s, h, c: (rows(s, h, c), v_off // dv + h)),
            pl.BlockSpec((r, dv), lambda s, h, c: (rows(s, h, c), g_off // dv + h)),
            pl.BlockSpec((r, dk), lambda s, h, c: (rows(s, h, c), h)),
            pl.BlockSpec((nb, 1, dk, dv), lambda s, h, c: (s, h, 0, 0)),
            pl.BlockSpec((1, dv), lambda s, h, c: (0, h)),
        ],
        out_specs=[
            pl.BlockSpec((r, dv), lambda s, h, c: (rows(s, h, c), h)),
            pl.BlockSpec((nb, 1, dk, dv), lambda s, h, c: (s, h, 0, 0)),
        ],
        out_shape=[jax.ShapeDtypeStruct((n, nh * dv), F32), jax.ShapeDtypeStruct(s0.shape, F32)],
        scratch_shapes=[pltpu.VMEM((dk, dv), F32)],
        compiler_params=_params("parallel", "parallel", "arbitrary"),
    )(proj, proj, proj, proj, lg, s0, gnorm.reshape(1, nh * dv))
    return yb, sn


def _mlstm_pre_kernel(xm_ref, prev_ref, cw_ref, cb_ref, wq_ref, wk_ref, wv_ref, wg_ref, bg_ref,
                      q_ref, k_ref, v_ref, xc_ref, gate_ref, prev_scr, *, nb, tc, n_heads):
    c = pl.program_id(1)

    @pl.when(c == 0)
    def _():
        prev_scr[...] = prev_ref[...]

    x = xm_ref[...]
    xc = jax.nn.silu(_causal_conv(x, prev_scr[...], cw_ref, cb_ref, tc))
    if nb == 1:
        prev_scr[...] = x[x.shape[0] - SUBLANES:]
    xc_ref[...] = xc
    di = x.shape[1]
    xcb = xc.astype(BF16)
    xb = x.astype(BF16)
    for t in range(di // MXU_WIDTH):
        sl = slice(t * MXU_WIDTH, (t + 1) * MXU_WIDTH)
        q_ref[:, sl] = jnp.dot(xcb[:, sl], wq_ref[t], preferred_element_type=F32)
        k_ref[:, sl] = jnp.dot(xcb[:, sl], wk_ref[t], preferred_element_type=F32)
        v_ref[:, sl] = jnp.dot(xb[:, sl], wv_ref[t], preferred_element_type=F32)
    pre = (_dot(q_ref[...], wg_ref[0:di, :]) + _dot(k_ref[...], wg_ref[di:2 * di, :])
           + _dot(v_ref[...], wg_ref[2 * di:3 * di, :]) + bg_ref[...])
    gate_ref[...] = jnp.where(_iota(pre.shape, 1) < n_heads, pre, _log_sigmoid(pre))


def mlstm_pre(up, prev8, conv_w, conv_b, wq_t, wk_t, wv_t, wg, bg, *, nseq, t, nb, tc, n_heads):
    n = up.shape[0]
    di = conv_w.shape[1]
    nchunk = t // tc
    assert nseq % nb == 0 and t % tc == 0 and (nb == 1 or tc == t == SUBLANES)
    r = nb * tc
    nblk = di // MXU_WIDTH
    rows = lambda s, c: (s * nchunk + c, 0)
    full2 = lambda s, c: (0, 0)
    full3 = lambda s, c: (0, 0, 0)
    wide = jax.ShapeDtypeStruct((n, di), F32)
    kern = functools.partial(_mlstm_pre_kernel, nb=nb, tc=tc, n_heads=n_heads)
    return pl.pallas_call(
        kern,
        grid=(nseq // nb, nchunk),
        in_specs=[
            pl.BlockSpec((r, di), rows),
            pl.BlockSpec((nb * SUBLANES, di), lambda s, c: (s, 0)),
            pl.BlockSpec((CONV_W, di), full2),
            pl.BlockSpec((1, di), full2),
            pl.BlockSpec((nblk, MXU_WIDTH, MXU_WIDTH), full3),
            pl.BlockSpec((nblk, MXU_WIDTH, MXU_WIDTH), full3),
            pl.BlockSpec((nblk, MXU_WIDTH, MXU_WIDTH), full3),
            pl.BlockSpec((3 * di, LANES), full2),
            pl.BlockSpec((1, LANES), full2),
        ],
        out_specs=[pl.BlockSpec((r, di), rows)] * 4 + [pl.BlockSpec((r, LANES), rows)],
        out_shape=[wide, wide, wide, wide, jax.ShapeDtypeStruct((n, LANES), F32)],
        scratch_shapes=[pltpu.VMEM((nb * SUBLANES, di), F32)],
        compiler_params=_params("parallel", "arbitrary"),
    )(up, prev8, conv_w, conv_b.reshape(1, di), wq_t, wk_t, wv_t, wg, bg)


def _mlstm_kernel(q_ref, k_ref, v_ref, gate_ref, xc_ref, z_ref, c0_ref, n0_ref, m0_ref, skip_ref, ng_ref,
                  out_ref, cn_ref, nn_ref, mn_ref, c_scr, n_scr, m_scr, *, chunk, n_heads):
    head = pl.program_id(1)
    c = pl.program_id(2)

    @pl.when(c == 0)
    def _():
        c_scr[...] = c0_ref[0, 0]
        n_scr[...] = n0_ref[0]
        m_scr[...] = m0_ref[0]

    dh = q_ref.shape[1]
    kscale = dh ** -0.5
    l = chunk
    row = _iota((l, l), 0)
    col = _iota((l, l), 1)
    tril = row >= col

    def body(i, carry):
        rows = pl.ds(pl.multiple_of(i * l, l), l)
        gates = gate_ref[rows, :]
        igc = _lane_column(gates, head)
        lfc = _lane_column(gates, head + n_heads)
        q = q_ref[rows, :]
        k = k_ref[rows, :] * kscale
        v = v_ref[rows, :]
        m_prev = m_scr[:, 0:1]
        n_prev = n_scr[...]
        f_b = jnp.broadcast_to(lfc, (l, l))
        i_b = jnp.broadcast_to(igc, (l, l))
        f_col = _dot_split(tril.astype(F32), f_b)
        row_term = _dot_split(jnp.ones((l, l), F32),
                              jnp.where(row == col, i_b, 0.0) - jnp.where(row <= col, f_b, 0.0))
        dm = jnp.where(tril, f_col + row_term, -jnp.inf)
        fcum = f_col[:, 0:1]
        prev = m_prev + fcum
        mt = jnp.maximum(prev, jnp.max(dm, axis=1, keepdims=True))
        wprev = jnp.exp(prev - mt)
        smat = _dot_nt(q, k) * jnp.exp(dm - mt)
        den = wprev * jnp.sum(q * n_prev, axis=1, keepdims=True) + jnp.sum(smat, axis=1, keepdims=True)
        inv = 1.0 / jnp.maximum(jnp.abs(den), jnp.exp(-mt))
        f_last = fcum[l - 1:l, :]
        m_last = mt[l - 1:l, :]
        w_c = jnp.exp(m_prev + f_last - m_last)
        kw = k * jnp.exp(f_last - fcum + igc - m_last)
        qb = q.astype(BF16)
        sb = smat.astype(BF16)
        kwb = kw.astype(BF16)
        parts = []
        for t in range(dh // MXU_WIDTH):
            sl = slice(t * MXU_WIDTH, (t + 1) * MXU_WIDTH)
            c_blk = c_scr[:, sl]
            vb = v[:, sl].astype(BF16)
            num = (wprev * jnp.dot(qb, c_blk.astype(BF16), preferred_element_type=F32)
                   + jnp.dot(sb, vb, preferred_element_type=F32))
            parts.append(num * inv)
            c_scr[:, sl] = w_c * c_blk + _dot_tn(kwb, vb)
        n_scr[...] = w_c * n_prev + jnp.sum(kw, axis=0, keepdims=True)
        m_scr[...] = jnp.broadcast_to(m_last, m_scr.shape)
        hh = jnp.concatenate(parts, axis=1)
        hc = hh - jnp.mean(hh, axis=-1, keepdims=True)
        hn = hc * lax.rsqrt(jnp.mean(hc * hc, axis=-1, keepdims=True) + EPS) * ng_ref[...]
        out_ref[rows, :] = (hn + skip_ref[...] * xc_ref[rows, :]) * jax.nn.silu(z_ref[rows, :])
        return carry

    lax.fori_loop(0, q_ref.shape[0] // l, body, 0)

    @pl.when(c == pl.num_programs(2) - 1)
    def _():
        cn_ref[0, 0] = c_scr[...]
        nn_ref[0] = n_scr[...]
        mn_ref[0] = m_scr[...]


def mlstm_recurrence(q, k, v, gates, xc, up, c0, n0, m0, skip, norm_g, *, nseq, t, rows_blk, chunk):
    n, di = q.shape
    _, nh, dh, _ = c0.shape
    ntb = t // rows_blk
    assert t % rows_blk == 0 and rows_blk % chunk == 0
    rows = lambda s, h, c: s * ntb + c
    per_head = lambda s, h, c: (s * nh + h, 0, 0)
    kern = functools.partial(_mlstm_kernel, chunk=chunk, n_heads=nh)
    n0r = n0.reshape(nseq * nh, 1, dh)
    m0r = jnp.broadcast_to(m0.reshape(nseq * nh, 1, 1), (nseq * nh, 1, LANES))
    out, cn, nn, mn = pl.pallas_call(
        kern,
        grid=(nseq, nh, ntb),
        in_specs=[
            pl.BlockSpec((rows_blk, dh), lambda s, h, c: (rows(s, h, c), h)),
            pl.BlockSpec((rows_blk, dh), lambda s, h, c: (rows(s, h, c), h)),
            pl.BlockSpec((rows_blk, dh), lambda s, h, c: (rows(s, h, c), h)),
            pl.BlockSpec((rows_blk, LANES), lambda s, h, c: (rows(s, h, c), 0)),
            pl.BlockSpec((rows_blk, dh), lambda s, h, c: (rows(s, h, c), h)),
            pl.BlockSpec((rows_blk, dh), lambda s, h, c: (rows(s, h, c), nh + h)),
            pl.BlockSpec((1, 1, dh, dh), lambda s, h, c: (s, h, 0, 0)),
            pl.BlockSpec((1, 1, dh), per_head),
            pl.BlockSpec((1, 1, LANES), per_head),
            pl.BlockSpec((1, dh), lambda s, h, c: (0, h)),
            pl.BlockSpec((1, dh), lambda s, h, c: (0, h)),
        ],
        out_specs=[
            pl.BlockSpec((rows_blk, dh), lambda s, h, c: (rows(s, h, c), h)),
            pl.BlockSpec((1, 1, dh, dh), lambda s, h, c: (s, h, 0, 0)),
            pl.BlockSpec((1, 1, dh), per_head),
            pl.BlockSpec((1, 1, LANES), per_head),
        ],
        out_shape=[
            jax.ShapeDtypeStruct((n, di), F32),
            jax.ShapeDtypeStruct(c0.shape, F32),
            jax.ShapeDtypeStruct((nseq * nh, 1, dh), F32),
            jax.ShapeDtypeStruct((nseq * nh, 1, LANES), F32),
        ],
        scratch_shapes=[pltpu.VMEM((dh, dh), F32), pltpu.VMEM((1, dh), F32), pltpu.VMEM((1, LANES), F32)],
        compiler_params=_params("parallel", "parallel", "arbitrary"),
    )(q, k, v, gates, xc, up, c0, n0r, m0r, skip.reshape(1, di), norm_g.reshape(1, di))
    return out, cn, nn.reshape(nseq, nh, dh), mn[:, 0, 0].reshape(nseq, nh)


def _router_kernel(x_ref, g_ref, wr_ref, br_ref, xn_ref, gates_ref):
    xn = _rmsnorm_rows(x_ref[...], g_ref[...])
    xn_ref[...] = xn.astype(BF16)
    logits = _dot(xn, wr_ref[...]) + br_ref[...]
    lane = _iota(logits.shape, 1)
    big = jnp.int32(LANES)
    is_g = jnp.logical_and(lane >= N_EXPERTS, lane < N_EXPERTS + N_GROUPS)
    gl = jnp.where(is_g, logits, -jnp.inf)
    gmax = jnp.max(gl, axis=1, keepdims=True)
    gsum = jnp.sum(jnp.where(is_g, jnp.exp(gl - gmax), 0.0), axis=1, keepdims=True)
    p_g = 1.0 / gsum
    g_idx = jnp.min(jnp.where(gl == gmax, lane, big), axis=1, keepdims=True) - N_EXPERTS
    sel = jnp.logical_and(lane < N_EXPERTS, (lane >> 2) == g_idx)
    el = jnp.where(sel, logits, -jnp.inf)
    emax = jnp.max(el, axis=1, keepdims=True)
    eexp = jnp.where(sel, jnp.exp(el - emax), 0.0)
    ep = eexp / jnp.sum(eexp, axis=1, keepdims=True)
    cand = jnp.where(sel, ep, -1.0)
    v1 = jnp.max(cand, axis=1, keepdims=True)
    idx1 = jnp.min(jnp.where(cand == v1, lane, big), axis=1, keepdims=True)
    cand2 = jnp.where(lane == idx1, -1.0, cand)
    v2 = jnp.max(cand2, axis=1, keepdims=True)
    idx2 = jnp.min(jnp.where(cand2 == v2, lane, big), axis=1, keepdims=True)
    tot = v1 + v2
    gates_ref[...] = (jnp.where(lane == idx1, v1 / tot * p_g, 0.0)
                      + jnp.where(lane == idx2, v2 / tot * p_g, 0.0))


def moe_router(x, gain, wr, br, tm=512):
    n, d = x.shape
    tm = _row_tile(n, tm)
    return pl.pallas_call(
        _router_kernel,
        grid=(n // tm,),
        in_specs=[
            pl.BlockSpec((tm, d), lambda i: (i, 0)),
            pl.BlockSpec((1, d), lambda i: (0, 0)),
            pl.BlockSpec((d, LANES), lambda i: (0, 0)),
            pl.BlockSpec((1, LANES), lambda i: (0, 0)),
        ],
        out_specs=[pl.BlockSpec((tm, d), lambda i: (i, 0)), pl.BlockSpec((tm, LANES), lambda i: (i, 0))],
        out_shape=[jax.ShapeDtypeStruct((n, d), BF16), jax.ShapeDtypeStruct((n, LANES), F32)],
        compiler_params=_params("parallel"),
    )(x, gain.reshape(1, d), wr, br)


def _moe_expert_kernel(*refs, has_final):
    xn_ref, gates_ref, wg_ref, wu_ref, wd_ref, x_ref = refs[:6]
    fg_ref = refs[6] if has_final else None
    o_ref, acc_scr = refs[6 + int(has_final)], refs[7 + int(has_final)]
    e = pl.program_id(1)

    @pl.when(e == 0)
    def _():
        acc_scr[...] = jnp.zeros_like(acc_scr)

    xn = xn_ref[...]
    hg = jnp.dot(xn, wg_ref[0].astype(BF16), preferred_element_type=F32)
    hu = jnp.dot(xn, wu_ref[0].astype(BF16), preferred_element_type=F32)
    h = jax.nn.silu(hg) * hu * _lane_column(gates_ref[...], e)
    acc_scr[...] += _dot(h, wd_ref[0])

    @pl.when(e == pl.num_programs(1) - 1)
    def _():
        y = x_ref[...] + acc_scr[...]
        if has_final:
            y = _rmsnorm_rows(y, fg_ref[...])
        o_ref[...] = y


def moe_experts(xn, gates, w_gate, w_up, w_down, x, final_gain=None, tm=512):
    n, d = x.shape
    ne, _, f = w_gate.shape
    tm = _row_tile(n, tm)
    in_specs = [
        pl.BlockSpec((tm, d), lambda i, e: (i, 0)),
        pl.BlockSpec((tm, LANES), lambda i, e: (i, 0)),
        pl.BlockSpec((1, d, f), lambda i, e: (e, 0, 0)),
        pl.BlockSpec((1, d, f), lambda i, e: (e, 0, 0)),
        pl.BlockSpec((1, f, d), lambda i, e: (e, 0, 0)),
        pl.BlockSpec((tm, d), lambda i, e: (i, 0)),
    ]
    args = [xn, gates, w_gate, w_up, w_down, x]
    if final_gain is not None:
        in_specs.append(pl.BlockSpec((1, d), lambda i, e: (0, 0)))
        args.append(final_gain.reshape(1, d))
    return pl.pallas_call(
        functools.partial(_moe_expert_kernel, has_final=final_gain is not None),
        grid=(n // tm, ne),
        in_specs=in_specs,
        out_specs=pl.BlockSpec((tm, d), lambda i, e: (i, 0)),
        out_shape=jax.ShapeDtypeStruct((n, d), F32),
        scratch_shapes=[pltpu.VMEM((tm, d), F32)],
        compiler_params=_params("parallel", "arbitrary"),
    )(*args)


def _blockdiag_tiles(w):
    nblocks, bi, bo = w.shape
    per = MXU_WIDTH // bi
    w4 = w.reshape(nblocks // per, per, bi, bo)
    dense = jnp.einsum('taio,ab->taibo', w4, jnp.eye(per, dtype=w.dtype))
    return dense.reshape(nblocks // per, MXU_WIDTH, MXU_WIDTH).astype(BF16)


def _pad_cols(w, width=LANES):
    return jnp.pad(w, ((0, 0), (0, width - w.shape[1])))


def _history_tiles(buf):
    nseq, hist, ch = buf.shape
    return jnp.pad(buf, ((0, 0), (SUBLANES - hist, 0), (0, 0))).reshape(nseq * SUBLANES, ch)


def _moe_layer(x, gain, w_rg, b_rg, w_re, b_re, w_gate, w_up, w_down, final_gain=None):
    wr = _pad_cols(jnp.concatenate([w_re, w_rg], axis=1)).astype(BF16)
    br = _pad_cols(jnp.concatenate([b_re, b_rg])[None, :])
    xn, gates = moe_router(x, gain, wr, br)
    return moe_experts(xn, gates, w_gate, w_up, w_down, x, final_gain)


def _trunk(x3, st_lru_conv, st_lru_h, st_gla_s, st_m_conv, st_m_c, st_m_n, st_m_m,
           norm_mix_g, norm_ffn_g, norm_final_g,
           l0_w_in, l0_lru_conv_w, l0_lru_conv_b, l0_lru_wa, l0_lru_ba, l0_lru_wx, l0_lru_bx, l0_lru_lam,
           l0_gla_wa2, l0_gla_ba2, l0_gla_norm_g, l0_w_out,
           l1_w_up, l1_conv_w, l1_conv_b, l1_wq, l1_wk, l1_wv, l1_w_ig, l1_b_ig, l1_w_fg, l1_b_fg, l1_skip,
           l1_norm_g, l1_w_down,
           moe_w_rg, moe_b_rg, moe_w_re, moe_b_re, moe_w_gate, moe_w_up, moe_w_down):
    nseq, t, d = x3.shape
    x = x3.reshape(nseq * t, d)
    depth = norm_mix_g.shape[0]
    single = t > SUBLANES
    hist = CONV_W - 1
    outs = {k: [] for k in ("lru_conv", "lru_h", "gla_s", "m_conv", "m_c", "m_n", "m_m")}
    for layer in range(depth):
        j = layer // 2
        if layer % 2 == 0:
            w = st_lru_h.shape[-1]
            _, _, nh, dk, dv = st_gla_s.shape
            rank = l0_gla_wa2.shape[1]
            main = 2 * w + 2 * nh * dk + 2 * nh * dv
            proj = fused_linear([x], l0_w_in[j], n_out=main, gain=norm_mix_g[layer])
            lg = gla_decay(x, norm_mix_g[layer], _pad_cols(l0_w_in[j][:, main:main + rank]).astype(BF16),
                           jnp.pad(l0_gla_wa2[j], ((0, LANES - rank), (0, 0))).astype(BF16), l0_gla_ba2[j])
            ya, h_last = lru_branch(
                proj, _history_tiles(st_lru_conv[j]), st_lru_h[j], l0_lru_conv_w[j], l0_lru_conv_b[j],
                _blockdiag_tiles(l0_lru_wa[j]), l0_lru_ba[j], _blockdiag_tiles(l0_lru_wx[j]), l0_lru_bx[j],
                l0_lru_lam[j], nseq=nseq, t=t, nb=1 if single else min(nseq, 64), tc=min(t, 256))
            yb, s_new = gla_branch(
                proj, lg, st_gla_s[j], l0_gla_norm_g[j], nseq=nseq, t=t, nb=1 if single else min(nseq, 16),
                rows_blk=min(t, 512), chunk=min(t, GLA_CHUNK), sub=min(t, GLA_SUB),
                q_off=2 * w, k_off=2 * w + nh * dk, v_off=2 * w + 2 * nh * dk, g_off=2 * w + 2 * nh * dk + nh * dv)
            x = fused_linear([ya, yb], l0_w_out[j], n_out=d, res=x)
            outs["lru_conv"].append(proj.reshape(nseq, t, main)[:, t - hist:, :w])
            outs["lru_h"].append(h_last)
            outs["gla_s"].append(s_new)
        else:
            _, _, nh, dh, _ = st_m_c.shape
            di = nh * dh
            up = fused_linear([x], l1_w_up[j], n_out=2 * di, gain=norm_mix_g[layer])
            wg = _pad_cols(jnp.concatenate([l1_w_ig[j], l1_w_fg[j]], axis=1)).astype(BF16)
            bg = _pad_cols(jnp.concatenate([l1_b_ig[j], l1_b_fg[j]])[None, :])
            q, k, v, xc, gates = mlstm_pre(
                up, _history_tiles(st_m_conv[j]), l1_conv_w[j], l1_conv_b[j], _blockdiag_tiles(l1_wq[j]),
                _blockdiag_tiles(l1_wk[j]), _blockdiag_tiles(l1_wv[j]), wg, bg,
                nseq=nseq, t=t, nb=1 if single else min(nseq, 16), tc=min(t, 128), n_heads=nh)
            hout, c_new, n_new, m_new = mlstm_recurrence(
                q, k, v, gates, xc, up, st_m_c[j], st_m_n[j], st_m_m[j], l1_skip[j], l1_norm_g[j],
                nseq=nseq, t=t, rows_blk=min(t, 256), chunk=min(t, M_CHUNK))
            x = fused_linear([hout], l1_w_down[j], n_out=d, res=x)
            outs["m_conv"].append(up.reshape(nseq, t, 2 * di)[:, t - hist:, :di])
            outs["m_c"].append(c_new)
            outs["m_n"].append(n_new)
            outs["m_m"].append(m_new)
        x = _moe_layer(x, norm_ffn_g[layer], moe_w_rg[layer], moe_b_rg[layer], moe_w_re[layer], moe_b_re[layer],
                       moe_w_gate[layer], moe_w_up[layer], moe_w_down[layer],
                       final_gain=norm_final_g if layer == depth - 1 else None)
    return (x.reshape(nseq, t, d),) + tuple(jnp.stack(outs[k]) for k in
                                             ("lru_conv", "lru_h", "gla_s", "m_conv", "m_c", "m_n", "m_m"))


def kernel(x_prompt, x_sample, state_lru_conv, state_lru_h, state_gla_S, state_mlstm_conv, state_mlstm_C,
           state_mlstm_n, state_mlstm_m, norm_mix_g, norm_ffn_g, norm_final_g, l0_w_in, l0_lru_conv_w,
           l0_lru_conv_b, l0_lru_wa, l0_lru_ba, l0_lru_wx, l0_lru_bx, l0_lru_lam, l0_gla_wa2, l0_gla_ba2,
           l0_gla_norm_g, l0_w_out, l1_w_up, l1_conv_w, l1_conv_b, l1_wq, l1_wk, l1_wv, l1_w_ig, l1_b_ig,
           l1_w_fg, l1_b_fg, l1_skip, l1_norm_g, l1_w_down, moe_w_rg, moe_b_rg, moe_w_re, moe_b_re, moe_w_gate,
           moe_w_up, moe_w_down):
    weights = (norm_mix_g, norm_ffn_g, norm_final_g, l0_w_in, l0_lru_conv_w, l0_lru_conv_b, l0_lru_wa, l0_lru_ba,
               l0_lru_wx, l0_lru_bx, l0_lru_lam, l0_gla_wa2, l0_gla_ba2, l0_gla_norm_g, l0_w_out, l1_w_up,
               l1_conv_w, l1_conv_b, l1_wq, l1_wk, l1_wv, l1_w_ig, l1_b_ig, l1_w_fg, l1_b_fg, l1_skip, l1_norm_g,
               l1_w_down, moe_w_rg, moe_b_rg, moe_w_re, moe_b_re, moe_w_gate, moe_w_up, moe_w_down)
    states = (state_lru_conv, state_lru_h, state_gla_S, state_mlstm_conv, state_mlstm_C, state_mlstm_n,
              state_mlstm_m)
    bp = x_prompt.shape[0]
    zero_states = tuple(jnp.zeros((s.shape[0], bp) + s.shape[2:], s.dtype) for s in states)
    prompt = _trunk(x_prompt, *zero_states, *weights)
    sample = _trunk(x_sample, *states, *weights)
    return (prompt[0], sample[0]) + prompt[1:] + sample[1:]
```

```python
import functools

import jax
import jax.numpy as jnp
from jax import lax
from jax.experimental import pallas as pl
from jax.experimental.pallas import tpu as pltpu

EPS = 1e-6
CONV_W = 4
LRU_C = 8.0
GLA_TAU = 16.0
GLA_CHUNK = 64
GLA_SUB = 16
M_CHUNK = 256
N_GROUPS = 4
E_PER_GROUP = 4
N_EXPERTS = N_GROUPS * E_PER_GROUP

V7X_VMEM_BYTES = 64 * 1024 * 1024
VMEM_LIMIT_BYTES = V7X_VMEM_BYTES - 8 * 1024 * 1024
SUBLANES = 8
LANES = 128
MXU_WIDTH = 256

F32 = jnp.float32
BF16 = jnp.bfloat16


def _params(*semantics):
    return pltpu.CompilerParams(dimension_semantics=semantics, vmem_limit_bytes=VMEM_LIMIT_BYTES)


def _dot(a, b):
    return jnp.dot(a.astype(BF16), b.astype(BF16), preferred_element_type=F32)


def _dot_nt(a, b):
    return lax.dot_general(a.astype(BF16), b.astype(BF16), (((1,), (1,)), ((), ())), preferred_element_type=F32)


def _dot_tn(a, b):
    return lax.dot_general(a.astype(BF16), b.astype(BF16), (((0,), (0,)), ((), ())), preferred_element_type=F32)


def _dot_split(m01, y):
    y_hi = y.astype(BF16)
    y_lo = (y - y_hi.astype(F32)).astype(BF16)
    m = m01.astype(BF16)
    return jnp.dot(m, y_hi, preferred_element_type=F32) + jnp.dot(m, y_lo, preferred_element_type=F32)


def _softplus(x):
    return jnp.maximum(x, 0.0) + jnp.log1p(jnp.exp(-jnp.abs(x)))


def _log_sigmoid(x):
    return -_softplus(-x)


def _rmsnorm_rows(x, g):
    return x * lax.rsqrt(jnp.mean(x * x, axis=-1, keepdims=True) + EPS) * g


def _iota(shape, dim):
    return lax.broadcasted_iota(jnp.int32, shape, dim)


def _lane_column(x, lane_index):
    return jnp.sum(jnp.where(_iota(x.shape, 1) == lane_index, x, 0.0), axis=1, keepdims=True)


def _row_tile(n, target):
    t = min(n, target)
    assert n % t == 0
    return t


def _linear_kernel(*refs, n_lhs, has_norm, has_res):
    lhs_refs = refs[:n_lhs]
    pos = n_lhs
    g_ref = refs[pos] if has_norm else None
    pos += int(has_norm)
    w_ref = refs[pos]
    pos += 1
    res_ref = refs[pos] if has_res else None
    pos += int(has_res)
    o_ref, lhs_scr = refs[pos], refs[pos + 1]

    @pl.when(pl.program_id(1) == 0)
    def _():
        off = 0
        for a_ref in lhs_refs:
            a = a_ref[...]
            if has_norm:
                a = _rmsnorm_rows(a, g_ref[...])
            lhs_scr[:, off:off + a.shape[1]] = a.astype(BF16)
            off += a.shape[1]

    acc = jnp.dot(lhs_scr[...], w_ref[...].astype(BF16), preferred_element_type=F32)
    if has_res:
        acc = res_ref[...] + acc
    o_ref[...] = acc


def fused_linear(lhs_list, w, *, n_out, gain=None, res=None, tm=1024, tn=1024, name="linear"):
    n = lhs_list[0].shape[0]
    ks = [a.shape[1] for a in lhs_list]
    ktot = sum(ks)
    assert w.shape[0] == ktot
    tm = _row_tile(n, tm)
    tn = _row_tile(n_out, tn)
    in_specs = [pl.BlockSpec((tm, k), lambda i, j: (i, 0)) for k in ks]
    args = list(lhs_list)
    if gain is not None:
        in_specs.append(pl.BlockSpec((1, ktot), lambda i, j: (0, 0)))
        args.append(gain.reshape(1, ktot))
    in_specs.append(pl.BlockSpec((ktot, tn), lambda i, j: (0, j)))
    args.append(w)
    if res is not None:
        in_specs.append(pl.BlockSpec((tm, tn), lambda i, j: (i, j)))
        args.append(res)
    kern = functools.partial(_linear_kernel, n_lhs=len(lhs_list), has_norm=gain is not None, has_res=res is not None)
    return pl.pallas_call(
        kern,
        grid=(n // tm, n_out // tn),
        in_specs=in_specs,
        out_specs=pl.BlockSpec((tm, tn), lambda i, j: (i, j)),
        out_shape=jax.ShapeDtypeStruct((n, n_out), F32),
        scratch_shapes=[pltpu.VMEM((tm, ktot), BF16)],
        compiler_params=_params("parallel", "arbitrary"),
        name=name,
    )(*args)


def _gla_decay_kernel(x_ref, g_ref, walr_ref, wa2_ref, ba2_ref, o_ref):
    xn = _rmsnorm_rows(x_ref[...], g_ref[...])
    alr = _dot(xn, walr_ref[...])
    o_ref[...] = _log_sigmoid(_dot(alr, wa2_ref[...]) + ba2_ref[...]) * (1.0 / GLA_TAU)


def gla_decay(x, gain, w_alr_pad, wa2_pad, ba2, tm=512):
    n, d = x.shape
    hk = wa2_pad.shape[1]
    tm = _row_tile(n, tm)
    return pl.pallas_call(
        _gla_decay_kernel,
        grid=(n // tm,),
        in_specs=[
            pl.BlockSpec((tm, d), lambda i: (i, 0)),
            pl.BlockSpec((1, d), lambda i: (0, 0)),
            pl.BlockSpec((d, LANES), lambda i: (0, 0)),
            pl.BlockSpec((LANES, hk), lambda i: (0, 0)),
            pl.BlockSpec((1, hk), lambda i: (0, 0)),
        ],
        out_specs=pl.BlockSpec((tm, hk), lambda i: (i, 0)),
        out_shape=jax.ShapeDtypeStruct((n, hk), F32),
        compiler_params=_params("parallel"),
        name="gla_decay",
    )(x, gain.reshape(1, d), w_alr_pad, wa2_pad, ba2.reshape(1, hk))


def _causal_conv(x, prev, w_ref, b_ref, seq_rows):
    r = x.shape[0]
    row = _iota((r, 1), 0)
    per_seq = prev.shape[0] == r
    pos = row & (seq_rows - 1) if per_seq else row
    acc = b_ref[...] + x * w_ref[CONV_W - 1:CONV_W, :]
    for j in range(1, CONV_W):
        if per_seq:
            hist = pltpu.roll(prev, (j - SUBLANES) % r, 0)
        else:
            hist = jnp.concatenate([pltpu.roll(prev, j, 0), x[SUBLANES:]], axis=0)
        shifted = jnp.where(pos < j, hist, pltpu.roll(x, j, 0))
        acc = acc + shifted * w_ref[CONV_W - 1 - j:CONV_W - j, :]
    return acc


def _lru_kernel(xa_ref, ga_ref, prev_ref, h0_ref, cw_ref, cb_ref, wa_ref, ba_ref, wx_ref, bx_ref, lam_ref,
                ya_ref, hl_ref, prev_scr, h_scr, a_scr, b_scr, *, nb, tc):
    c = pl.program_id(1)

    @pl.when(c == 0)
    def _():
        prev_scr[...] = prev_ref[...]
        h_scr[...] = h0_ref[...]

    x = xa_ref[...]
    xc = _causal_conv(x, prev_scr[...], cw_ref, cb_ref, tc)
    if nb == 1:
        prev_scr[...] = x[x.shape[0] - SUBLANES:]
    nblk = x.shape[1] // MXU_WIDTH
    xcb = xc.astype(BF16)
    r_parts, i_parts = [], []
    for t in range(nblk):
        sl = slice(t * MXU_WIDTH, (t + 1) * MXU_WIDTH)
        r_parts.append(jnp.dot(xcb[:, sl], wa_ref[t], preferred_element_type=F32))
        i_parts.append(jnp.dot(xcb[:, sl], wx_ref[t], preferred_element_type=F32))
    r_gate = jax.nn.sigmoid(jnp.concatenate(r_parts, axis=1) + ba_ref[...])
    i_gate = jax.nn.sigmoid(jnp.concatenate(i_parts, axis=1) + bx_ref[...])
    log_a = (-LRU_C) * r_gate * _softplus(-lam_ref[...])
    a = jnp.exp(log_a)
    a_scr[...] = a
    b_scr[...] = jnp.sqrt(-jnp.tanh(log_a) * (a * a + 1.0)) * (i_gate * xc)

    def seq_body(s, carry):
        def tile_body(tl, h):
            base = pl.multiple_of(s * tc + tl * SUBLANES, SUBLANES)
            for i in range(SUBLANES):
                h = a_scr[pl.ds(base + i, 1), :] * h + b_scr[pl.ds(base + i, 1), :]
                b_scr[pl.ds(base + i, 1), :] = h
            return h

        h_scr[s] = lax.fori_loop(0, tc // SUBLANES, tile_body, h_scr[s])
        return carry

    lax.fori_loop(0, nb, seq_body, 0, unroll=min(nb, 4))
    ya_ref[...] = b_scr[...] * jax.nn.gelu(ga_ref[...])

    @pl.when(c == pl.num_programs(1) - 1)
    def _():
        hl_ref[...] = h_scr[...]


def lru_branch(proj, prev8, h0, conv_w, conv_b, wa_t, ba, wx_t, bx, lam, *, nseq, t, nb, tc):
    n = proj.shape[0]
    w = h0.shape[1]
    nchunk = t // tc
    assert nseq % nb == 0 and t % tc == 0 and (nb == 1 or tc == t == SUBLANES)
    r = nb * tc
    nblk = w // MXU_WIDTH
    rows = lambda s, c: s * nchunk + c
    full2 = lambda s, c: (0, 0)
    full3 = lambda s, c: (0, 0, 0)
    kern = functools.partial(_lru_kernel, nb=nb, tc=tc)
    ya, hl = pl.pallas_call(
        kern,
        grid=(nseq // nb, nchunk),
        in_specs=[
            pl.BlockSpec((r, w), lambda s, c: (rows(s, c), 0)),
            pl.BlockSpec((r, w), lambda s, c: (rows(s, c), 1)),
            pl.BlockSpec((nb * SUBLANES, w), lambda s, c: (s, 0)),
            pl.BlockSpec((nb, 1, w), lambda s, c: (s, 0, 0)),
            pl.BlockSpec((CONV_W, w), full2),
            pl.BlockSpec((1, w), full2),
            pl.BlockSpec((nblk, MXU_WIDTH, MXU_WIDTH), full3),
            pl.BlockSpec((1, w), full2),
            pl.BlockSpec((nblk, MXU_WIDTH, MXU_WIDTH), full3),
            pl.BlockSpec((1, w), full2),
            pl.BlockSpec((1, w), full2),
        ],
        out_specs=[
            pl.BlockSpec((r, w), lambda s, c: (rows(s, c), 0)),
            pl.BlockSpec((nb, 1, w), lambda s, c: (s, 0, 0)),
        ],
        out_shape=[jax.ShapeDtypeStruct((n, w), F32), jax.ShapeDtypeStruct((nseq, 1, w), F32)],
        scratch_shapes=[
            pltpu.VMEM((nb * SUBLANES, w), F32),
            pltpu.VMEM((nb, 1, w), F32),
            pltpu.VMEM((r, w), F32),
            pltpu.VMEM((r, w), F32),
        ],
        compiler_params=_params("parallel", "arbitrary"),
        name="lru_branch",
    )(proj, proj, prev8, h0.reshape(nseq, 1, w), conv_w, conv_b.reshape(1, w), wa_t, ba.reshape(1, w), wx_t,
      bx.reshape(1, w), lam.reshape(1, w))
    return ya, hl.reshape(nseq, w)


def _gla_chunk(qs, k, v, lg, s_state, sub):
    l, dk = qs.shape
    row = _iota((l, l), 0)
    col = _iota((l, l), 1)
    bc = _dot_split((row >= col).astype(F32), lg)
    o = _dot(qs * jnp.exp(bc), s_state)
    rowi = _iota((l, 1), 0)
    att = jnp.zeros((l, l), F32)
    for j in range(l // sub - 1):
        end = (j + 1) * sub
        e_j = bc[end - 1:end, :]
        in_blk = jnp.logical_and(rowi >= j * sub, rowi < end)
        later = rowi >= end
        kp = jnp.where(in_blk, k * jnp.exp(jnp.where(in_blk, e_j - bc, 0.0)), 0.0)
        qp = jnp.where(later, qs * jnp.exp(jnp.where(later, bc - e_j, 0.0)), 0.0)
        att = att + _dot_nt(qp, kp)
    posr = rowi & (sub - 1)
    for d in range(sub):
        kd = k if d == 0 else pltpu.roll(k, d, 0)
        bcd = bc if d == 0 else pltpu.roll(bc, d, 0)
        valid = posr >= d
        prod = qs * kd * jnp.exp(jnp.where(valid, bc - bcd, 0.0))
        diag = jnp.sum(jnp.where(valid, prod, 0.0), axis=1, keepdims=True)
        att = att + jnp.where(col == row - d, diag, 0.0)
    o = o + _dot(att, v)
    bl = bc[l - 1:l, :]
    eye = _iota((dk, dk), 0) == _iota((dk, dk), 1)
    decay_col = jnp.sum(jnp.where(eye, jnp.exp(bl), 0.0), axis=1, keepdims=True)
    s_new = decay_col * s_state + _dot_tn(k * jnp.exp(bl - bc), v)
    return o, s_new


def _gla_kernel(q_ref, k_ref, v_ref, g_ref, lg_ref, s0_ref, gn_ref, yb_ref, sn_ref, s_scr, *, nb, chunk, sub):
    c = pl.program_id(2)
    scale = q_ref.shape[1] ** -0.5

    def run(rows, s_state):
        o, s_new = _gla_chunk(q_ref[rows, :] * scale, k_ref[rows, :], v_ref[rows, :], lg_ref[rows, :], s_state, sub)
        on = o * lax.rsqrt(jnp.mean(o * o, axis=-1, keepdims=True) + EPS) * gn_ref[...]
        yb_ref[rows, :] = on * jax.nn.silu(g_ref[rows, :])
        return s_new

    if nb == 1:
        @pl.when(c == 0)
        def _():
            s_scr[...] = s0_ref[0, 0]

        def body(i, carry):
            s_scr[...] = run(pl.ds(pl.multiple_of(i * chunk, chunk), chunk), s_scr[...])
            return carry

        lax.fori_loop(0, q_ref.shape[0] // chunk, body, 0, unroll=2)

        @pl.when(c == pl.num_programs(2) - 1)
        def _():
            sn_ref[0, 0] = s_scr[...]
    else:
        def body(j, carry):
            sn_ref[j, 0] = run(pl.ds(pl.multiple_of(j * chunk, chunk), chunk), s0_ref[j, 0])
            return carry

        lax.fori_loop(0, nb, body, 0, unroll=min(nb, 4))


def gla_branch(proj, lg, s0, gnorm, *, nseq, t, nb, rows_blk, chunk, sub, q_off, k_off, v_off, g_off):
    n = proj.shape[0]
    _, nh, dk, dv = s0.shape
    ntb = t // rows_blk if nb == 1 else 1
    r = rows_blk if nb == 1 else nb * t
    assert (nb == 1 and t % rows_blk == 0 and rows_blk % chunk == 0) or (chunk == t and nseq % nb == 0)
    rows = lambda s, h, c: s * ntb + c
    kern = functools.partial(_gla_kernel, nb=nb, chunk=chunk, sub=sub)
    yb, sn = pl.pallas_call(
        kern,
        grid=(nseq // nb, nh, ntb),
        in_specs=[
            pl.BlockSpec((r, dk), lambda s, h, c: (rows(s, h, c), q_off // dk + h)),
            pl.BlockSpec((r, dk), lambda s, h, c: (rows(s, h, c), k_off // dk + h)),
            pl.BlockSpec((r, dv), lambda s, h, c: (rows(s, h, c), v_off // dv + h)),
            pl.BlockSpec((r, dv), lambda s, h, c: (rows(s, h, c), g_off // dv + h)),
            pl.BlockSpec((r, dk), lambda s, h, c: (rows(s, h, c), h)),
            pl.BlockSpec((nb, 1, dk, dv), lambda s, h, c: (s, h, 0, 0)),
            pl.BlockSpec((1, dv), lambda s, h, c: (0, h)),
        ],
        out_specs=[
            pl.BlockSpec((r, dv), lambda s, h, c: (rows(s, h, c), h)),
            pl.BlockSpec((nb, 1, dk, dv), lambda s, h, c: (s, h, 0, 0)),
        ],
        out_shape=[jax.ShapeDtypeStruct((n, nh * dv), F32), jax.ShapeDtypeStruct(s0.shape, F32)],
        scratch_shapes=[pltpu.VMEM((dk, dv), F32)],
        compiler_params=_params("parallel", "parallel", "arbitrary"),
        name="gla_branch",
    )(proj, proj, proj, proj, lg, s0, gnorm.reshape(1, nh * dv))
    return yb, sn


def _mlstm_pre_kernel(xm_ref, prev_ref, cw_ref, cb_ref, wq_ref, wk_ref, wv_ref, wg_ref, bg_ref,
                      q_ref, k_ref, v_ref, xc_ref, gate_ref, prev_scr, *, nb, tc, n_heads):
    c = pl.program_id(1)

    @pl.when(c == 0)
    def _():
        prev_scr[...] = prev_ref[...]

    x = xm_ref[...]
    xc = jax.nn.silu(_causal_conv(x, prev_scr[...], cw_ref, cb_ref, tc))
    if nb == 1:
        prev_scr[...] = x[x.shape[0] - SUBLANES:]
    xc_ref[...] = xc
    di = x.shape[1]
    xcb = xc.astype(BF16)
    xb = x.astype(BF16)
    for t in range(di // MXU_WIDTH):
        sl = slice(t * MXU_WIDTH, (t + 1) * MXU_WIDTH)
        q_ref[:, sl] = jnp.dot(xcb[:, sl], wq_ref[t], preferred_element_type=F32)
        k_ref[:, sl] = jnp.dot(xcb[:, sl], wk_ref[t], preferred_element_type=F32)
        v_ref[:, sl] = jnp.dot(xb[:, sl], wv_ref[t], preferred_element_type=F32)
    pre = (_dot(q_ref[...], wg_ref[0:di, :]) + _dot(k_ref[...], wg_ref[di:2 * di, :])
           + _dot(v_ref[...], wg_ref[2 * di:3 * di, :]) + bg_ref[...])
    gate_ref[...] = jnp.where(_iota(pre.shape, 1) < n_heads, pre, _log_sigmoid(pre))


def mlstm_pre(up, prev8, conv_w, conv_b, wq_t, wk_t, wv_t, wg, bg, *, nseq, t, nb, tc, n_heads):
    n = up.shape[0]
    di = conv_w.shape[1]
    nchunk = t // tc
    assert nseq % nb == 0 and t % tc == 0 and (nb == 1 or tc == t == SUBLANES)
    r = nb * tc
    nblk = di // MXU_WIDTH
    rows = lambda s, c: (s * nchunk + c, 0)
    full2 = lambda s, c: (0, 0)
    full3 = lambda s, c: (0, 0, 0)
    wide = jax.ShapeDtypeStruct((n, di), F32)
    kern = functools.partial(_mlstm_pre_kernel, nb=nb, tc=tc, n_heads=n_heads)
    return pl.pallas_call(
        kern,
        grid=(nseq // nb, nchunk),
        in_specs=[
            pl.BlockSpec((r, di), rows),
            pl.BlockSpec((nb * SUBLANES, di), lambda s, c: (s, 0)),
            pl.BlockSpec((CONV_W, di), full2),
            pl.BlockSpec((1, di), full2),
            pl.BlockSpec((nblk, MXU_WIDTH, MXU_WIDTH), full3),
            pl.BlockSpec((nblk, MXU_WIDTH, MXU_WIDTH), full3),
            pl.BlockSpec((nblk, MXU_WIDTH, MXU_WIDTH), full3),
            pl.BlockSpec((3 * di, LANES), full2),
            pl.BlockSpec((1, LANES), full2),
        ],
        out_specs=[pl.BlockSpec((r, di), rows)] * 4 + [pl.BlockSpec((r, LANES), rows)],
        out_shape=[wide, wide, wide, wide, jax.ShapeDtypeStruct((n, LANES), F32)],
        scratch_shapes=[pltpu.VMEM((nb * SUBLANES, di), F32)],
        compiler_params=_params("parallel", "arbitrary"),
        name="mlstm_pre",
    )(up, prev8, conv_w, conv_b.reshape(1, di), wq_t, wk_t, wv_t, wg, bg)


def _mlstm_kernel(q_ref, k_ref, v_ref, gate_ref, xc_ref, z_ref, c0_ref, n0_ref, m0_ref, skip_ref, ng_ref,
                  out_ref, cn_ref, nn_ref, mn_ref, c_scr, n_scr, m_scr, *, chunk, n_heads):
    head = pl.program_id(1)
    c = pl.program_id(2)

    @pl.when(c == 0)
    def _():
        c_scr[...] = c0_ref[0, 0]
        n_scr[...] = n0_ref[0]
        m_scr[...] = m0_ref[0]

    dh = q_ref.shape[1]
    kscale = dh ** -0.5
    l = chunk
    row = _iota((l, l), 0)
    col = _iota((l, l), 1)
    tril = row >= col

    def body(i, carry):
        rows = pl.ds(pl.multiple_of(i * l, l), l)
        gates = gate_ref[rows, :]
        igc = _lane_column(gates, head)
        lfc = _lane_column(gates, head + n_heads)
        q = q_ref[rows, :]
        k = k_ref[rows, :] * kscale
        v = v_ref[rows, :]
        m_prev = m_scr[:, 0:1]
        n_prev = n_scr[...]
        f_b = jnp.broadcast_to(lfc, (l, l))
        i_b = jnp.broadcast_to(igc, (l, l))
        f_col = _dot_split(tril.astype(F32), f_b)
        row_term = _dot_split(jnp.ones((l, l), F32),
                              jnp.where(row == col, i_b, 0.0) - jnp.where(row <= col, f_b, 0.0))
        dm = jnp.where(tril, f_col + row_term, -jnp.inf)
        fcum = f_col[:, 0:1]
        prev = m_prev + fcum
        mt = jnp.maximum(prev, jnp.max(dm, axis=1, keepdims=True))
        wprev = jnp.exp(prev - mt)
        smat = _dot_nt(q, k) * jnp.exp(dm - mt)
        den = wprev * jnp.sum(q * n_prev, axis=1, keepdims=True) + jnp.sum(smat, axis=1, keepdims=True)
        inv = 1.0 / jnp.maximum(jnp.abs(den), jnp.exp(-mt))
        f_last = fcum[l - 1:l, :]
        m_last = mt[l - 1:l, :]
        w_c = jnp.exp(m_prev + f_last - m_last)
        kw = k * jnp.exp(f_last - fcum + igc - m_last)
        qb = q.astype(BF16)
        sb = smat.astype(BF16)
        kwb = kw.astype(BF16)
        parts = []
        for t in range(dh // MXU_WIDTH):
            sl = slice(t * MXU_WIDTH, (t + 1) * MXU_WIDTH)
            c_blk = c_scr[:, sl]
            vb = v[:, sl].astype(BF16)
            num = (wprev * jnp.dot(qb, c_blk.astype(BF16), preferred_element_type=F32)
                   + jnp.dot(sb, vb, preferred_element_type=F32))
            parts.append(num * inv)
            c_scr[:, sl] = w_c * c_blk + _dot_tn(kwb, vb)
        n_scr[...] = w_c * n_prev + jnp.sum(kw, axis=0, keepdims=True)
        m_scr[...] = jnp.broadcast_to(m_last, m_scr.shape)
        hh = jnp.concatenate(parts, axis=1)
        hc = hh - jnp.mean(hh, axis=-1, keepdims=True)
        hn = hc * lax.rsqrt(jnp.mean(hc * hc, axis=-1, keepdims=True) + EPS) * ng_ref[...]
        out_ref[rows, :] = (hn + skip_ref[...] * xc_ref[rows, :]) * jax.nn.silu(z_ref[rows, :])
        return carry

    lax.fori_loop(0, q_ref.shape[0] // l, body, 0)

    @pl.when(c == pl.num_programs(2) - 1)
    def _():
        cn_ref[0, 0] = c_scr[...]
        nn_ref[0] = n_scr[...]
        mn_ref[0] = m_scr[...]


def mlstm_recurrence(q, k, v, gates, xc, up, c0, n0, m0, skip, norm_g, *, nseq, t, rows_blk, chunk):
    n, di = q.shape
    _, nh, dh, _ = c0.shape
    ntb = t // rows_blk
    assert t % rows_blk == 0 and rows_blk % chunk == 0
    rows = lambda s, h, c: s * ntb + c
    per_head = lambda s, h, c: (s * nh + h, 0, 0)
    kern = functools.partial(_mlstm_kernel, chunk=chunk, n_heads=nh)
    n0r = n0.reshape(nseq * nh, 1, dh)
    m0r = jnp.broadcast_to(m0.reshape(nseq * nh, 1, 1), (nseq * nh, 1, LANES))
    out, cn, nn, mn = pl.pallas_call(
        kern,
        grid=(nseq, nh, ntb),
        in_specs=[
            pl.BlockSpec((rows_blk, dh), lambda s, h, c: (rows(s, h, c), h)),
            pl.BlockSpec((rows_blk, dh), lambda s, h, c: (rows(s, h, c), h)),
            pl.BlockSpec((rows_blk, dh), lambda s, h, c: (rows(s, h, c), h)),
            pl.BlockSpec((rows_blk, LANES), lambda s, h, c: (rows(s, h, c), 0)),
            pl.BlockSpec((rows_blk, dh), lambda s, h, c: (rows(s, h, c), h)),
            pl.BlockSpec((rows_blk, dh), lambda s, h, c: (rows(s, h, c), nh + h)),
            pl.BlockSpec((1, 1, dh, dh), lambda s, h, c: (s, h, 0, 0)),
            pl.BlockSpec((1, 1, dh), per_head),
            pl.BlockSpec((1, 1, LANES), per_head),
            pl.BlockSpec((1, dh), lambda s, h, c: (0, h)),
            pl.BlockSpec((1, dh), lambda s, h, c: (0, h)),
        ],
        out_specs=[
            pl.BlockSpec((rows_blk, dh), lambda s, h, c: (rows(s, h, c), h)),
            pl.BlockSpec((1, 1, dh, dh), lambda s, h, c: (s, h, 0, 0)),
            pl.BlockSpec((1, 1, dh), per_head),
            pl.BlockSpec((1, 1, LANES), per_head),
        ],
        out_shape=[
            jax.ShapeDtypeStruct((n, di), F32),
            jax.ShapeDtypeStruct(c0.shape, F32),
            jax.ShapeDtypeStruct((nseq * nh, 1, dh), F32),
            jax.ShapeDtypeStruct((nseq * nh, 1, LANES), F32),
        ],
        scratch_shapes=[pltpu.VMEM((dh, dh), F32), pltpu.VMEM((1, dh), F32), pltpu.VMEM((1, LANES), F32)],
        compiler_params=_params("parallel", "parallel", "arbitrary"),
        name="mlstm_recurrence",
    )(q, k, v, gates, xc, up, c0, n0r, m0r, skip.reshape(1, di), norm_g.reshape(1, di))
    return out, cn, nn.reshape(nseq, nh, dh), mn[:, 0, 0].reshape(nseq, nh)


def _router_kernel(x_ref, g_ref, wr_ref, br_ref, xn_ref, gates_ref):
    xn = _rmsnorm_rows(x_ref[...], g_ref[...])
    xn_ref[...] = xn.astype(BF16)
    logits = _dot(xn, wr_ref[...]) + br_ref[...]
    lane = _iota(logits.shape, 1)
    big = jnp.int32(LANES)
    is_g = jnp.logical_and(lane >= N_EXPERTS, lane < N_EXPERTS + N_GROUPS)
    gl = jnp.where(is_g, logits, -jnp.inf)
    gmax = jnp.max(gl, axis=1, keepdims=True)
    gsum = jnp.sum(jnp.where(is_g, jnp.exp(gl - gmax), 0.0), axis=1, keepdims=True)
    p_g = 1.0 / gsum
    g_idx = jnp.min(jnp.where(gl == gmax, lane, big), axis=1, keepdims=True) - N_EXPERTS
    sel = jnp.logical_and(lane < N_EXPERTS, (lane >> 2) == g_idx)
    el = jnp.where(sel, logits, -jnp.inf)
    emax = jnp.max(el, axis=1, keepdims=True)
    eexp = jnp.where(sel, jnp.exp(el - emax), 0.0)
    ep = eexp / jnp.sum(eexp, axis=1, keepdims=True)
    cand = jnp.where(sel, ep, -1.0)
    v1 = jnp.max(cand, axis=1, keepdims=True)
    idx1 = jnp.min(jnp.where(cand == v1, lane, big), axis=1, keepdims=True)
    cand2 = jnp.where(lane == idx1, -1.0, cand)
    v2 = jnp.max(cand2, axis=1, keepdims=True)
    idx2 = jnp.min(jnp.where(cand2 == v2, lane, big), axis=1, keepdims=True)
    tot = v1 + v2
    gates_ref[...] = (jnp.where(lane == idx1, v1 / tot * p_g, 0.0)
                      + jnp.where(lane == idx2, v2 / tot * p_g, 0.0))


def moe_router(x, gain, wr, br, tm=1024):
    n, d = x.shape
    tm = _row_tile(n, tm)
    return pl.pallas_call(
        _router_kernel,
        grid=(n // tm,),
        in_specs=[
            pl.BlockSpec((tm, d), lambda i: (i, 0)),
            pl.BlockSpec((1, d), lambda i: (0, 0)),
            pl.BlockSpec((d, LANES), lambda i: (0, 0)),
            pl.BlockSpec((1, LANES), lambda i: (0, 0)),
        ],
        out_specs=[pl.BlockSpec((tm, d), lambda i: (i, 0)), pl.BlockSpec((tm, LANES), lambda i: (i, 0))],
        out_shape=[jax.ShapeDtypeStruct((n, d), BF16), jax.ShapeDtypeStruct((n, LANES), F32)],
        compiler_params=_params("parallel"),
        name="moe_router",
    )(x, gain.reshape(1, d), wr, br)


def _moe_expert_kernel(*refs, has_final):
    xn_ref, gates_ref, wg_ref, wu_ref, wd_ref, x_ref = refs[:6]
    fg_ref = refs[6] if has_final else None
    o_ref = refs[6 + int(has_final)]
    e = pl.program_id(1)

    @pl.when(e == 0)
    def _():
        o_ref[...] = x_ref[...]

    xn = xn_ref[...]
    hg = jnp.dot(xn, wg_ref[0, 0].astype(BF16), preferred_element_type=F32)
    hu = jnp.dot(xn, wu_ref[0, 0].astype(BF16), preferred_element_type=F32)
    h = jax.nn.silu(hg) * hu * _lane_column(gates_ref[...], e)
    o_ref[...] += _dot(h, wd_ref[0, 0])

    if has_final:
        @pl.when(e == pl.num_programs(1) - 1)
        def _():
            o_ref[...] = _rmsnorm_rows(o_ref[...], fg_ref[...])


def moe_experts(xn, gates, w_gate, w_up, w_down, layer, x, final_gain=None, tm=1024):
    n, d = x.shape
    _, ne, _, f = w_gate.shape
    tm = _row_tile(n, tm)
    in_specs = [
        pl.BlockSpec((tm, d), lambda i, e: (i, 0)),
        pl.BlockSpec((tm, LANES), lambda i, e: (i, 0)),
        pl.BlockSpec((1, 1, d, f), lambda i, e: (layer, e, 0, 0)),
        pl.BlockSpec((1, 1, d, f), lambda i, e: (layer, e, 0, 0)),
        pl.BlockSpec((1, 1, f, d), lambda i, e: (layer, e, 0, 0)),
        pl.BlockSpec((tm, d), lambda i, e: (i, 0)),
    ]
    args = [xn, gates, w_gate, w_up, w_down, x]
    if final_gain is not None:
        in_specs.append(pl.BlockSpec((1, d), lambda i, e: (0, 0)))
        args.append(final_gain.reshape(1, d))
    return pl.pallas_call(
        functools.partial(_moe_expert_kernel, has_final=final_gain is not None),
        grid=(n // tm, ne),
        in_specs=in_specs,
        out_specs=pl.BlockSpec((tm, d), lambda i, e: (i, 0)),
        out_shape=jax.ShapeDtypeStruct((n, d), F32),
        compiler_params=_params("parallel", "arbitrary"),
        name="moe_experts",
    )(*args)


def _blockdiag_tiles(w):
    nblocks, bi, bo = w.shape
    per = MXU_WIDTH // bi
    w4 = w.reshape(nblocks // per, per, bi, bo)
    dense = jnp.einsum('taio,ab->taibo', w4, jnp.eye(per, dtype=w.dtype))
    return dense.reshape(nblocks // per, MXU_WIDTH, MXU_WIDTH).astype(BF16)


def _pad_cols(w, width=LANES):
    return jnp.pad(w, ((0, 0), (0, width - w.shape[1])))


def _history_tiles(buf):
    nseq, hist, ch = buf.shape
    return jnp.pad(buf, ((0, 0), (SUBLANES - hist, 0), (0, 0))).reshape(nseq * SUBLANES, ch)


def _moe_layer(x, gain, w_rg, b_rg, w_re, b_re, w_gate, w_up, w_down, layer, final_gain=None):
    wr = _pad_cols(jnp.concatenate([w_re, w_rg], axis=1)).astype(BF16)
    br = _pad_cols(jnp.concatenate([b_re, b_rg])[None, :])
    xn, gates = moe_router(x, gain, wr, br)
    return moe_experts(xn, gates, w_gate, w_up, w_down, layer, x, final_gain)


def _trunk(x3, st_lru_conv, st_lru_h, st_gla_s, st_m_conv, st_m_c, st_m_n, st_m_m,
           norm_mix_g, norm_ffn_g, norm_final_g,
           l0_w_in, l0_lru_conv_w, l0_lru_conv_b, l0_lru_wa, l0_lru_ba, l0_lru_wx, l0_lru_bx, l0_lru_lam,
           l0_gla_wa2, l0_gla_ba2, l0_gla_norm_g, l0_w_out,
           l1_w_up, l1_conv_w, l1_conv_b, l1_wq, l1_wk, l1_wv, l1_w_ig, l1_b_ig, l1_w_fg, l1_b_fg, l1_skip,
           l1_norm_g, l1_w_down,
           moe_w_rg, moe_b_rg, moe_w_re, moe_b_re, moe_w_gate, moe_w_up, moe_w_down):
    nseq, t, d = x3.shape
    x = x3.reshape(nseq * t, d)
    depth = norm_mix_g.shape[0]
    single = t > SUBLANES
    hist = CONV_W - 1
    outs = {k: [] for k in ("lru_conv", "lru_h", "gla_s", "m_conv", "m_c", "m_n", "m_m")}
    for layer in range(depth):
        j = layer // 2
        if layer % 2 == 0:
            w = st_lru_h.shape[-1]
            _, _, nh, dk, dv = st_gla_s.shape
            rank = l0_gla_wa2.shape[1]
            main = 2 * w + 2 * nh * dk + 2 * nh * dv
            proj = fused_linear([x], l0_w_in[j], n_out=main, gain=norm_mix_g[layer], name="linear_in")
            lg = gla_decay(x, norm_mix_g[layer], _pad_cols(l0_w_in[j][:, main:main + rank]).astype(BF16),
                           jnp.pad(l0_gla_wa2[j], ((0, LANES - rank), (0, 0))).astype(BF16), l0_gla_ba2[j])
            ya, h_last = lru_branch(
                proj, _history_tiles(st_lru_conv[j]), st_lru_h[j], l0_lru_conv_w[j], l0_lru_conv_b[j],
                _blockdiag_tiles(l0_lru_wa[j]), l0_lru_ba[j], _blockdiag_tiles(l0_lru_wx[j]), l0_lru_bx[j],
                l0_lru_lam[j], nseq=nseq, t=t, nb=1 if single else min(nseq, 64), tc=min(t, 256))
            yb, s_new = gla_branch(
                proj, lg, st_gla_s[j], l0_gla_norm_g[j], nseq=nseq, t=t, nb=1 if single else min(nseq, 16),
                rows_blk=min(t, 512), chunk=min(t, GLA_CHUNK), sub=min(t, GLA_SUB),
                q_off=2 * w, k_off=2 * w + nh * dk, v_off=2 * w + 2 * nh * dk, g_off=2 * w + 2 * nh * dk + nh * dv)
            x = fused_linear([ya, yb], l0_w_out[j], n_out=d, res=x, name="linear_out")
            outs["lru_conv"].append(proj.reshape(nseq, t, main)[:, t - hist:, :w])
            outs["lru_h"].append(h_last)
            outs["gla_s"].append(s_new)
        else:
            _, _, nh, dh, _ = st_m_c.shape
            di = nh * dh
            up = fused_linear([x], l1_w_up[j], n_out=2 * di, gain=norm_mix_g[layer], name="linear_up")
            wg = _pad_cols(jnp.concatenate([l1_w_ig[j], l1_w_fg[j]], axis=1)).astype(BF16)
            bg = _pad_cols(jnp.concatenate([l1_b_ig[j], l1_b_fg[j]])[None, :])
            q, k, v, xc, gates = mlstm_pre(
                up, _history_tiles(st_m_conv[j]), l1_conv_w[j], l1_conv_b[j], _blockdiag_tiles(l1_wq[j]),
                _blockdiag_tiles(l1_wk[j]), _blockdiag_tiles(l1_wv[j]), wg, bg,
                nseq=nseq, t=t, nb=1 if single else min(nseq, 16), tc=min(t, 128), n_heads=nh)
            hout, c_new, n_new, m_new = mlstm_recurrence(
                q, k, v, gates, xc, up, st_m_c[j], st_m_n[j], st_m_m[j], l1_skip[j], l1_norm_g[j],
                nseq=nseq, t=t, rows_blk=min(t, M_CHUNK), chunk=min(t, M_CHUNK))
            x = fused_linear([hout], l1_w_down[j], n_out=d, res=x, tm=512, name="linear_down")
            outs["m_conv"].append(up.reshape(nseq, t, 2 * di)[:, t - hist:, :di])
            outs["m_c"].append(c_new)
            outs["m_n"].append(n_new)
            outs["m_m"].append(m_new)
        x = _moe_layer(x, norm_ffn_g[layer], moe_w_rg[layer], moe_b_rg[layer], moe_w_re[layer], moe_b_re[layer],
                       moe_w_gate, moe_w_up, moe_w_down, layer,
                       final_gain=norm_final_g if layer == depth - 1 else None)
    return (x.reshape(nseq, t, d),) + tuple(jnp.stack(outs[k]) for k in
                                             ("lru_conv", "lru_h", "gla_s", "m_conv", "m_c", "m_n", "m_m"))


def kernel(x_prompt, x_sample, state_lru_conv, state_lru_h, state_gla_S, state_mlstm_conv, state_mlstm_C,
           state_mlstm_n, state_mlstm_m, norm_mix_g, norm_ffn_g, norm_final_g, l0_w_in, l0_lru_conv_w,
           l0_lru_conv_b, l0_lru_wa, l0_lru_ba, l0_lru_wx, l0_lru_bx, l0_lru_lam, l0_gla_wa2, l0_gla_ba2,
           l0_gla_norm_g, l0_w_out, l1_w_up, l1_conv_w, l1_conv_b, l1_wq, l1_wk, l1_wv, l1_w_ig, l1_b_ig,
           l1_w_fg, l1_b_fg, l1_skip, l1_norm_g, l1_w_down, moe_w_rg, moe_b_rg, moe_w_re, moe_b_re, moe_w_gate,
           moe_w_up, moe_w_down):
    l0_w_in, l0_w_out, l1_w_up, l1_w_down, moe_w_gate, moe_w_up, moe_w_down = (
        w.astype(BF16) for w in (l0_w_in, l0_w_out, l1_w_up, l1_w_down, moe_w_gate, moe_w_up, moe_w_down))
    weights = (norm_mix_g, norm_ffn_g, norm_final_g, l0_w_in, l0_lru_conv_w, l0_lru_conv_b, l0_lru_wa, l0_lru_ba,
               l0_lru_wx, l0_lru_bx, l0_lru_lam, l0_gla_wa2, l0_gla_ba2, l0_gla_norm_g, l0_w_out, l1_w_up,
               l1_conv_w, l1_conv_b, l1_wq, l1_wk, l1_wv, l1_w_ig, l1_b_ig, l1_w_fg, l1_b_fg, l1_skip, l1_norm_g,
               l1_w_down, moe_w_rg, moe_b_rg, moe_w_re, moe_b_re, moe_w_gate, moe_w_up, moe_w_down)
    states = (state_lru_conv, state_lru_h, state_gla_S, state_mlstm_conv, state_mlstm_C, state_mlstm_n,
              state_mlstm_m)
    bp = x_prompt.shape[0]
    zero_states = tuple(jnp.zeros((s.shape[0], bp) + s.shape[2:], s.dtype) for s in states)
    prompt = _trunk(x_prompt, *zero_states, *weights)
    sample = _trunk(x_sample, *states, *weights)
    return (prompt[0], sample[0]) + prompt[1:] + sample[1:]
```

```python
import functools

import jax
import jax.numpy as jnp
from jax import lax
from jax.experimental import pallas as pl
from jax.experimental.pallas import tpu as pltpu

EPS = 1e-6
CONV_W = 4
LRU_C = 8.0
GLA_TAU = 16.0
GLA_CHUNK = 64
GLA_SUB = 8
M_CHUNK = 256
N_GROUPS = 4
E_PER_GROUP = 4
N_EXPERTS = N_GROUPS * E_PER_GROUP

V7X_VMEM_BYTES = 64 * 1024 * 1024
VMEM_LIMIT_BYTES = V7X_VMEM_BYTES - 8 * 1024 * 1024
SUBLANES = 8
LANES = 128
MXU_WIDTH = 256

F32 = jnp.float32
BF16 = jnp.bfloat16


def _params(*semantics):
    return pltpu.CompilerParams(dimension_semantics=semantics, vmem_limit_bytes=VMEM_LIMIT_BYTES)


def _dot(a, b):
    return jnp.dot(a.astype(BF16), b.astype(BF16), preferred_element_type=F32)


def _dot_nt(a, b):
    return lax.dot_general(a.astype(BF16), b.astype(BF16), (((1,), (1,)), ((), ())), preferred_element_type=F32)


def _dot_tn(a, b):
    return lax.dot_general(a.astype(BF16), b.astype(BF16), (((0,), (0,)), ((), ())), preferred_element_type=F32)


def _dot_split(m01, y):
    y_hi = y.astype(BF16)
    y_lo = (y - y_hi.astype(F32)).astype(BF16)
    m = m01.astype(BF16)
    return jnp.dot(m, y_hi, preferred_element_type=F32) + jnp.dot(m, y_lo, preferred_element_type=F32)


def _softplus(x):
    return jnp.maximum(x, 0.0) + jnp.log1p(jnp.exp(-jnp.abs(x)))


def _log_sigmoid(x):
    return -_softplus(-x)


def _rmsnorm_rows(x, g):
    return x * lax.rsqrt(jnp.mean(x * x, axis=-1, keepdims=True) + EPS) * g


def _iota(shape, dim):
    return lax.broadcasted_iota(jnp.int32, shape, dim)


def _lane_column(x, lane_index):
    return jnp.sum(jnp.where(_iota(x.shape, 1) == lane_index, x, 0.0), axis=1, keepdims=True)


def _row_tile(n, target):
    t = min(n, target)
    assert n % t == 0
    return t


def _route(xn, wr_ref, br_ref):
    logits = _dot(xn, wr_ref[...]) + br_ref[...]
    lane = _iota(logits.shape, 1)
    big = jnp.int32(LANES)
    is_g = jnp.logical_and(lane >= N_EXPERTS, lane < N_EXPERTS + N_GROUPS)
    gl = jnp.where(is_g, logits, -jnp.inf)
    gmax = jnp.max(gl, axis=1, keepdims=True)
    gsum = jnp.sum(jnp.where(is_g, jnp.exp(gl - gmax), 0.0), axis=1, keepdims=True)
    p_g = 1.0 / gsum
    g_idx = jnp.min(jnp.where(gl == gmax, lane, big), axis=1, keepdims=True) - N_EXPERTS
    sel = jnp.logical_and(lane < N_EXPERTS, (lane >> 2) == g_idx)
    el = jnp.where(sel, logits, -jnp.inf)
    emax = jnp.max(el, axis=1, keepdims=True)
    eexp = jnp.where(sel, jnp.exp(el - emax), 0.0)
    ep = eexp / jnp.sum(eexp, axis=1, keepdims=True)
    cand = jnp.where(sel, ep, -1.0)
    v1 = jnp.max(cand, axis=1, keepdims=True)
    idx1 = jnp.min(jnp.where(cand == v1, lane, big), axis=1, keepdims=True)
    cand2 = jnp.where(lane == idx1, -1.0, cand)
    v2 = jnp.max(cand2, axis=1, keepdims=True)
    idx2 = jnp.min(jnp.where(cand2 == v2, lane, big), axis=1, keepdims=True)
    tot = v1 + v2
    return jnp.where(lane == idx1, v1 / tot * p_g, 0.0) + jnp.where(lane == idx2, v2 / tot * p_g, 0.0)


def _linear_kernel(*refs, n_lhs, has_norm, has_res, has_router):
    lhs_refs = refs[:n_lhs]
    pos = n_lhs
    g_ref = refs[pos] if has_norm else None
    pos += int(has_norm)
    w_ref = refs[pos]
    pos += 1
    res_ref = refs[pos] if has_res else None
    pos += int(has_res)
    if has_router:
        fg_ref, wr_ref, br_ref = refs[pos:pos + 3]
        pos += 3
    o_ref = refs[pos]
    pos += 1
    if has_router:
        xn_ref, gates_ref = refs[pos:pos + 2]
        pos += 2
    lhs_scr = refs[pos]

    @pl.when(pl.program_id(1) == 0)
    def _():
        off = 0
        for a_ref in lhs_refs:
            a = a_ref[...]
            if has_norm:
                a = _rmsnorm_rows(a, g_ref[...])
            lhs_scr[:, off:off + a.shape[1]] = a.astype(BF16)
            off += a.shape[1]

    acc = jnp.dot(lhs_scr[...], w_ref[...].astype(BF16), preferred_element_type=F32)
    if has_res:
        acc = res_ref[...] + acc
    o_ref[...] = acc
    if has_router:
        xn = _rmsnorm_rows(acc, fg_ref[...])
        xn_ref[...] = xn.astype(BF16)
        gates_ref[...] = _route(xn, wr_ref, br_ref)


def fused_linear(lhs_list, w, *, n_out, gain=None, res=None, router=None, tm=1024, tn=1024, name="linear"):
    n = lhs_list[0].shape[0]
    ks = [a.shape[1] for a in lhs_list]
    ktot = sum(ks)
    assert w.shape[0] == ktot
    tm = _row_tile(n, tm)
    tn = _row_tile(n_out, tn)
    resident = tn == n_out == w.shape[1]
    assert router is None or resident
    in_specs = [pl.BlockSpec((tm, k), lambda i, j: (i, 0)) for k in ks]
    args = list(lhs_list)
    if gain is not None:
        in_specs.append(pl.BlockSpec((1, ktot), lambda i, j: (0, 0)))
        args.append(gain.reshape(1, ktot))
    in_specs.append(pl.BlockSpec((ktot, tn), lambda i, j: (0, j), pipeline_mode=pl.Buffered(1) if resident else None))
    args.append(w)
    if res is not None:
        in_specs.append(pl.BlockSpec((tm, tn), lambda i, j: (i, j)))
        args.append(res)
    out_specs = [pl.BlockSpec((tm, tn), lambda i, j: (i, j))]
    out_shape = [jax.ShapeDtypeStruct((n, n_out), F32)]
    if router is not None:
        fgain, wr, br = router
        in_specs += [pl.BlockSpec((1, n_out), lambda i, j: (0, 0)), pl.BlockSpec((n_out, LANES), lambda i, j: (0, 0)),
                     pl.BlockSpec((1, LANES), lambda i, j: (0, 0))]
        args += [fgain.reshape(1, n_out), wr, br]
        out_specs += [pl.BlockSpec((tm, n_out), lambda i, j: (i, 0)), pl.BlockSpec((tm, LANES), lambda i, j: (i, 0))]
        out_shape += [jax.ShapeDtypeStruct((n, n_out), BF16), jax.ShapeDtypeStruct((n, LANES), F32)]
    kern = functools.partial(_linear_kernel, n_lhs=len(lhs_list), has_norm=gain is not None, has_res=res is not None,
                             has_router=router is not None)
    outs = pl.pallas_call(
        kern,
        grid=(n // tm, n_out // tn),
        in_specs=in_specs,
        out_specs=out_specs,
        out_shape=out_shape,
        scratch_shapes=[pltpu.VMEM((tm, ktot), BF16)],
        compiler_params=_params("parallel", "arbitrary"),
        name=name,
    )(*args)
    return outs if router is not None else outs[0]


def _gla_decay_kernel(x_ref, g_ref, walr_ref, wa2_ref, ba2_ref, o_ref):
    xn = _rmsnorm_rows(x_ref[...], g_ref[...])
    alr = _dot(xn, walr_ref[...])
    o_ref[...] = _log_sigmoid(_dot(alr, wa2_ref[...]) + ba2_ref[...]) * (1.0 / GLA_TAU)


def gla_decay(x, gain, w_alr_pad, wa2_pad, ba2, tm=512):
    n, d = x.shape
    hk = wa2_pad.shape[1]
    tm = _row_tile(n, tm)
    return pl.pallas_call(
        _gla_decay_kernel,
        grid=(n // tm,),
        in_specs=[
            pl.BlockSpec((tm, d), lambda i: (i, 0)),
            pl.BlockSpec((1, d), lambda i: (0, 0)),
            pl.BlockSpec((d, LANES), lambda i: (0, 0)),
            pl.BlockSpec((LANES, hk), lambda i: (0, 0)),
            pl.BlockSpec((1, hk), lambda i: (0, 0)),
        ],
        out_specs=pl.BlockSpec((tm, hk), lambda i: (i, 0)),
        out_shape=jax.ShapeDtypeStruct((n, hk), F32),
        compiler_params=_params("parallel"),
        name="gla_decay",
    )(x, gain.reshape(1, d), w_alr_pad, wa2_pad, ba2.reshape(1, hk))


def _causal_conv(x, prev, w_ref, b_ref, seq_rows):
    r = x.shape[0]
    row = _iota((r, 1), 0)
    per_seq = prev.shape[0] == r
    pos = row & (seq_rows - 1) if per_seq else row
    acc = b_ref[...] + x * w_ref[CONV_W - 1:CONV_W, :]
    for j in range(1, CONV_W):
        if per_seq:
            hist = pltpu.roll(prev, (j - SUBLANES) % r, 0)
        else:
            hist = jnp.concatenate([pltpu.roll(prev, j, 0), x[SUBLANES:]], axis=0)
        shifted = jnp.where(pos < j, hist, pltpu.roll(x, j, 0))
        acc = acc + shifted * w_ref[CONV_W - 1 - j:CONV_W - j, :]
    return acc


def _lru_kernel(xa_ref, ga_ref, prev_ref, h0_ref, cw_ref, cb_ref, wa_ref, ba_ref, wx_ref, bx_ref, lam_ref,
                ya_ref, hl_ref, prev_scr, h_scr, a_scr, b_scr, *, nb, tc):
    c = pl.program_id(1)

    @pl.when(c == 0)
    def _():
        prev_scr[...] = prev_ref[...]
        h_scr[...] = h0_ref[...]

    x = xa_ref[...]
    xc = _causal_conv(x, prev_scr[...], cw_ref, cb_ref, tc)
    if nb == 1:
        prev_scr[...] = x[x.shape[0] - SUBLANES:]
    nblk = x.shape[1] // MXU_WIDTH
    xcb = xc.astype(BF16)
    r_parts, i_parts = [], []
    for t in range(nblk):
        sl = slice(t * MXU_WIDTH, (t + 1) * MXU_WIDTH)
        r_parts.append(jnp.dot(xcb[:, sl], wa_ref[t], preferred_element_type=F32))
        i_parts.append(jnp.dot(xcb[:, sl], wx_ref[t], preferred_element_type=F32))
    r_gate = jax.nn.sigmoid(jnp.concatenate(r_parts, axis=1) + ba_ref[...])
    i_gate = jax.nn.sigmoid(jnp.concatenate(i_parts, axis=1) + bx_ref[...])
    log_a = (-LRU_C) * r_gate * _softplus(-lam_ref[...])
    a = jnp.exp(log_a)
    a_scr[...] = a
    b_scr[...] = jnp.sqrt(-jnp.tanh(log_a) * (a * a + 1.0)) * (i_gate * xc)

    def seq_body(s, carry):
        def tile_body(tl, h):
            base = pl.multiple_of(s * tc + tl * SUBLANES, SUBLANES)
            for i in range(SUBLANES):
                h = a_scr[pl.ds(base + i, 1), :] * h + b_scr[pl.ds(base + i, 1), :]
                b_scr[pl.ds(base + i, 1), :] = h
            return h

        h_scr[s] = lax.fori_loop(0, tc // SUBLANES, tile_body, h_scr[s])
        return carry

    lax.fori_loop(0, nb, seq_body, 0, unroll=min(nb, 4))
    ya_ref[...] = b_scr[...] * jax.nn.gelu(ga_ref[...])

    @pl.when(c == pl.num_programs(1) - 1)
    def _():
        hl_ref[...] = h_scr[...]


def lru_branch(proj, prev8, h0, conv_w, conv_b, wa_t, ba, wx_t, bx, lam, *, nseq, t, nb, tc):
    n = proj.shape[0]
    w = h0.shape[1]
    nchunk = t // tc
    assert nseq % nb == 0 and t % tc == 0 and (nb == 1 or tc == t == SUBLANES)
    r = nb * tc
    nblk = w // MXU_WIDTH
    rows = lambda s, c: s * nchunk + c
    full2 = lambda s, c: (0, 0)
    full3 = lambda s, c: (0, 0, 0)
    kern = functools.partial(_lru_kernel, nb=nb, tc=tc)
    ya, hl = pl.pallas_call(
        kern,
        grid=(nseq // nb, nchunk),
        in_specs=[
            pl.BlockSpec((r, w), lambda s, c: (rows(s, c), 0)),
            pl.BlockSpec((r, w), lambda s, c: (rows(s, c), 1)),
            pl.BlockSpec((nb * SUBLANES, w), lambda s, c: (s, 0)),
            pl.BlockSpec((nb, 1, w), lambda s, c: (s, 0, 0)),
            pl.BlockSpec((CONV_W, w), full2),
            pl.BlockSpec((1, w), full2),
            pl.BlockSpec((nblk, MXU_WIDTH, MXU_WIDTH), full3),
            pl.BlockSpec((1, w), full2),
            pl.BlockSpec((nblk, MXU_WIDTH, MXU_WIDTH), full3),
            pl.BlockSpec((1, w), full2),
            pl.BlockSpec((1, w), full2),
        ],
        out_specs=[
            pl.BlockSpec((r, w), lambda s, c: (rows(s, c), 0)),
            pl.BlockSpec((nb, 1, w), lambda s, c: (s, 0, 0)),
        ],
        out_shape=[jax.ShapeDtypeStruct((n, w), F32), jax.ShapeDtypeStruct((nseq, 1, w), F32)],
        scratch_shapes=[
            pltpu.VMEM((nb * SUBLANES, w), F32),
            pltpu.VMEM((nb, 1, w), F32),
            pltpu.VMEM((r, w), F32),
            pltpu.VMEM((r, w), F32),
        ],
        compiler_params=_params("parallel", "arbitrary"),
        name="lru_branch",
    )(proj, proj, prev8, h0.reshape(nseq, 1, w), conv_w, conv_b.reshape(1, w), wa_t, ba.reshape(1, w), wx_t,
      bx.reshape(1, w), lam.reshape(1, w))
    return ya, hl.reshape(nseq, w)


def _gla_cumdecay(lg, run_rows):
    r = lg.shape[0]
    row = _iota((r, r), 0)
    col = _iota((r, r), 1)
    shift = run_rows.bit_length() - 1
    tri = jnp.logical_and(row >= col, (row >> shift) == (col >> shift))
    return _dot_split(tri.astype(F32), lg)


def _gla_near_att(qs, k, bc, sub):
    r = qs.shape[0]
    row = _iota((r, r), 0)
    col = _iota((r, r), 1)
    posr = _iota((r, 1), 0) & (sub - 1)
    att = jnp.zeros((r, r), F32)
    for d in range(sub):
        kd = k if d == 0 else pltpu.roll(k, d, 0)
        bcd = bc if d == 0 else pltpu.roll(bc, d, 0)
        valid = posr >= d
        prod = qs * kd * jnp.exp(jnp.where(valid, bc - bcd, 0.0))
        diag = jnp.sum(jnp.where(valid, prod, 0.0), axis=1, keepdims=True)
        att = att + jnp.where(col == row - d, diag, 0.0)
    return att


def _gla_far_att(qs, k, bc, sub):
    l, dk = qs.shape
    att = jnp.zeros((l, l), F32)
    for j in range(l // sub - 1):
        lo, hi = j * sub, (j + 1) * sub
        e_j = bc[hi - 1:hi, :]
        kp = k[lo:hi] * jnp.exp(e_j - bc[lo:hi])
        qp = qs[hi:] * jnp.exp(bc[hi:] - e_j)
        k_rows = [jnp.zeros((lo, dk), F32)] * (lo > 0) + [kp, jnp.zeros((l - hi, dk), F32)]
        att = att + _dot_nt(jnp.concatenate([jnp.zeros((hi, dk), F32), qp], axis=0), jnp.concatenate(k_rows, axis=0))
    return att


def _as_column(row_vec):
    d = row_vec.shape[1]
    eye = _iota((d, d), 0) == _iota((d, d), 1)
    return jnp.sum(jnp.where(eye, row_vec, 0.0), axis=1, keepdims=True)


def _gla_kernel(q_ref, k_ref, v_ref, g_ref, lg_ref, s0_ref, gn_ref, yb_ref, sn_ref, *scratch, nb, chunk, sub):
    c = pl.program_id(2)
    scale = q_ref.shape[1] ** -0.5

    def finish(o, g):
        on = o * lax.rsqrt(jnp.mean(o * o, axis=-1, keepdims=True) + EPS) * gn_ref[...]
        return on * jax.nn.silu(g)

    if nb == 1:
        (s_scr,) = scratch

        @pl.when(c == 0)
        def _():
            s_scr[...] = s0_ref[0, 0]

        def body(i, carry):
            rows = pl.ds(pl.multiple_of(i * chunk, chunk), chunk)
            qs = q_ref[rows, :] * scale
            k = k_ref[rows, :]
            v = v_ref[rows, :]
            bc = _gla_cumdecay(lg_ref[rows, :], chunk)
            s_state = s_scr[...]
            o = _dot(qs * jnp.exp(bc), s_state) + _dot(_gla_near_att(qs, k, bc, sub) + _gla_far_att(qs, k, bc, sub), v)
            bl = bc[chunk - 1:chunk, :]
            s_scr[...] = _as_column(jnp.exp(bl)) * s_state + _dot_tn(k * jnp.exp(bl - bc), v)
            yb_ref[rows, :] = finish(o, g_ref[rows, :])
            return carry

        lax.fori_loop(0, q_ref.shape[0] // chunk, body, 0, unroll=2)

        @pl.when(c == pl.num_programs(2) - 1)
        def _():
            sn_ref[0, 0] = s_scr[...]
    else:
        qe_scr, kd_scr, eb_scr, o_scr = scratch
        r = q_ref.shape[0]
        qs = q_ref[...] * scale
        k = k_ref[...]
        bc = _gla_cumdecay(lg_ref[...], chunk)
        o_scr[...] = _dot(_gla_near_att(qs, k, bc, sub), v_ref[...])
        qe_scr[...] = qs * jnp.exp(bc)
        row = _iota((r, r), 0)
        col = _iota((r, r), 1)
        last = (col == (row | (chunk - 1))).astype(F32)
        bl = _dot_split(last, bc)
        kd_scr[...] = k * jnp.exp(bl - bc)
        eb_scr[...] = jnp.exp(bl)

        def body(j, carry):
            rows = pl.ds(pl.multiple_of(j * chunk, chunk), chunk)
            s_state = s0_ref[j, 0]
            o_scr[rows, :] += _dot(qe_scr[rows, :], s_state)
            decay = _as_column(eb_scr[pl.ds(pl.multiple_of(j * chunk, chunk), 1), :])
            sn_ref[j, 0] = decay * s_state + _dot_tn(kd_scr[rows, :], v_ref[rows, :])
            return carry

        lax.fori_loop(0, nb, body, 0, unroll=min(nb, 4))
        yb_ref[...] = finish(o_scr[...], g_ref[...])


def gla_branch(proj, lg, s0, gnorm, *, nseq, t, nb, rows_blk, chunk, sub, q_off, k_off, v_off, g_off):
    n = proj.shape[0]
    _, nh, dk, dv = s0.shape
    ntb = t // rows_blk if nb == 1 else 1
    r = rows_blk if nb == 1 else nb * t
    assert (nb == 1 and t % rows_blk == 0 and rows_blk % chunk == 0) or (chunk == sub == t and nseq % nb == 0)
    rows = lambda s, h, c: s * ntb + c
    kern = functools.partial(_gla_kernel, nb=nb, chunk=chunk, sub=sub)
    if nb == 1:
        scratch = [pltpu.VMEM((dk, dv), F32)]
    else:
        scratch = [pltpu.VMEM((r, dk), F32), pltpu.VMEM((r, dk), F32), pltpu.VMEM((r, dk), F32), pltpu.VMEM((r, dv), F32)]
    yb, sn = pl.pallas_call(
        kern,
        grid=(nseq // nb, nh, ntb),
        in_specs=[
            pl.BlockSpec((r, dk), lambda s, h, c: (rows(s, h, c), q_off // dk + h)),
            pl.BlockSpec((r, dk), lambda s, h, c: (rows(s, h, c), k_off // dk + h)),
            pl.BlockSpec((r, dv), lambda s, h, c: (rows(s, h, c), v_off // dv + h)),
            pl.BlockSpec((r, dv), lambda s, h, c: (rows(s, h, c), g_off // dv + h)),
            pl.BlockSpec((r, dk), lambda s, h, c: (rows(s, h, c), h)),
            pl.BlockSpec((nb, 1, dk, dv), lambda s, h, c: (s, h, 0, 0)),
            pl.BlockSpec((1, dv), lambda s, h, c: (0, h)),
        ],
        out_specs=[
            pl.BlockSpec((r, dv), lambda s, h, c: (rows(s, h, c), h)),
            pl.BlockSpec((nb, 1, dk, dv), lambda s, h, c: (s, h, 0, 0)),
        ],
        out_shape=[jax.ShapeDtypeStruct((n, nh * dv), F32), jax.ShapeDtypeStruct(s0.shape, F32)],
        scratch_shapes=scratch,
        compiler_params=_params("parallel", "parallel", "arbitrary"),
        name="gla_branch",
    )(proj, proj, proj, proj, lg, s0, gnorm.reshape(1, nh * dv))
    return yb, sn


def _mlstm_pre_kernel(xm_ref, prev_ref, cw_ref, cb_ref, wq_ref, wk_ref, wv_ref, wg_ref, bg_ref,
                      q_ref, k_ref, v_ref, xc_ref, gate_ref, prev_scr, *, nb, tc, n_heads):
    c = pl.program_id(1)

    @pl.when(c == 0)
    def _():
        prev_scr[...] = prev_ref[...]

    x = xm_ref[...]
    xc = jax.nn.silu(_causal_conv(x, prev_scr[...], cw_ref, cb_ref, tc))
    if nb == 1:
        prev_scr[...] = x[x.shape[0] - SUBLANES:]
    xc_ref[...] = xc
    di = x.shape[1]
    xcb = xc.astype(BF16)
    xb = x.astype(BF16)
    for t in range(di // MXU_WIDTH):
        sl = slice(t * MXU_WIDTH, (t + 1) * MXU_WIDTH)
        q_ref[:, sl] = jnp.dot(xcb[:, sl], wq_ref[t], preferred_element_type=F32)
        k_ref[:, sl] = jnp.dot(xcb[:, sl], wk_ref[t], preferred_element_type=F32)
        v_ref[:, sl] = jnp.dot(xb[:, sl], wv_ref[t], preferred_element_type=F32)
    pre = (_dot(q_ref[...], wg_ref[0:di, :]) + _dot(k_ref[...], wg_ref[di:2 * di, :])
           + _dot(v_ref[...], wg_ref[2 * di:3 * di, :]) + bg_ref[...])
    gate_ref[...] = jnp.where(_iota(pre.shape, 1) < n_heads, pre, _log_sigmoid(pre))


def mlstm_pre(up, prev8, conv_w, conv_b, wq_t, wk_t, wv_t, wg, bg, *, nseq, t, nb, tc, n_heads):
    n = up.shape[0]
    di = conv_w.shape[1]
    nchunk = t // tc
    assert nseq % nb == 0 and t % tc == 0 and (nb == 1 or tc == t == SUBLANES)
    r = nb * tc
    nblk = di // MXU_WIDTH
    rows = lambda s, c: (s * nchunk + c, 0)
    full2 = lambda s, c: (0, 0)
    full3 = lambda s, c: (0, 0, 0)
    wide = jax.ShapeDtypeStruct((n, di), F32)
    kern = functools.partial(_mlstm_pre_kernel, nb=nb, tc=tc, n_heads=n_heads)
    return pl.pallas_call(
        kern,
        grid=(nseq // nb, nchunk),
        in_specs=[
            pl.BlockSpec((r, di), rows),
            pl.BlockSpec((nb * SUBLANES, di), lambda s, c: (s, 0)),
            pl.BlockSpec((CONV_W, di), full2),
            pl.BlockSpec((1, di), full2),
            pl.BlockSpec((nblk, MXU_WIDTH, MXU_WIDTH), full3),
            pl.BlockSpec((nblk, MXU_WIDTH, MXU_WIDTH), full3),
            pl.BlockSpec((nblk, MXU_WIDTH, MXU_WIDTH), full3),
            pl.BlockSpec((3 * di, LANES), full2),
            pl.BlockSpec((1, LANES), full2),
        ],
        out_specs=[pl.BlockSpec((r, di), rows)] * 4 + [pl.BlockSpec((r, LANES), rows)],
        out_shape=[wide, wide, wide, wide, jax.ShapeDtypeStruct((n, LANES), F32)],
        scratch_shapes=[pltpu.VMEM((nb * SUBLANES, di), F32)],
        compiler_params=_params("parallel", "arbitrary"),
        name="mlstm_pre",
    )(up, prev8, conv_w, conv_b.reshape(1, di), wq_t, wk_t, wv_t, wg, bg)


def _mlstm_chunk(q, k, v, igc, lfc, xc, z, skip, ng, m_prev, n_prev, c_load, c_store):
    l, dh = q.shape
    row = _iota((l, l), 0)
    col = _iota((l, l), 1)
    tril = row >= col
    f_b = jnp.broadcast_to(lfc, (l, l))
    i_b = jnp.broadcast_to(igc, (l, l))
    f_col = _dot_split(tril.astype(F32), f_b)
    row_term = _dot_split(jnp.ones((l, l), F32),
                          jnp.where(row == col, i_b, 0.0) - jnp.where(row <= col, f_b, 0.0))
    dm = jnp.where(tril, f_col + row_term, -jnp.inf)
    fcum = f_col[:, 0:1]
    prev = m_prev + fcum
    mt = jnp.maximum(prev, jnp.max(dm, axis=1, keepdims=True))
    wprev = jnp.exp(prev - mt)
    smat = _dot_nt(q, k) * jnp.exp(dm - mt)
    den = wprev * jnp.sum(q * n_prev, axis=1, keepdims=True) + jnp.sum(smat, axis=1, keepdims=True)
    inv = 1.0 / jnp.maximum(jnp.abs(den), jnp.exp(-mt))
    f_last = fcum[l - 1:l, :]
    m_last = mt[l - 1:l, :]
    w_c = jnp.exp(m_prev + f_last - m_last)
    kw = k * jnp.exp(f_last - fcum + igc - m_last)
    qb = q.astype(BF16)
    sb = smat.astype(BF16)
    kwb = kw.astype(BF16)
    parts = []
    for t in range(dh // MXU_WIDTH):
        cols = slice(t * MXU_WIDTH, (t + 1) * MXU_WIDTH)
        c_blk = c_load(cols)
        vb = v[:, cols].astype(BF16)
        num = (wprev * jnp.dot(qb, c_blk.astype(BF16), preferred_element_type=F32)
               + jnp.dot(sb, vb, preferred_element_type=F32))
        parts.append(num * inv)
        c_store(cols, w_c * c_blk + _dot_tn(kwb, vb))
    n_new = w_c * n_prev + jnp.sum(kw, axis=0, keepdims=True)
    hh = jnp.concatenate(parts, axis=1)
    hc = hh - jnp.mean(hh, axis=-1, keepdims=True)
    hn = hc * lax.rsqrt(jnp.mean(hc * hc, axis=-1, keepdims=True) + EPS) * ng
    return (hn + skip * xc) * jax.nn.silu(z), n_new, m_last


def _mlstm_seq_kernel(*refs, chunk, n_heads, zero_state):
    q_ref, k_ref, v_ref, gate_ref, xc_ref, z_ref = refs[:6]
    pos = 6
    if not zero_state:
        c0_ref, n0_ref, m0_ref = refs[pos:pos + 3]
        pos += 3
    skip_ref, ng_ref, out_ref, cn_ref, nn_ref, mn_ref, c_scr, n_scr, m_scr = refs[pos:pos + 9]
    head = pl.program_id(1)
    c = pl.program_id(2)

    @pl.when(c == 0)
    def _():
        if zero_state:
            c_scr[...] = jnp.zeros_like(c_scr)
            n_scr[...] = jnp.zeros_like(n_scr)
            m_scr[...] = jnp.zeros_like(m_scr)
        else:
            c_scr[...] = c0_ref[0, 0]
            n_scr[...] = n0_ref[0]
            m_scr[...] = m0_ref[0]

    kscale = q_ref.shape[1] ** -0.5

    def c_store(cols, value):
        c_scr[:, cols] = value

    def body(i, carry):
        rows = pl.ds(pl.multiple_of(i * chunk, chunk), chunk)
        gates = gate_ref[rows, :]
        out, n_new, m_last = _mlstm_chunk(
            q_ref[rows, :], k_ref[rows, :] * kscale, v_ref[rows, :], _lane_column(gates, head),
            _lane_column(gates, head + n_heads), xc_ref[rows, :], z_ref[rows, :], skip_ref[...], ng_ref[...],
            m_scr[:, 0:1], n_scr[...], lambda cols: c_scr[:, cols], c_store)
        out_ref[rows, :] = out
        n_scr[...] = n_new
        m_scr[...] = jnp.broadcast_to(m_last, m_scr.shape)
        return carry

    lax.fori_loop(0, q_ref.shape[0] // chunk, body, 0)

    @pl.when(c == pl.num_programs(2) - 1)
    def _():
        cn_ref[0, 0] = c_scr[...]
        nn_ref[0] = n_scr[...]
        mn_ref[0] = m_scr[...]


def _mlstm_step_kernel(q_ref, k_ref, v_ref, gate_ref, xc_ref, z_ref, c0_ref, n0_ref, m0_ref, skip_ref, ng_ref,
                       out_ref, cn_ref, nn_ref, mn_ref, *, hb, n_heads):
    hblk = pl.program_id(1)
    dh = c0_ref.shape[2]
    kscale = dh ** -0.5
    gates = gate_ref[...]
    for hh in range(hb):
        cols_h = slice(hh * dh, (hh + 1) * dh)
        head = hblk * hb + hh

        def c_store(cols, value, hh=hh):
            cn_ref[0, hh, :, cols] = value

        out, n_new, m_last = _mlstm_chunk(
            q_ref[:, cols_h], k_ref[:, cols_h] * kscale, v_ref[:, cols_h], _lane_column(gates, head),
            _lane_column(gates, head + n_heads), xc_ref[:, cols_h], z_ref[:, cols_h], skip_ref[:, cols_h],
            ng_ref[:, cols_h], m0_ref[hh][:, 0:1], n0_ref[hh], lambda cols, hh=hh: c0_ref[0, hh, :, cols], c_store)
        out_ref[:, cols_h] = out
        nn_ref[hh] = n_new
        mn_ref[hh] = jnp.broadcast_to(m_last, (1, LANES))


def mlstm_recurrence(q, k, v, gates, xc, up, c0, n0, m0, skip, norm_g, *, nseq, t, nh, rows_blk, chunk, hb):
    n, di = q.shape
    dh = di // nh
    zero_state = c0 is None
    out_shape = [
        jax.ShapeDtypeStruct((n, di), F32),
        jax.ShapeDtypeStruct((nseq, nh, dh, dh), F32),
        jax.ShapeDtypeStruct((nseq * nh, 1, dh), F32),
        jax.ShapeDtypeStruct((nseq * nh, 1, LANES), F32),
    ]
    state_args = []
    if not zero_state:
        state_args = [c0, n0.reshape(nseq * nh, 1, dh),
                      jnp.broadcast_to(m0.reshape(nseq * nh, 1, 1), (nseq * nh, 1, LANES))]
    if t == chunk and not zero_state:
        assert nh % hb == 0
        nhb = nh // hb
        wide = lambda s, h: (s, h)
        per_head = lambda s, h: (s * nhb + h, 0, 0)
        mat = lambda s, h: (s, h, 0, 0)
        out, cn, nn, mn = pl.pallas_call(
            functools.partial(_mlstm_step_kernel, hb=hb, n_heads=nh),
            grid=(nseq, nhb),
            in_specs=[pl.BlockSpec((t, hb * dh), wide)] * 3
            + [pl.BlockSpec((t, LANES), lambda s, h: (s, 0)), pl.BlockSpec((t, hb * dh), wide),
               pl.BlockSpec((t, hb * dh), lambda s, h: (s, nhb + h)), pl.BlockSpec((1, hb, dh, dh), mat),
               pl.BlockSpec((hb, 1, dh), per_head), pl.BlockSpec((hb, 1, LANES), per_head),
               pl.BlockSpec((1, hb * dh), lambda s, h: (0, h)), pl.BlockSpec((1, hb * dh), lambda s, h: (0, h))],
            out_specs=[pl.BlockSpec((t, hb * dh), wide), pl.BlockSpec((1, hb, dh, dh), mat),
                       pl.BlockSpec((hb, 1, dh), per_head), pl.BlockSpec((hb, 1, LANES), per_head)],
            out_shape=out_shape,
            compiler_params=_params("parallel", "arbitrary"),
            name="mlstm_step",
        )(q, k, v, gates, xc, up, *state_args, skip.reshape(1, di), norm_g.reshape(1, di))
    else:
        ntb = t // rows_blk
        assert t % rows_blk == 0 and rows_blk % chunk == 0
        rows = lambda s, h, c: (s * ntb + c, h)
        per_head = lambda s, h, c: (s * nh + h, 0, 0)
        mat = lambda s, h, c: (s, h, 0, 0)
        state_specs = [] if zero_state else [pl.BlockSpec((1, 1, dh, dh), mat), pl.BlockSpec((1, 1, dh), per_head),
                                             pl.BlockSpec((1, 1, LANES), per_head)]
        out, cn, nn, mn = pl.pallas_call(
            functools.partial(_mlstm_seq_kernel, chunk=chunk, n_heads=nh, zero_state=zero_state),
            grid=(nseq, nh, ntb),
            in_specs=[pl.BlockSpec((rows_blk, dh), rows)] * 3
            + [pl.BlockSpec((rows_blk, LANES), lambda s, h, c: (s * ntb + c, 0)), pl.BlockSpec((rows_blk, dh), rows),
               pl.BlockSpec((rows_blk, dh), lambda s, h, c: (s * ntb + c, nh + h))]
            + state_specs
            + [pl.BlockSpec((1, dh), lambda s, h, c: (0, h)), pl.BlockSpec((1, dh), lambda s, h, c: (0, h))],
            out_specs=[pl.BlockSpec((rows_blk, dh), rows), pl.BlockSpec((1, 1, dh, dh), mat),
                       pl.BlockSpec((1, 1, dh), per_head), pl.BlockSpec((1, 1, LANES), per_head)],
            out_shape=out_shape,
            scratch_shapes=[pltpu.VMEM((dh, dh), F32), pltpu.VMEM((1, dh), F32), pltpu.VMEM((1, LANES), F32)],
            compiler_params=_params("parallel", "parallel", "arbitrary"),
            name="mlstm_seq",
        )(q, k, v, gates, xc, up, *state_args, skip.reshape(1, di), norm_g.reshape(1, di))
    return out, cn, nn.reshape(nseq, nh, dh), mn[:, 0, 0].reshape(nseq, nh)


def _moe_expert_kernel(*refs, has_final):
    xn_ref, gates_ref, wg_ref, wu_ref, wd_ref, x_ref = refs[:6]
    fg_ref = refs[6] if has_final else None
    o_ref = refs[6 + int(has_final)]
    e = pl.program_id(1)

    @pl.when(e == 0)
    def _():
        o_ref[...] = x_ref[...]

    xn = xn_ref[...]
    hg = jnp.dot(xn, wg_ref[0, 0].astype(BF16), preferred_element_type=F32)
    hu = jnp.dot(xn, wu_ref[0, 0].astype(BF16), preferred_element_type=F32)
    h = jax.nn.silu(hg) * hu * _lane_column(gates_ref[...], e)
    o_ref[...] += _dot(h, wd_ref[0, 0])

    if has_final:
        @pl.when(e == pl.num_programs(1) - 1)
        def _():
            o_ref[...] = _rmsnorm_rows(o_ref[...], fg_ref[...])


def moe_experts(xn, gates, w_gate, w_up, w_down, layer, x, final_gain=None, tm=1024):
    n, d = x.shape
    _, ne, _, f = w_gate.shape
    tm = _row_tile(n, tm)
    in_specs = [
        pl.BlockSpec((tm, d), lambda i, e: (i, 0)),
        pl.BlockSpec((tm, LANES), lambda i, e: (i, 0)),
        pl.BlockSpec((1, 1, d, f), lambda i, e: (layer, e, 0, 0)),
        pl.BlockSpec((1, 1, d, f), lambda i, e: (layer, e, 0, 0)),
        pl.BlockSpec((1, 1, f, d), lambda i, e: (layer, e, 0, 0)),
        pl.BlockSpec((tm, d), lambda i, e: (i, 0)),
    ]
    args = [xn, gates, w_gate, w_up, w_down, x]
    if final_gain is not None:
        in_specs.append(pl.BlockSpec((1, d), lambda i, e: (0, 0)))
        args.append(final_gain.reshape(1, d))
    return pl.pallas_call(
        functools.partial(_moe_expert_kernel, has_final=final_gain is not None),
        grid=(n // tm, ne),
        in_specs=in_specs,
        out_specs=pl.BlockSpec((tm, d), lambda i, e: (i, 0)),
        out_shape=jax.ShapeDtypeStruct((n, d), F32),
        compiler_params=_params("parallel", "arbitrary"),
        name="moe_experts",
    )(*args)


def _blockdiag_tiles(w):
    nblocks, bi, bo = w.shape
    per = MXU_WIDTH // bi
    ntiles = nblocks // per
    rows_of_tile = w.reshape(ntiles, MXU_WIDTH, bo)
    spread = jnp.broadcast_to(rows_of_tile[:, :, None, :], (ntiles, MXU_WIDTH, per, bo)).reshape(
        ntiles, MXU_WIDTH, MXU_WIDTH)
    on_diag = (_iota((MXU_WIDTH, MXU_WIDTH), 0) // bi) == (_iota((MXU_WIDTH, MXU_WIDTH), 1) // bo)
    return jnp.where(on_diag, spread, 0.0).astype(BF16)


def _pad_cols(w, width=LANES):
    return jnp.pad(w, ((0, 0), (0, width - w.shape[1])))


def _history_tiles(buf):
    nseq, hist, ch = buf.shape
    return jnp.pad(buf, ((0, 0), (SUBLANES - hist, 0), (0, 0))).reshape(nseq * SUBLANES, ch)


def _block_plan(nseq, t):
    long_seq = t > SUBLANES
    return dict(
        lru=dict(nb=1 if long_seq else min(nseq, 64), tc=min(t, 256)),
        gla=dict(nb=1 if long_seq else min(nseq, 16), rows_blk=min(t, 512), chunk=min(t, GLA_CHUNK), sub=min(t, GLA_SUB)),
        pre=dict(nb=1 if long_seq else min(nseq, 16), tc=min(t, 128)),
        rec=dict(rows_blk=min(t, M_CHUNK), chunk=min(t, M_CHUNK), hb=2),
    )


def _trunk(x3, st_lru_conv, st_lru_h, st_gla_s, st_m_conv, st_m_c, st_m_n, st_m_m,
           norm_mix_g, norm_ffn_g, norm_final_g,
           l0_w_in, l0_lru_conv_w, l0_lru_conv_b, l0_lru_wa, l0_lru_ba, l0_lru_wx, l0_lru_bx, l0_lru_lam,
           l0_gla_wa2, l0_gla_ba2, l0_gla_norm_g, l0_w_out,
           l1_w_up, l1_conv_w, l1_conv_b, l1_wq, l1_wk, l1_wv, l1_w_ig, l1_b_ig, l1_w_fg, l1_b_fg, l1_skip,
           l1_norm_g, l1_w_down,
           moe_w_rg, moe_b_rg, moe_w_re, moe_b_re, moe_w_gate, moe_w_up, moe_w_down):
    nseq, t, d = x3.shape
    x = x3.reshape(nseq * t, d)
    depth = norm_mix_g.shape[0]
    plan = _block_plan(nseq, t)
    hist = CONV_W - 1
    outs = {k: [] for k in ("lru_conv", "lru_h", "gla_s", "m_conv", "m_c", "m_n", "m_m")}
    for layer in range(depth):
        j = layer // 2
        wr = _pad_cols(jnp.concatenate([moe_w_re[layer], moe_w_rg[layer]], axis=1)).astype(BF16)
        br = _pad_cols(jnp.concatenate([moe_b_re[layer], moe_b_rg[layer]])[None, :])
        router = (norm_ffn_g[layer], wr, br)
        if layer % 2 == 0:
            w = st_lru_h.shape[-1]
            _, _, nh, dk, dv = st_gla_s.shape
            rank = l0_gla_wa2.shape[1]
            main = 2 * w + 2 * nh * dk + 2 * nh * dv
            w_in = l0_w_in[j].astype(BF16)
            proj = fused_linear([x], w_in, n_out=main, gain=norm_mix_g[layer], name="linear_in")
            lg = gla_decay(x, norm_mix_g[layer], _pad_cols(w_in[:, main:main + rank]),
                           jnp.pad(l0_gla_wa2[j], ((0, LANES - rank), (0, 0))).astype(BF16), l0_gla_ba2[j])
            ya, h_last = lru_branch(
                proj, _history_tiles(st_lru_conv[j]), st_lru_h[j], l0_lru_conv_w[j], l0_lru_conv_b[j],
                _blockdiag_tiles(l0_lru_wa[j]), l0_lru_ba[j], _blockdiag_tiles(l0_lru_wx[j]), l0_lru_bx[j],
                l0_lru_lam[j], nseq=nseq, t=t, **plan["lru"])
            yb, s_new = gla_branch(
                proj, lg, st_gla_s[j], l0_gla_norm_g[j], nseq=nseq, t=t, **plan["gla"],
                q_off=2 * w, k_off=2 * w + nh * dk, v_off=2 * w + 2 * nh * dk, g_off=2 * w + 2 * nh * dk + nh * dv)
            x, xn, gates = fused_linear([ya, yb], l0_w_out[j].astype(BF16), n_out=d, res=x, router=router,
                                        tm=512, tn=d, name="linear_out")
            outs["lru_conv"].append(proj.reshape(nseq, t, main)[:, t - hist:, :w])
            outs["lru_h"].append(h_last)
            outs["gla_s"].append(s_new)
        else:
            _, _, nh, dh = st_m_n.shape
            di = nh * dh
            up = fused_linear([x], l1_w_up[j].astype(BF16), n_out=2 * di, gain=norm_mix_g[layer], name="linear_up")
            wg = _pad_cols(jnp.concatenate([l1_w_ig[j], l1_w_fg[j]], axis=1)).astype(BF16)
            bg = _pad_cols(jnp.concatenate([l1_b_ig[j], l1_b_fg[j]])[None, :])
            q, k, v, xc, gates_m = mlstm_pre(
                up, _history_tiles(st_m_conv[j]), l1_conv_w[j], l1_conv_b[j], _blockdiag_tiles(l1_wq[j]),
                _blockdiag_tiles(l1_wk[j]), _blockdiag_tiles(l1_wv[j]), wg, bg,
                nseq=nseq, t=t, n_heads=nh, **plan["pre"])
            hout, c_new, n_new, m_new = mlstm_recurrence(
                q, k, v, gates_m, xc, up, None if st_m_c is None else st_m_c[j], st_m_n[j], st_m_m[j], l1_skip[j],
                l1_norm_g[j], nseq=nseq, t=t, nh=nh, **plan["rec"])
            x, xn, gates = fused_linear([hout], l1_w_down[j].astype(BF16), n_out=d, res=x, router=router,
                                        tm=256, tn=d, name="linear_down")
            outs["m_conv"].append(up.reshape(nseq, t, 2 * di)[:, t - hist:, :di])
            outs["m_c"].append(c_new)
            outs["m_n"].append(n_new)
            outs["m_m"].append(m_new)
        x = moe_experts(xn, gates, moe_w_gate, moe_w_up, moe_w_down, layer, x,
                        final_gain=norm_final_g if layer == depth - 1 else None)
    return (x.reshape(nseq, t, d),) + tuple(jnp.stack(outs[k]) for k in
                                             ("lru_conv", "lru_h", "gla_s", "m_conv", "m_c", "m_n", "m_m"))


def kernel(x_prompt, x_sample, state_lru_conv, state_lru_h, state_gla_S, state_mlstm_conv, state_mlstm_C,
           state_mlstm_n, state_mlstm_m, norm_mix_g, norm_ffn_g, norm_final_g, l0_w_in, l0_lru_conv_w,
           l0_lru_conv_b, l0_lru_wa, l0_lru_ba, l0_lru_wx, l0_lru_bx, l0_lru_lam, l0_gla_wa2, l0_gla_ba2,
           l0_gla_norm_g, l0_w_out, l1_w_up, l1_conv_w, l1_conv_b, l1_wq, l1_wk, l1_wv, l1_w_ig, l1_b_ig,
           l1_w_fg, l1_b_fg, l1_skip, l1_norm_g, l1_w_down, moe_w_rg, moe_b_rg, moe_w_re, moe_b_re, moe_w_gate,
           moe_w_up, moe_w_down):
    moe_w_gate, moe_w_up, moe_w_down = (w.astype(BF16) for w in (moe_w_gate, moe_w_up, moe_w_down))
    weights = (norm_mix_g, norm_ffn_g, norm_final_g, l0_w_in, l0_lru_conv_w, l0_lru_conv_b, l0_lru_wa, l0_lru_ba,
               l0_lru_wx, l0_lru_bx, l0_lru_lam, l0_gla_wa2, l0_gla_ba2, l0_gla_norm_g, l0_w_out, l1_w_up,
               l1_conv_w, l1_conv_b, l1_wq, l1_wk, l1_wv, l1_w_ig, l1_b_ig, l1_w_fg, l1_b_fg, l1_skip, l1_norm_g,
               l1_w_down, moe_w_rg, moe_b_rg, moe_w_re, moe_b_re, moe_w_gate, moe_w_up, moe_w_down)
    states = (state_lru_conv, state_lru_h, state_gla_S, state_mlstm_conv, state_mlstm_C, state_mlstm_n,
              state_mlstm_m)
    bp = x_prompt.shape[0]
    zero_states = tuple(None if s is state_mlstm_C else jnp.zeros((s.shape[0], bp) + s.shape[2:], s.dtype)
                        for s in states)
    prompt = _trunk(x_prompt, *zero_states, *weights)
    sample = _trunk(x_sample, *states, *weights)
    return (prompt[0], sample[0]) + prompt[1:] + sample[1:]
```

```python
import functools

import jax
import jax.numpy as jnp
from jax import lax
from jax.experimental import pallas as pl
from jax.experimental.pallas import tpu as pltpu

EPS = 1e-6
CONV_W = 4
LRU_C = 8.0
GLA_TAU = 16.0
GLA_CHUNK = 64
GLA_SUB = 8
M_CHUNK = 256
N_GROUPS = 4
E_PER_GROUP = 4
N_EXPERTS = N_GROUPS * E_PER_GROUP

V7X_VMEM_BYTES = 64 * 1024 * 1024
VMEM_LIMIT_BYTES = V7X_VMEM_BYTES - 8 * 1024 * 1024
SUBLANES = 8
LANES = 128
MXU_WIDTH = 256

F32 = jnp.float32
BF16 = jnp.bfloat16


def _params(*semantics):
    return pltpu.CompilerParams(dimension_semantics=semantics, vmem_limit_bytes=VMEM_LIMIT_BYTES)


def _dot(a, b):
    return jnp.dot(a.astype(BF16), b.astype(BF16), preferred_element_type=F32)


def _dot_nt(a, b):
    return lax.dot_general(a.astype(BF16), b.astype(BF16), (((1,), (1,)), ((), ())), preferred_element_type=F32)


def _dot_tn(a, b):
    return lax.dot_general(a.astype(BF16), b.astype(BF16), (((0,), (0,)), ((), ())), preferred_element_type=F32)


def _dot_split(m01, y):
    y_hi = y.astype(BF16)
    y_lo = (y - y_hi.astype(F32)).astype(BF16)
    m = m01.astype(BF16)
    return jnp.dot(m, y_hi, preferred_element_type=F32) + jnp.dot(m, y_lo, preferred_element_type=F32)


def _softplus(x):
    return jnp.maximum(x, 0.0) + jnp.log1p(jnp.exp(-jnp.abs(x)))


def _log_sigmoid(x):
    return -_softplus(-x)


def _rmsnorm_rows(x, g):
    return x * lax.rsqrt(jnp.mean(x * x, axis=-1, keepdims=True) + EPS) * g


def _iota(shape, dim):
    return lax.broadcasted_iota(jnp.int32, shape, dim)


def _lane_column(x, lane_index):
    return jnp.sum(jnp.where(_iota(x.shape, 1) == lane_index, x, 0.0), axis=1, keepdims=True)


def _row_tile(n, target):
    t = min(n, target)
    assert n % t == 0
    return t


def _route(xn, wr_ref, br_ref):
    logits = _dot(xn, wr_ref[...]) + br_ref[...]
    lane = _iota(logits.shape, 1)
    big = jnp.int32(LANES)
    is_g = jnp.logical_and(lane >= N_EXPERTS, lane < N_EXPERTS + N_GROUPS)
    gl = jnp.where(is_g, logits, -jnp.inf)
    gmax = jnp.max(gl, axis=1, keepdims=True)
    gsum = jnp.sum(jnp.where(is_g, jnp.exp(gl - gmax), 0.0), axis=1, keepdims=True)
    p_g = 1.0 / gsum
    g_idx = jnp.min(jnp.where(gl == gmax, lane, big), axis=1, keepdims=True) - N_EXPERTS
    sel = jnp.logical_and(lane < N_EXPERTS, (lane >> 2) == g_idx)
    el = jnp.where(sel, logits, -jnp.inf)
    emax = jnp.max(el, axis=1, keepdims=True)
    eexp = jnp.where(sel, jnp.exp(el - emax), 0.0)
    ep = eexp / jnp.sum(eexp, axis=1, keepdims=True)
    cand = jnp.where(sel, ep, -1.0)
    v1 = jnp.max(cand, axis=1, keepdims=True)
    idx1 = jnp.min(jnp.where(cand == v1, lane, big), axis=1, keepdims=True)
    cand2 = jnp.where(lane == idx1, -1.0, cand)
    v2 = jnp.max(cand2, axis=1, keepdims=True)
    idx2 = jnp.min(jnp.where(cand2 == v2, lane, big), axis=1, keepdims=True)
    tot = v1 + v2
    return jnp.where(lane == idx1, v1 / tot * p_g, 0.0) + jnp.where(lane == idx2, v2 / tot * p_g, 0.0)


def _linear_kernel(*refs, n_lhs, has_norm, has_res, has_router):
    lhs_refs = refs[:n_lhs]
    pos = n_lhs
    g_ref = refs[pos] if has_norm else None
    pos += int(has_norm)
    w_ref = refs[pos]
    pos += 1
    res_ref = refs[pos] if has_res else None
    pos += int(has_res)
    if has_router:
        fg_ref, wr_ref, br_ref = refs[pos:pos + 3]
        pos += 3
    o_ref = refs[pos]
    pos += 1
    if has_router:
        xn_ref, gates_ref = refs[pos:pos + 2]
        pos += 2
    lhs_scr = refs[pos]

    @pl.when(pl.program_id(1) == 0)
    def _():
        off = 0
        for a_ref in lhs_refs:
            a = a_ref[...]
            if has_norm:
                a = _rmsnorm_rows(a, g_ref[...])
            lhs_scr[:, off:off + a.shape[1]] = a.astype(BF16)
            off += a.shape[1]

    acc = jnp.dot(lhs_scr[...], w_ref[...].astype(BF16), preferred_element_type=F32)
    if has_res:
        acc = res_ref[...] + acc
    o_ref[...] = acc
    if has_router:
        xn = _rmsnorm_rows(acc, fg_ref[...])
        xn_ref[...] = xn.astype(BF16)
        gates_ref[...] = _route(xn, wr_ref, br_ref)


def fused_linear(lhs_list, w, *, n_out, gain=None, res=None, router=None, tm=1024, tn=1024, name="linear"):
    n = lhs_list[0].shape[0]
    ks = [a.shape[1] for a in lhs_list]
    ktot = sum(ks)
    assert w.shape[0] == ktot
    tm = _row_tile(n, tm)
    tn = _row_tile(n_out, tn)
    resident = tn == n_out == w.shape[1]
    assert router is None or resident
    in_specs = [pl.BlockSpec((tm, k), lambda i, j: (i, 0)) for k in ks]
    args = list(lhs_list)
    if gain is not None:
        in_specs.append(pl.BlockSpec((1, ktot), lambda i, j: (0, 0)))
        args.append(gain.reshape(1, ktot))
    in_specs.append(pl.BlockSpec((ktot, tn), lambda i, j: (0, j), pipeline_mode=pl.Buffered(1) if resident else None))
    args.append(w)
    if res is not None:
        in_specs.append(pl.BlockSpec((tm, tn), lambda i, j: (i, j)))
        args.append(res)
    out_specs = [pl.BlockSpec((tm, tn), lambda i, j: (i, j))]
    out_shape = [jax.ShapeDtypeStruct((n, n_out), F32)]
    if router is not None:
        fgain, wr, br = router
        in_specs += [pl.BlockSpec((1, n_out), lambda i, j: (0, 0)), pl.BlockSpec((n_out, LANES), lambda i, j: (0, 0)),
                     pl.BlockSpec((1, LANES), lambda i, j: (0, 0))]
        args += [fgain.reshape(1, n_out), wr, br]
        out_specs += [pl.BlockSpec((tm, n_out), lambda i, j: (i, 0)), pl.BlockSpec((tm, LANES), lambda i, j: (i, 0))]
        out_shape += [jax.ShapeDtypeStruct((n, n_out), BF16), jax.ShapeDtypeStruct((n, LANES), F32)]
    kern = functools.partial(_linear_kernel, n_lhs=len(lhs_list), has_norm=gain is not None, has_res=res is not None,
                             has_router=router is not None)
    outs = pl.pallas_call(
        kern,
        grid=(n // tm, n_out // tn),
        in_specs=in_specs,
        out_specs=out_specs,
        out_shape=out_shape,
        scratch_shapes=[pltpu.VMEM((tm, ktot), BF16)],
        compiler_params=_params("parallel", "arbitrary"),
        name=name,
    )(*args)
    return outs if router is not None else outs[0]


def _gla_decay_kernel(x_ref, g_ref, walr_ref, wa2_ref, ba2_ref, o_ref):
    xn = _rmsnorm_rows(x_ref[...], g_ref[...])
    alr = _dot(xn, walr_ref[...])
    o_ref[...] = _log_sigmoid(_dot(alr, wa2_ref[...]) + ba2_ref[...]) * (1.0 / GLA_TAU)


def gla_decay(x, gain, w_alr_pad, wa2_pad, ba2, tm=512):
    n, d = x.shape
    hk = wa2_pad.shape[1]
    tm = _row_tile(n, tm)
    return pl.pallas_call(
        _gla_decay_kernel,
        grid=(n // tm,),
        in_specs=[
            pl.BlockSpec((tm, d), lambda i: (i, 0)),
            pl.BlockSpec((1, d), lambda i: (0, 0)),
            pl.BlockSpec((d, LANES), lambda i: (0, 0)),
            pl.BlockSpec((LANES, hk), lambda i: (0, 0)),
            pl.BlockSpec((1, hk), lambda i: (0, 0)),
        ],
        out_specs=pl.BlockSpec((tm, hk), lambda i: (i, 0)),
        out_shape=jax.ShapeDtypeStruct((n, hk), F32),
        compiler_params=_params("parallel"),
        name="gla_decay",
    )(x, gain.reshape(1, d), w_alr_pad, wa2_pad, ba2.reshape(1, hk))


def _causal_conv(x, prev, w_ref, b_ref, seq_rows):
    r = x.shape[0]
    row = _iota((r, 1), 0)
    per_seq = prev.shape[0] == r
    pos = row & (seq_rows - 1) if per_seq else row
    acc = b_ref[...] + x * w_ref[CONV_W - 1:CONV_W, :]
    for j in range(1, CONV_W):
        if per_seq:
            hist = pltpu.roll(prev, (j - SUBLANES) % r, 0)
        else:
            hist = jnp.concatenate([pltpu.roll(prev, j, 0), x[SUBLANES:]], axis=0)
        shifted = jnp.where(pos < j, hist, pltpu.roll(x, j, 0))
        acc = acc + shifted * w_ref[CONV_W - 1 - j:CONV_W - j, :]
    return acc


def _lru_kernel(xa_ref, ga_ref, prev_ref, h0_ref, cw_ref, cb_ref, wa_ref, ba_ref, wx_ref, bx_ref, lam_ref,
                ya_ref, hl_ref, prev_scr, h_scr, a_scr, b_scr, *, nb, tc):
    c = pl.program_id(1)

    @pl.when(c == 0)
    def _():
        prev_scr[...] = prev_ref[...]
        h_scr[...] = h0_ref[...]

    x = xa_ref[...]
    xc = _causal_conv(x, prev_scr[...], cw_ref, cb_ref, tc)
    if nb == 1:
        prev_scr[...] = x[x.shape[0] - SUBLANES:]
    nblk = x.shape[1] // MXU_WIDTH
    xcb = xc.astype(BF16)
    r_parts, i_parts = [], []
    for t in range(nblk):
        sl = slice(t * MXU_WIDTH, (t + 1) * MXU_WIDTH)
        r_parts.append(jnp.dot(xcb[:, sl], wa_ref[t], preferred_element_type=F32))
        i_parts.append(jnp.dot(xcb[:, sl], wx_ref[t], preferred_element_type=F32))
    r_gate = jax.nn.sigmoid(jnp.concatenate(r_parts, axis=1) + ba_ref[...])
    i_gate = jax.nn.sigmoid(jnp.concatenate(i_parts, axis=1) + bx_ref[...])
    log_a = (-LRU_C) * r_gate * _softplus(-lam_ref[...])
    a = jnp.exp(log_a)
    a_scr[...] = a
    b_scr[...] = jnp.sqrt(-jnp.tanh(log_a) * (a * a + 1.0)) * (i_gate * xc)

    def seq_body(s, carry):
        def tile_body(tl, h):
            base = pl.multiple_of(s * tc + tl * SUBLANES, SUBLANES)
            for i in range(SUBLANES):
                h = a_scr[pl.ds(base + i, 1), :] * h + b_scr[pl.ds(base + i, 1), :]
                b_scr[pl.ds(base + i, 1), :] = h
            return h

        h_scr[s] = lax.fori_loop(0, tc // SUBLANES, tile_body, h_scr[s])
        return carry

    lax.fori_loop(0, nb, seq_body, 0, unroll=min(nb, 4))
    ya_ref[...] = b_scr[...] * jax.nn.gelu(ga_ref[...])

    @pl.when(c == pl.num_programs(1) - 1)
    def _():
        hl_ref[...] = h_scr[...]


def lru_branch(proj, prev8, h0, conv_w, conv_b, wa_t, ba, wx_t, bx, lam, *, nseq, t, nb, tc):
    n = proj.shape[0]
    w = h0.shape[1]
    nchunk = t // tc
    assert nseq % nb == 0 and t % tc == 0 and (nb == 1 or tc == t == SUBLANES)
    r = nb * tc
    nblk = w // MXU_WIDTH
    rows = lambda s, c: s * nchunk + c
    full2 = lambda s, c: (0, 0)
    full3 = lambda s, c: (0, 0, 0)
    kern = functools.partial(_lru_kernel, nb=nb, tc=tc)
    ya, hl = pl.pallas_call(
        kern,
        grid=(nseq // nb, nchunk),
        in_specs=[
            pl.BlockSpec((r, w), lambda s, c: (rows(s, c), 0)),
            pl.BlockSpec((r, w), lambda s, c: (rows(s, c), 1)),
            pl.BlockSpec((nb * SUBLANES, w), lambda s, c: (s, 0)),
            pl.BlockSpec((nb, 1, w), lambda s, c: (s, 0, 0)),
            pl.BlockSpec((CONV_W, w), full2),
            pl.BlockSpec((1, w), full2),
            pl.BlockSpec((nblk, MXU_WIDTH, MXU_WIDTH), full3),
            pl.BlockSpec((1, w), full2),
            pl.BlockSpec((nblk, MXU_WIDTH, MXU_WIDTH), full3),
            pl.BlockSpec((1, w), full2),
            pl.BlockSpec((1, w), full2),
        ],
        out_specs=[
            pl.BlockSpec((r, w), lambda s, c: (rows(s, c), 0)),
            pl.BlockSpec((nb, 1, w), lambda s, c: (s, 0, 0)),
        ],
        out_shape=[jax.ShapeDtypeStruct((n, w), F32), jax.ShapeDtypeStruct((nseq, 1, w), F32)],
        scratch_shapes=[
            pltpu.VMEM((nb * SUBLANES, w), F32),
            pltpu.VMEM((nb, 1, w), F32),
            pltpu.VMEM((r, w), F32),
            pltpu.VMEM((r, w), F32),
        ],
        compiler_params=_params("parallel", "arbitrary"),
        name="lru_branch",
    )(proj, proj, prev8, h0.reshape(nseq, 1, w), conv_w, conv_b.reshape(1, w), wa_t, ba.reshape(1, w), wx_t,
      bx.reshape(1, w), lam.reshape(1, w))
    return ya, hl.reshape(nseq, w)


def _gla_cumdecay(lg, run_rows):
    r = lg.shape[0]
    row = _iota((r, r), 0)
    col = _iota((r, r), 1)
    shift = run_rows.bit_length() - 1
    tri = jnp.logical_and(row >= col, (row >> shift) == (col >> shift))
    return _dot_split(tri.astype(F32), lg)


def _gla_near_att(qs, k, bc, sub):
    r = qs.shape[0]
    row = _iota((r, r), 0)
    col = _iota((r, r), 1)
    posr = _iota((r, 1), 0) & (sub - 1)
    att = jnp.zeros((r, r), F32)
    for d in range(sub):
        kd = k if d == 0 else pltpu.roll(k, d, 0)
        bcd = bc if d == 0 else pltpu.roll(bc, d, 0)
        valid = posr >= d
        prod = qs * kd * jnp.exp(jnp.where(valid, bc - bcd, 0.0))
        diag = jnp.sum(jnp.where(valid, prod, 0.0), axis=1, keepdims=True)
        att = att + jnp.where(col == row - d, diag, 0.0)
    return att


def _gla_far_att(qs, k, bc, sub):
    l, dk = qs.shape
    att = jnp.zeros((l, l), F32)
    for j in range(l // sub - 1):
        lo, hi = j * sub, (j + 1) * sub
        e_j = bc[hi - 1:hi, :]
        kp = k[lo:hi] * jnp.exp(e_j - bc[lo:hi])
        qp = qs[hi:] * jnp.exp(bc[hi:] - e_j)
        k_rows = [jnp.zeros((lo, dk), F32)] * (lo > 0) + [kp, jnp.zeros((l - hi, dk), F32)]
        att = att + _dot_nt(jnp.concatenate([jnp.zeros((hi, dk), F32), qp], axis=0), jnp.concatenate(k_rows, axis=0))
    return att


def _as_column(row_vec):
    d = row_vec.shape[1]
    eye = _iota((d, d), 0) == _iota((d, d), 1)
    return jnp.sum(jnp.where(eye, row_vec, 0.0), axis=1, keepdims=True)


def _gla_kernel(q_ref, k_ref, v_ref, g_ref, lg_ref, s0_ref, gn_ref, yb_ref, sn_ref, *scratch, nb, chunk, sub):
    c = pl.program_id(2)
    scale = q_ref.shape[1] ** -0.5

    def finish(o, g):
        on = o * lax.rsqrt(jnp.mean(o * o, axis=-1, keepdims=True) + EPS) * gn_ref[...]
        return on * jax.nn.silu(g)

    if nb == 1:
        (s_scr,) = scratch

        @pl.when(c == 0)
        def _():
            s_scr[...] = s0_ref[0, 0]

        def body(i, carry):
            rows = pl.ds(pl.multiple_of(i * chunk, chunk), chunk)
            qs = q_ref[rows, :] * scale
            k = k_ref[rows, :]
            v = v_ref[rows, :]
            bc = _gla_cumdecay(lg_ref[rows, :], chunk)
            s_state = s_scr[...]
            o = _dot(qs * jnp.exp(bc), s_state) + _dot(_gla_near_att(qs, k, bc, sub) + _gla_far_att(qs, k, bc, sub), v)
            bl = bc[chunk - 1:chunk, :]
            s_scr[...] = _as_column(jnp.exp(bl)) * s_state + _dot_tn(k * jnp.exp(bl - bc), v)
            yb_ref[rows, :] = finish(o, g_ref[rows, :])
            return carry

        lax.fori_loop(0, q_ref.shape[0] // chunk, body, 0, unroll=2)

        @pl.when(c == pl.num_programs(2) - 1)
        def _():
            sn_ref[0, 0] = s_scr[...]
    else:
        qe_scr, kd_scr, eb_scr, o_scr = scratch
        r = q_ref.shape[0]
        qs = q_ref[...] * scale
        k = k_ref[...]
        bc = _gla_cumdecay(lg_ref[...], chunk)
        o_scr[...] = _dot(_gla_near_att(qs, k, bc, sub), v_ref[...])
        qe_scr[...] = qs * jnp.exp(bc)
        row = _iota((r, r), 0)
        col = _iota((r, r), 1)
        last = (col == (row | (chunk - 1))).astype(F32)
        bl = _dot_split(last, bc)
        kd_scr[...] = k * jnp.exp(bl - bc)
        eb_scr[...] = jnp.exp(bl)

        def body(j, carry):
            rows = pl.ds(pl.multiple_of(j * chunk, chunk), chunk)
            s_state = s0_ref[j, 0]
            o_scr[rows, :] += _dot(qe_scr[rows, :], s_state)
            decay = _as_column(eb_scr[pl.ds(pl.multiple_of(j * chunk, chunk), 1), :])
            sn_ref[j, 0] = decay * s_state + _dot_tn(kd_scr[rows, :], v_ref[rows, :])
            return carry

        lax.fori_loop(0, nb, body, 0, unroll=min(nb, 4))
        yb_ref[...] = finish(o_scr[...], g_ref[...])


def gla_branch(proj, lg, s0, gnorm, *, nseq, t, nb, rows_blk, chunk, sub, q_off, k_off, v_off, g_off):
    n = proj.shape[0]
    _, nh, dk, dv = s0.shape
    ntb = t // rows_blk if nb == 1 else 1
    r = rows_blk if nb == 1 else nb * t
    assert (nb == 1 and t % rows_blk == 0 and rows_blk % chunk == 0) or (chunk == sub == t and nseq % nb == 0)
    rows = lambda s, h, c: s * ntb + c
    kern = functools.partial(_gla_kernel, nb=nb, chunk=chunk, sub=sub)
    if nb == 1:
        scratch = [pltpu.VMEM((dk, dv), F32)]
    else:
        scratch = [pltpu.VMEM((r, dk), F32), pltpu.VMEM((r, dk), F32), pltpu.VMEM((r, dk), F32), pltpu.VMEM((r, dv), F32)]
    yb, sn = pl.pallas_call(
        kern,
        grid=(nseq // nb, nh, ntb),
        in_specs=[
            pl.BlockSpec((r, dk), lambda s, h, c: (rows(s, h, c), q_off // dk + h)),
            pl.BlockSpec((r, dk), lambda s, h, c: (rows(s, h, c), k_off // dk + h)),
            pl.BlockSpec((r, dv), lambda s, h, c: (rows(s, h, c), v_off // dv + h)),
            pl.BlockSpec((r, dv), lambda s, h, c: (rows(s, h, c), g_off // dv + h)),
            pl.BlockSpec((r, dk), lambda s, h, c: (rows(s, h, c), h)),
            pl.BlockSpec((nb, 1, dk, dv), lambda s, h, c: (s, h, 0, 0)),
            pl.BlockSpec((1, dv), lambda s, h, c: (0, h)),
        ],
        out_specs=[
            pl.BlockSpec((r, dv), lambda s, h, c: (rows(s, h, c), h)),
            pl.BlockSpec((nb, 1, dk, dv), lambda s, h, c: (s, h, 0, 0)),
        ],
        out_shape=[jax.ShapeDtypeStruct((n, nh * dv), F32), jax.ShapeDtypeStruct(s0.shape, F32)],
        scratch_shapes=scratch,
        compiler_params=_params("parallel", "parallel", "arbitrary"),
        name="gla_branch",
    )(proj, proj, proj, proj, lg, s0, gnorm.reshape(1, nh * dv))
    return yb, sn


def _mlstm_pre_kernel(xm_ref, prev_ref, cw_ref, cb_ref, wq_ref, wk_ref, wv_ref, wg_ref, bg_ref,
                      q_ref, k_ref, v_ref, xc_ref, gate_ref, prev_scr, *, nb, tc, n_heads):
    c = pl.program_id(1)

    @pl.when(c == 0)
    def _():
        prev_scr[...] = prev_ref[...]

    x = xm_ref[...]
    xc = jax.nn.silu(_causal_conv(x, prev_scr[...], cw_ref, cb_ref, tc))
    if nb == 1:
        prev_scr[...] = x[x.shape[0] - SUBLANES:]
    xc_ref[...] = xc
    di = x.shape[1]
    xcb = xc.astype(BF16)
    xb = x.astype(BF16)
    for t in range(di // MXU_WIDTH):
        sl = slice(t * MXU_WIDTH, (t + 1) * MXU_WIDTH)
        q_ref[:, sl] = jnp.dot(xcb[:, sl], wq_ref[t], preferred_element_type=F32)
        k_ref[:, sl] = jnp.dot(xcb[:, sl], wk_ref[t], preferred_element_type=F32)
        v_ref[:, sl] = jnp.dot(xb[:, sl], wv_ref[t], preferred_element_type=F32)
    pre = (_dot(q_ref[...], wg_ref[0:di, :]) + _dot(k_ref[...], wg_ref[di:2 * di, :])
           + _dot(v_ref[...], wg_ref[2 * di:3 * di, :]) + bg_ref[...])
    gate_ref[...] = jnp.where(_iota(pre.shape, 1) < n_heads, pre, _log_sigmoid(pre))


def mlstm_pre(up, prev8, conv_w, conv_b, wq_t, wk_t, wv_t, wg, bg, *, nseq, t, nb, tc, n_heads):
    n = up.shape[0]
    di = conv_w.shape[1]
    nchunk = t // tc
    assert nseq % nb == 0 and t % tc == 0 and (nb == 1 or tc == t == SUBLANES)
    r = nb * tc
    nblk = di // MXU_WIDTH
    rows = lambda s, c: (s * nchunk + c, 0)
    full2 = lambda s, c: (0, 0)
    full3 = lambda s, c: (0, 0, 0)
    wide = jax.ShapeDtypeStruct((n, di), F32)
    kern = functools.partial(_mlstm_pre_kernel, nb=nb, tc=tc, n_heads=n_heads)
    return pl.pallas_call(
        kern,
        grid=(nseq // nb, nchunk),
        in_specs=[
            pl.BlockSpec((r, di), rows),
            pl.BlockSpec((nb * SUBLANES, di), lambda s, c: (s, 0)),
            pl.BlockSpec((CONV_W, di), full2),
            pl.BlockSpec((1, di), full2),
            pl.BlockSpec((nblk, MXU_WIDTH, MXU_WIDTH), full3),
            pl.BlockSpec((nblk, MXU_WIDTH, MXU_WIDTH), full3),
            pl.BlockSpec((nblk, MXU_WIDTH, MXU_WIDTH), full3),
            pl.BlockSpec((3 * di, LANES), full2),
            pl.BlockSpec((1, LANES), full2),
        ],
        out_specs=[pl.BlockSpec((r, di), rows)] * 4 + [pl.BlockSpec((r, LANES), rows)],
        out_shape=[wide, wide, wide, wide, jax.ShapeDtypeStruct((n, LANES), F32)],
        scratch_shapes=[pltpu.VMEM((nb * SUBLANES, di), F32)],
        compiler_params=_params("parallel", "arbitrary"),
        name="mlstm_pre",
    )(up, prev8, conv_w, conv_b.reshape(1, di), wq_t, wk_t, wv_t, wg, bg)


def _mlstm_chunk(q, k, v, igc, lfc, xc, z, skip, ng, m_prev, n_prev, c_load, c_store):
    l, dh = q.shape
    row = _iota((l, l), 0)
    col = _iota((l, l), 1)
    tril = row >= col
    f_b = jnp.broadcast_to(lfc, (l, l))
    i_b = jnp.broadcast_to(igc, (l, l))
    f_col = _dot_split(tril.astype(F32), f_b)
    row_term = _dot_split(jnp.ones((l, l), F32),
                          jnp.where(row == col, i_b, 0.0) - jnp.where(row <= col, f_b, 0.0))
    dm = jnp.where(tril, f_col + row_term, -jnp.inf)
    fcum = f_col[:, 0:1]
    prev = m_prev + fcum
    mt = jnp.maximum(prev, jnp.max(dm, axis=1, keepdims=True))
    wprev = jnp.exp(prev - mt)
    smat = _dot_nt(q, k) * jnp.exp(dm - mt)
    den = wprev * jnp.sum(q * n_prev, axis=1, keepdims=True) + jnp.sum(smat, axis=1, keepdims=True)
    inv = 1.0 / jnp.maximum(jnp.abs(den), jnp.exp(-mt))
    f_last = fcum[l - 1:l, :]
    m_last = mt[l - 1:l, :]
    w_c = jnp.exp(m_prev + f_last - m_last)
    kw = k * jnp.exp(f_last - fcum + igc - m_last)
    qb = q.astype(BF16)
    sb = smat.astype(BF16)
    kwb = kw.astype(BF16)
    parts = []
    for t in range(dh // MXU_WIDTH):
        cols = slice(t * MXU_WIDTH, (t + 1) * MXU_WIDTH)
        c_blk = c_load(cols)
        vb = v[:, cols].astype(BF16)
        num = (wprev * jnp.dot(qb, c_blk.astype(BF16), preferred_element_type=F32)
               + jnp.dot(sb, vb, preferred_element_type=F32))
        parts.append(num * inv)
        c_store(cols, w_c * c_blk + _dot_tn(kwb, vb))
    n_new = w_c * n_prev + jnp.sum(kw, axis=0, keepdims=True)
    hh = jnp.concatenate(parts, axis=1)
    hc = hh - jnp.mean(hh, axis=-1, keepdims=True)
    hn = hc * lax.rsqrt(jnp.mean(hc * hc, axis=-1, keepdims=True) + EPS) * ng
    return (hn + skip * xc) * jax.nn.silu(z), n_new, m_last


def _mlstm_seq_kernel(*refs, chunk, n_heads, zero_state):
    q_ref, k_ref, v_ref, gate_ref, xc_ref, z_ref = refs[:6]
    pos = 6
    if not zero_state:
        c0_ref, n0_ref, m0_ref = refs[pos:pos + 3]
        pos += 3
    skip_ref, ng_ref, out_ref, cn_ref, nn_ref, mn_ref, c_scr, n_scr, m_scr = refs[pos:pos + 9]
    head = pl.program_id(1)
    c = pl.program_id(2)

    @pl.when(c == 0)
    def _():
        if zero_state:
            c_scr[...] = jnp.zeros_like(c_scr)
            n_scr[...] = jnp.zeros_like(n_scr)
            m_scr[...] = jnp.zeros_like(m_scr)
        else:
            c_scr[...] = c0_ref[0, 0]
            n_scr[...] = n0_ref[0]
            m_scr[...] = m0_ref[0]

    kscale = q_ref.shape[1] ** -0.5

    def c_store(cols, value):
        c_scr[:, cols] = value

    def body(i, carry):
        rows = pl.ds(pl.multiple_of(i * chunk, chunk), chunk)
        gates = gate_ref[rows, :]
        out, n_new, m_last = _mlstm_chunk(
            q_ref[rows, :], k_ref[rows, :] * kscale, v_ref[rows, :], _lane_column(gates, head),
            _lane_column(gates, head + n_heads), xc_ref[rows, :], z_ref[rows, :], skip_ref[...], ng_ref[...],
            m_scr[:, 0:1], n_scr[...], lambda cols: c_scr[:, cols], c_store)
        out_ref[rows, :] = out
        n_scr[...] = n_new
        m_scr[...] = jnp.broadcast_to(m_last, m_scr.shape)
        return carry

    lax.fori_loop(0, q_ref.shape[0] // chunk, body, 0)

    @pl.when(c == pl.num_programs(2) - 1)
    def _():
        cn_ref[0, 0] = c_scr[...]
        nn_ref[0] = n_scr[...]
        mn_ref[0] = m_scr[...]


def _mlstm_step_kernel(q_ref, k_ref, v_ref, gate_ref, xc_ref, z_ref, c0_ref, n0_ref, m0_ref, skip_ref, ng_ref,
                       out_ref, cn_ref, nn_ref, mn_ref, *, hb, n_heads, head_block=None):
    hblk = pl.program_id(1) if head_block is None else head_block
    dh = c0_ref.shape[2]
    kscale = dh ** -0.5
    gates = gate_ref[...]
    for hh in range(hb):
        cols_h = slice(hh * dh, (hh + 1) * dh)
        head = hblk * hb + hh

        def c_store(cols, value, hh=hh):
            cn_ref[0, hh, :, cols] = value

        out, n_new, m_last = _mlstm_chunk(
            q_ref[:, cols_h], k_ref[:, cols_h] * kscale, v_ref[:, cols_h], _lane_column(gates, head),
            _lane_column(gates, head + n_heads), xc_ref[:, cols_h], z_ref[:, cols_h], skip_ref[:, cols_h],
            ng_ref[:, cols_h], m0_ref[hh][:, 0:1], n0_ref[hh], lambda cols, hh=hh: c0_ref[0, hh, :, cols], c_store)
        out_ref[:, cols_h] = out
        nn_ref[hh] = n_new
        mn_ref[hh] = jnp.broadcast_to(m_last, (1, LANES))


def mlstm_recurrence(q, k, v, gates, xc, up, c0, n0, m0, skip, norm_g, *, nseq, t, nh, rows_blk, chunk, hb):
    n, di = q.shape
    dh = di // nh
    zero_state = c0 is None
    out_shape = [
        jax.ShapeDtypeStruct((n, di), F32),
        jax.ShapeDtypeStruct((nseq, nh, dh, dh), F32),
        jax.ShapeDtypeStruct((nseq * nh, 1, dh), F32),
        jax.ShapeDtypeStruct((nseq * nh, 1, LANES), F32),
    ]
    state_args = []
    if not zero_state:
        state_args = [c0, n0.reshape(nseq * nh, 1, dh),
                      jnp.broadcast_to(m0.reshape(nseq * nh, 1, 1), (nseq * nh, 1, LANES))]
    if t == chunk and not zero_state:
        assert nh % hb == 0
        nhb = nh // hb
        wide = lambda s, h: (s, h)
        per_head = lambda s, h: (s * nhb + h, 0, 0)
        mat = lambda s, h: (s, h, 0, 0)
        out, cn, nn, mn = pl.pallas_call(
            functools.partial(_mlstm_step_kernel, hb=hb, n_heads=nh),
            grid=(nseq, nhb),
            in_specs=[pl.BlockSpec((t, hb * dh), wide)] * 3
            + [pl.BlockSpec((t, LANES), lambda s, h: (s, 0)), pl.BlockSpec((t, hb * dh), wide),
               pl.BlockSpec((t, hb * dh), lambda s, h: (s, nhb + h)), pl.BlockSpec((1, hb, dh, dh), mat),
               pl.BlockSpec((hb, 1, dh), per_head), pl.BlockSpec((hb, 1, LANES), per_head),
               pl.BlockSpec((1, hb * dh), lambda s, h: (0, h)), pl.BlockSpec((1, hb * dh), lambda s, h: (0, h))],
            out_specs=[pl.BlockSpec((t, hb * dh), wide), pl.BlockSpec((1, hb, dh, dh), mat),
                       pl.BlockSpec((hb, 1, dh), per_head), pl.BlockSpec((hb, 1, LANES), per_head)],
            out_shape=out_shape,
            compiler_params=_params("parallel", "arbitrary"),
            name="mlstm_step",
        )(q, k, v, gates, xc, up, *state_args, skip.reshape(1, di), norm_g.reshape(1, di))
    else:
        ntb = t // rows_blk
        assert t % rows_blk == 0 and rows_blk % chunk == 0
        rows = lambda s, h, c: (s * ntb + c, h)
        per_head = lambda s, h, c: (s * nh + h, 0, 0)
        mat = lambda s, h, c: (s, h, 0, 0)
        state_specs = [] if zero_state else [pl.BlockSpec((1, 1, dh, dh), mat), pl.BlockSpec((1, 1, dh), per_head),
                                             pl.BlockSpec((1, 1, LANES), per_head)]
        out, cn, nn, mn = pl.pallas_call(
            functools.partial(_mlstm_seq_kernel, chunk=chunk, n_heads=nh, zero_state=zero_state),
            grid=(nseq, nh, ntb),
            in_specs=[pl.BlockSpec((rows_blk, dh), rows)] * 3
            + [pl.BlockSpec((rows_blk, LANES), lambda s, h, c: (s * ntb + c, 0)), pl.BlockSpec((rows_blk, dh), rows),
               pl.BlockSpec((rows_blk, dh), lambda s, h, c: (s * ntb + c, nh + h))]
            + state_specs
            + [pl.BlockSpec((1, dh), lambda s, h, c: (0, h)), pl.BlockSpec((1, dh), lambda s, h, c: (0, h))],
            out_specs=[pl.BlockSpec((rows_blk, dh), rows), pl.BlockSpec((1, 1, dh, dh), mat),
                       pl.BlockSpec((1, 1, dh), per_head), pl.BlockSpec((1, 1, LANES), per_head)],
            out_shape=out_shape,
            scratch_shapes=[pltpu.VMEM((dh, dh), F32), pltpu.VMEM((1, dh), F32), pltpu.VMEM((1, LANES), F32)],
            compiler_params=_params("parallel", "parallel", "arbitrary"),
            name="mlstm_seq",
        )(q, k, v, gates, xc, up, *state_args, skip.reshape(1, di), norm_g.reshape(1, di))
    return out, cn, nn.reshape(nseq, nh, dh), mn[:, 0, 0].reshape(nseq, nh)


def _mlstm_dual_kernel(ql_ref, kl_ref, vl_ref, gl_ref, xcl_ref, zl_ref, skipl_ref, ngl_ref,
                       qs_ref, ks_ref, vs_ref, gs_ref, xcs_ref, zs_ref, c0_ref, n0_ref, m0_ref, skips_ref, ngs_ref,
                       outl_ref, cnl_ref, nnl_ref, mnl_ref, outs_ref, cns_ref, nns_ref, mns_ref, n_scr, m_scr,
                       *, hb, n_heads, chunks_per_seq):
    g = pl.program_id(0)
    c = g % chunks_per_seq
    head_l = (g // chunks_per_seq) % n_heads
    dh = ql_ref.shape[1]
    kscale = dh ** -0.5

    @pl.when(c == 0)
    def _():
        cnl_ref[...] = jnp.zeros_like(cnl_ref)
        n_scr[...] = jnp.zeros_like(n_scr)
        m_scr[...] = jnp.zeros_like(m_scr)

    def cl_store(cols, value):
        cnl_ref[0, 0, :, cols] = value

    gl = gl_ref[...]
    out, n_new, m_last = _mlstm_chunk(
        ql_ref[...], kl_ref[...] * kscale, vl_ref[...], _lane_column(gl, head_l), _lane_column(gl, head_l + n_heads),
        xcl_ref[...], zl_ref[...], skipl_ref[...], ngl_ref[...], m_scr[:, 0:1], n_scr[...],
        lambda cols: cnl_ref[0, 0, :, cols], cl_store)
    outl_ref[...] = out
    n_scr[...] = n_new
    m_scr[...] = jnp.broadcast_to(m_last, m_scr.shape)

    @pl.when(c == chunks_per_seq - 1)
    def _():
        nnl_ref[0] = n_scr[...]
        mnl_ref[0] = m_scr[...]

    _mlstm_step_kernel(qs_ref, ks_ref, vs_ref, gs_ref, xcs_ref, zs_ref, c0_ref, n0_ref, m0_ref, skips_ref, ngs_ref,
                       outs_ref, cns_ref, nns_ref, mns_ref, hb=hb, n_heads=n_heads, head_block=g % (n_heads // hb))


def dual_chunk_rows(nseq_l, t_l, nseq_s, nh, hb):
    steps = nseq_s * (nh // hb)
    total = nseq_l * t_l * nh
    if nh % hb or total % steps:
        return None
    chunk = total // steps
    ok = chunk % SUBLANES == 0 and t_l % chunk == 0 and 64 <= chunk <= M_CHUNK
    return chunk if ok else None


def mlstm_recurrence_dual(long_in, short_in, c0, n0, m0, skip, norm_g, *, nseq_l, t_l, nseq_s, t_s, nh, hb, chunk):
    di = long_in[0].shape[1]
    dh = di // nh
    nhb = nh // hb
    cps = t_l // chunk
    steps = nseq_s * nhb
    assert steps == nseq_l * nh * cps

    def rows_l(g):
        return (g // (nh * cps)) * cps + g % cps

    head_l = lambda g: (g // cps) % nh
    tile_l = lambda g: (rows_l(g), head_l(g))
    per_head_l = lambda g: (g // cps, 0, 0)
    tile_s = lambda g: (g // nhb, g % nhb)
    per_head_s = lambda g: (g, 0, 0)
    mat_s = lambda g: (g // nhb, g % nhb, 0, 0)
    n_l, n_s = long_in[0].shape[0], short_in[0].shape[0]
    in_specs = (
        [pl.BlockSpec((chunk, dh), tile_l)] * 3
        + [pl.BlockSpec((chunk, LANES), lambda g: (rows_l(g), 0)), pl.BlockSpec((chunk, dh), tile_l),
           pl.BlockSpec((chunk, dh), lambda g: (rows_l(g), nh + head_l(g))),
           pl.BlockSpec((1, dh), lambda g: (0, head_l(g))), pl.BlockSpec((1, dh), lambda g: (0, head_l(g)))]
        + [pl.BlockSpec((t_s, hb * dh), tile_s)] * 3
        + [pl.BlockSpec((t_s, LANES), lambda g: (g // nhb, 0)), pl.BlockSpec((t_s, hb * dh), tile_s),
           pl.BlockSpec((t_s, hb * dh), lambda g: (g // nhb, nhb + g % nhb)), pl.BlockSpec((1, hb, dh, dh), mat_s),
           pl.BlockSpec((hb, 1, dh), per_head_s), pl.BlockSpec((hb, 1, LANES), per_head_s),
           pl.BlockSpec((1, hb * dh), lambda g: (0, g % nhb)), pl.BlockSpec((1, hb * dh), lambda g: (0, g % nhb))])
    out_specs = [
        pl.BlockSpec((chunk, dh), tile_l), pl.BlockSpec((1, 1, dh, dh), lambda g: (g // (nh * cps), head_l(g), 0, 0)),
        pl.BlockSpec((1, 1, dh), per_head_l), pl.BlockSpec((1, 1, LANES), per_head_l),
        pl.BlockSpec((t_s, hb * dh), tile_s), pl.BlockSpec((1, hb, dh, dh), mat_s),
        pl.BlockSpec((hb, 1, dh), per_head_s), pl.BlockSpec((hb, 1, LANES), per_head_s)]
    out_shape = [
        jax.ShapeDtypeStruct((n_l, di), F32), jax.ShapeDtypeStruct((nseq_l, nh, dh, dh), F32),
        jax.ShapeDtypeStruct((nseq_l * nh, 1, dh), F32), jax.ShapeDtypeStruct((nseq_l * nh, 1, LANES), F32),
        jax.ShapeDtypeStruct((n_s, di), F32), jax.ShapeDtypeStruct((nseq_s, nh, dh, dh), F32),
        jax.ShapeDtypeStruct((nseq_s * nh, 1, dh), F32), jax.ShapeDtypeStruct((nseq_s * nh, 1, LANES), F32)]
    skip2, ng2 = skip.reshape(1, di), norm_g.reshape(1, di)
    res = pl.pallas_call(
        functools.partial(_mlstm_dual_kernel, hb=hb, n_heads=nh, chunks_per_seq=cps),
        grid=(steps,),
        in_specs=in_specs,
        out_specs=out_specs,
        out_shape=out_shape,
        scratch_shapes=[pltpu.VMEM((1, dh), F32), pltpu.VMEM((1, LANES), F32)],
        compiler_params=_params("arbitrary"),
        name="mlstm_dual",
    )(*long_in, skip2, ng2, *short_in, c0, n0.reshape(nseq_s * nh, 1, dh),
      jnp.broadcast_to(m0.reshape(nseq_s * nh, 1, 1), (nseq_s * nh, 1, LANES)), skip2, ng2)
    unpack = lambda o, cn, nn, mn, nseq: (o, cn, nn.reshape(nseq, nh, dh), mn[:, 0, 0].reshape(nseq, nh))
    return unpack(*res[:4], nseq_l), unpack(*res[4:], nseq_s)


def _moe_expert_kernel(*refs, has_final):
    xn_ref, gates_ref, wg_ref, wu_ref, wd_ref, x_ref = refs[:6]
    fg_ref = refs[6] if has_final else None
    o_ref = refs[6 + int(has_final)]
    e = pl.program_id(1)

    @pl.when(e == 0)
    def _():
        o_ref[...] = x_ref[...]

    xn = xn_ref[...]
    hg = jnp.dot(xn, wg_ref[0, 0].astype(BF16), preferred_element_type=F32)
    hu = jnp.dot(xn, wu_ref[0, 0].astype(BF16), preferred_element_type=F32)
    h = jax.nn.silu(hg) * hu * _lane_column(gates_ref[...], e)
    o_ref[...] += _dot(h, wd_ref[0, 0])

    if has_final:
        @pl.when(e == pl.num_programs(1) - 1)
        def _():
            o_ref[...] = _rmsnorm_rows(o_ref[...], fg_ref[...])


def moe_experts(xn, gates, w_gate, w_up, w_down, layer, x, final_gain=None, tm=1024):
    n, d = x.shape
    _, ne, _, f = w_gate.shape
    tm = _row_tile(n, tm)
    in_specs = [
        pl.BlockSpec((tm, d), lambda i, e: (i, 0)),
        pl.BlockSpec((tm, LANES), lambda i, e: (i, 0)),
        pl.BlockSpec((1, 1, d, f), lambda i, e: (layer, e, 0, 0)),
        pl.BlockSpec((1, 1, d, f), lambda i, e: (layer, e, 0, 0)),
        pl.BlockSpec((1, 1, f, d), lambda i, e: (layer, e, 0, 0)),
        pl.BlockSpec((tm, d), lambda i, e: (i, 0)),
    ]
    args = [xn, gates, w_gate, w_up, w_down, x]
    if final_gain is not None:
        in_specs.append(pl.BlockSpec((1, d), lambda i, e: (0, 0)))
        args.append(final_gain.reshape(1, d))
    return pl.pallas_call(
        functools.partial(_moe_expert_kernel, has_final=final_gain is not None),
        grid=(n // tm, ne),
        in_specs=in_specs,
        out_specs=pl.BlockSpec((tm, d), lambda i, e: (i, 0)),
        out_shape=jax.ShapeDtypeStruct((n, d), F32),
        compiler_params=_params("parallel", "arbitrary"),
        name="moe_experts",
    )(*args)


def _blockdiag_tiles(w):
    nblocks, bi, bo = w.shape
    per = MXU_WIDTH // bi
    ntiles = nblocks // per
    rows_of_tile = w.reshape(ntiles, MXU_WIDTH, bo)
    spread = jnp.broadcast_to(rows_of_tile[:, :, None, :], (ntiles, MXU_WIDTH, per, bo)).reshape(
        ntiles, MXU_WIDTH, MXU_WIDTH)
    on_diag = (_iota((MXU_WIDTH, MXU_WIDTH), 0) // bi) == (_iota((MXU_WIDTH, MXU_WIDTH), 1) // bo)
    return jnp.where(on_diag, spread, 0.0).astype(BF16)


def _pad_cols(w, width=LANES):
    return jnp.pad(w, ((0, 0), (0, width - w.shape[1])))


def _history_tiles(buf):
    nseq, hist, ch = buf.shape
    return jnp.pad(buf, ((0, 0), (SUBLANES - hist, 0), (0, 0))).reshape(nseq * SUBLANES, ch)


def _block_plan(nseq, t):
    long_seq = t > SUBLANES
    return dict(
        lru=dict(nb=1 if long_seq else min(nseq, 64), tc=min(t, 256)),
        gla=dict(nb=1 if long_seq else min(nseq, 16), rows_blk=min(t, 512), chunk=min(t, GLA_CHUNK), sub=min(t, GLA_SUB)),
        pre=dict(nb=1 if long_seq else min(nseq, 16), tc=min(t, 128)),
        rec=dict(rows_blk=min(t, M_CHUNK), chunk=min(t, M_CHUNK), hb=2),
    )


_STATE_KEYS = ("lru_conv", "lru_h", "gla_s", "m_conv", "m_c", "m_n", "m_m")


class _Group:
    def __init__(self, x3, states):
        self.nseq, self.t, self.d = x3.shape
        self.x = x3.reshape(self.nseq * self.t, self.d)
        self.st = dict(zip(_STATE_KEYS, states))
        self.plan = _block_plan(self.nseq, self.t)
        self.outs = {k: [] for k in _STATE_KEYS}
        self.xn = self.gates = None

    def result(self):
        return (self.x.reshape(self.nseq, self.t, self.d),) + tuple(jnp.stack(self.outs[k]) for k in _STATE_KEYS)


def _trunks(groups,
            norm_mix_g, norm_ffn_g, norm_final_g,
            l0_w_in, l0_lru_conv_w, l0_lru_conv_b, l0_lru_wa, l0_lru_ba, l0_lru_wx, l0_lru_bx, l0_lru_lam,
            l0_gla_wa2, l0_gla_ba2, l0_gla_norm_g, l0_w_out,
            l1_w_up, l1_conv_w, l1_conv_b, l1_wq, l1_wk, l1_wv, l1_w_ig, l1_b_ig, l1_w_fg, l1_b_fg, l1_skip,
            l1_norm_g, l1_w_down,
            moe_w_rg, moe_b_rg, moe_w_re, moe_b_re, moe_w_gate, moe_w_up, moe_w_down):
    depth = norm_mix_g.shape[0]
    hist = CONV_W - 1
    for layer in range(depth):
        j = layer // 2
        wr = _pad_cols(jnp.concatenate([moe_w_re[layer], moe_w_rg[layer]], axis=1)).astype(BF16)
        br = _pad_cols(jnp.concatenate([moe_b_re[layer], moe_b_rg[layer]])[None, :])
        router = (norm_ffn_g[layer], wr, br)
        if layer % 2 == 0:
            w_in = l0_w_in[j].astype(BF16)
            w_out = l0_w_out[j].astype(BF16)
            wa_t, wx_t = _blockdiag_tiles(l0_lru_wa[j]), _blockdiag_tiles(l0_lru_wx[j])
            rank = l0_gla_wa2.shape[1]
            wa2 = jnp.pad(l0_gla_wa2[j], ((0, LANES - rank), (0, 0))).astype(BF16)
            for g in groups:
                w = g.st["lru_h"].shape[-1]
                _, _, nh, dk, dv = g.st["gla_s"].shape
                main = 2 * w + 2 * nh * dk + 2 * nh * dv
                proj = fused_linear([g.x], w_in, n_out=main, gain=norm_mix_g[layer], name="linear_in")
                lg = gla_decay(g.x, norm_mix_g[layer], _pad_cols(w_in[:, main:main + rank]), wa2, l0_gla_ba2[j])
                ya, h_last = lru_branch(
                    proj, _history_tiles(g.st["lru_conv"][j]), g.st["lru_h"][j], l0_lru_conv_w[j], l0_lru_conv_b[j],
                    wa_t, l0_lru_ba[j], wx_t, l0_lru_bx[j], l0_lru_lam[j], nseq=g.nseq, t=g.t, **g.plan["lru"])
                yb, s_new = gla_branch(
                    proj, lg, g.st["gla_s"][j], l0_gla_norm_g[j], nseq=g.nseq, t=g.t, **g.plan["gla"], q_off=2 * w,
                    k_off=2 * w + nh * dk, v_off=2 * w + 2 * nh * dk, g_off=2 * w + 2 * nh * dk + nh * dv)
                g.x, g.xn, g.gates = fused_linear([ya, yb], w_out, n_out=g.d, res=g.x, router=router, tm=512,
                                                  tn=g.d, name="linear_out")
                g.outs["lru_conv"].append(proj.reshape(g.nseq, g.t, main)[:, g.t - hist:, :w])
                g.outs["lru_h"].append(h_last)
                g.outs["gla_s"].append(s_new)
        else:
            w_up = l1_w_up[j].astype(BF16)
            w_down = l1_w_down[j].astype(BF16)
            wq_t, wk_t, wv_t = (_blockdiag_tiles(m[j]) for m in (l1_wq, l1_wk, l1_wv))
            wg = _pad_cols(jnp.concatenate([l1_w_ig[j], l1_w_fg[j]], axis=1)).astype(BF16)
            bg = _pad_cols(jnp.concatenate([l1_b_ig[j], l1_b_fg[j]])[None, :])
            fronts = []
            for g in groups:
                _, _, nh, dh = g.st["m_n"].shape
                up = fused_linear([g.x], w_up, n_out=2 * nh * dh, gain=norm_mix_g[layer], name="linear_up")
                q, k, v, xc, gates_m = mlstm_pre(
                    up, _history_tiles(g.st["m_conv"][j]), l1_conv_w[j], l1_conv_b[j], wq_t, wk_t, wv_t, wg, bg,
                    nseq=g.nseq, t=g.t, n_heads=nh, **g.plan["pre"])
                fronts.append((q, k, v, gates_m, xc, up))
                g.outs["m_conv"].append(up.reshape(g.nseq, g.t, 2 * nh * dh)[:, g.t - hist:, :nh * dh])
            recs = _mlstm_recurrences(groups, fronts, j, l1_skip[j], l1_norm_g[j])
            for g, (hout, c_new, n_new, m_new) in zip(groups, recs):
                g.x, g.xn, g.gates = fused_linear([hout], w_down, n_out=g.d, res=g.x, router=router, tm=256,
                                                  tn=g.d, name="linear_down")
                g.outs["m_c"].append(c_new)
                g.outs["m_n"].append(n_new)
                g.outs["m_m"].append(m_new)
        for g in groups:
            g.x = moe_experts(g.xn, g.gates, moe_w_gate, moe_w_up, moe_w_down, layer, g.x,
                              final_gain=norm_final_g if layer == depth - 1 else None)
    return [g.result() for g in groups]


def _mlstm_recurrences(groups, fronts, j, skip, norm_g):
    nh = groups[0].st["m_n"].shape[2]
    hb = groups[0].plan["rec"]["hb"]
    if len(groups) == 2:
        for il, i_s in ((0, 1), (1, 0)):
            gl, gs = groups[il], groups[i_s]
            fits = gl.st["m_c"] is None and gs.st["m_c"] is not None and gs.t == SUBLANES and gl.t > SUBLANES
            chunk = dual_chunk_rows(gl.nseq, gl.t, gs.nseq, nh, hb) if fits else None
            if chunk is not None:
                rl, rs = mlstm_recurrence_dual(
                    fronts[il], fronts[i_s], gs.st["m_c"][j], gs.st["m_n"][j], gs.st["m_m"][j], skip, norm_g,
                    nseq_l=gl.nseq, t_l=gl.t, nseq_s=gs.nseq, t_s=gs.t, nh=nh, hb=hb, chunk=chunk)
                return [rl, rs] if il == 0 else [rs, rl]
    return [mlstm_recurrence(*f, None if g.st["m_c"] is None else g.st["m_c"][j], g.st["m_n"][j], g.st["m_m"][j],
                             skip, norm_g, nseq=g.nseq, t=g.t, nh=nh, **g.plan["rec"])
            for g, f in zip(groups, fronts)]


def _trunk(x3, *states_and_weights):
    return _trunks([_Group(x3, states_and_weights[:7])], *states_and_weights[7:])[0]


def kernel(x_prompt, x_sample, state_lru_conv, state_lru_h, state_gla_S, state_mlstm_conv, state_mlstm_C,
           state_mlstm_n, state_mlstm_m, norm_mix_g, norm_ffn_g, norm_final_g, l0_w_in, l0_lru_conv_w,
           l0_lru_conv_b, l0_lru_wa, l0_lru_ba, l0_lru_wx, l0_lru_bx, l0_lru_lam, l0_gla_wa2, l0_gla_ba2,
           l0_gla_norm_g, l0_w_out, l1_w_up, l1_conv_w, l1_conv_b, l1_wq, l1_wk, l1_wv, l1_w_ig, l1_b_ig,
           l1_w_fg, l1_b_fg, l1_skip, l1_norm_g, l1_w_down, moe_w_rg, moe_b_rg, moe_w_re, moe_b_re, moe_w_gate,
           moe_w_up, moe_w_down):
    moe_w_gate, moe_w_up, moe_w_down = (w.astype(BF16) for w in (moe_w_gate, moe_w_up, moe_w_down))
    weights = (norm_mix_g, norm_ffn_g, norm_final_g, l0_w_in, l0_lru_conv_w, l0_lru_conv_b, l0_lru_wa, l0_lru_ba,
               l0_lru_wx, l0_lru_bx, l0_lru_lam, l0_gla_wa2, l0_gla_ba2, l0_gla_norm_g, l0_w_out, l1_w_up,
               l1_conv_w, l1_conv_b, l1_wq, l1_wk, l1_wv, l1_w_ig, l1_b_ig, l1_w_fg, l1_b_fg, l1_skip, l1_norm_g,
               l1_w_down, moe_w_rg, moe_b_rg, moe_w_re, moe_b_re, moe_w_gate, moe_w_up, moe_w_down)
    states = (state_lru_conv, state_lru_h, state_gla_S, state_mlstm_conv, state_mlstm_C, state_mlstm_n,
              state_mlstm_m)
    bp = x_prompt.shape[0]
    zero_states = tuple(None if s is state_mlstm_C else jnp.zeros((s.shape[0], bp) + s.shape[2:], s.dtype)
                        for s in states)
    prompt, sample = _trunks([_Group(x_prompt, zero_states), _Group(x_sample, states)], *weights)
    return (prompt[0], sample[0]) + prompt[1:] + sample[1:]
```

```python
import functools

import jax
import jax.numpy as jnp
from jax import lax
from jax.experimental import pallas as pl
from jax.experimental.pallas import tpu as pltpu

EPS = 1e-6
CONV_W = 4
LRU_C = 8.0
GLA_TAU = 16.0
GLA_CHUNK = 64
GLA_SUB = 8
M_CHUNK = 256
N_GROUPS = 4
E_PER_GROUP = 4
N_EXPERTS = N_GROUPS * E_PER_GROUP

V7X_VMEM_BYTES = 64 * 1024 * 1024
VMEM_LIMIT_BYTES = V7X_VMEM_BYTES - 8 * 1024 * 1024
SUBLANES = 8
LANES = 128
MXU_WIDTH = 256

F32 = jnp.float32
BF16 = jnp.bfloat16


def _params(*semantics):
    return pltpu.CompilerParams(dimension_semantics=semantics, vmem_limit_bytes=VMEM_LIMIT_BYTES)


def _dot(a, b):
    return jnp.dot(a.astype(BF16), b.astype(BF16), preferred_element_type=F32)


def _dot_nt(a, b):
    return lax.dot_general(a.astype(BF16), b.astype(BF16), (((1,), (1,)), ((), ())), preferred_element_type=F32)


def _dot_tn(a, b):
    return lax.dot_general(a.astype(BF16), b.astype(BF16), (((0,), (0,)), ((), ())), preferred_element_type=F32)


def _dot_split(m01, y):
    y_hi = y.astype(BF16)
    y_lo = (y - y_hi.astype(F32)).astype(BF16)
    m = m01.astype(BF16)
    return jnp.dot(m, y_hi, preferred_element_type=F32) + jnp.dot(m, y_lo, preferred_element_type=F32)


def _softplus(x):
    return jnp.maximum(x, 0.0) + jnp.log1p(jnp.exp(-jnp.abs(x)))


def _log_sigmoid(x):
    return -_softplus(-x)


def _rmsnorm_rows(x, g):
    return x * lax.rsqrt(jnp.mean(x * x, axis=-1, keepdims=True) + EPS) * g


def _iota(shape, dim):
    return lax.broadcasted_iota(jnp.int32, shape, dim)


def _lane_column(x, lane_index):
    return jnp.sum(jnp.where(_iota(x.shape, 1) == lane_index, x, 0.0), axis=1, keepdims=True)


def _row_tile(n, target):
    t = min(n, target)
    assert n % t == 0
    return t


def _route(xn, wr_ref, br_ref):
    logits = _dot(xn, wr_ref[...]) + br_ref[...]
    lane = _iota(logits.shape, 1)
    big = jnp.int32(LANES)
    is_g = jnp.logical_and(lane >= N_EXPERTS, lane < N_EXPERTS + N_GROUPS)
    gl = jnp.where(is_g, logits, -jnp.inf)
    gmax = jnp.max(gl, axis=1, keepdims=True)
    gsum = jnp.sum(jnp.where(is_g, jnp.exp(gl - gmax), 0.0), axis=1, keepdims=True)
    p_g = 1.0 / gsum
    g_idx = jnp.min(jnp.where(gl == gmax, lane, big), axis=1, keepdims=True) - N_EXPERTS
    sel = jnp.logical_and(lane < N_EXPERTS, (lane >> 2) == g_idx)
    el = jnp.where(sel, logits, -jnp.inf)
    emax = jnp.max(el, axis=1, keepdims=True)
    eexp = jnp.where(sel, jnp.exp(el - emax), 0.0)
    ep = eexp / jnp.sum(eexp, axis=1, keepdims=True)
    cand = jnp.where(sel, ep, -1.0)
    v1 = jnp.max(cand, axis=1, keepdims=True)
    idx1 = jnp.min(jnp.where(cand == v1, lane, big), axis=1, keepdims=True)
    cand2 = jnp.where(lane == idx1, -1.0, cand)
    v2 = jnp.max(cand2, axis=1, keepdims=True)
    idx2 = jnp.min(jnp.where(cand2 == v2, lane, big), axis=1, keepdims=True)
    tot = v1 + v2
    return jnp.where(lane == idx1, v1 / tot * p_g, 0.0) + jnp.where(lane == idx2, v2 / tot * p_g, 0.0)


def _linear_kernel(*refs, n_lhs, has_norm, has_res, has_router):
    lhs_refs = refs[:n_lhs]
    pos = n_lhs
    g_ref = refs[pos] if has_norm else None
    pos += int(has_norm)
    w_ref = refs[pos]
    pos += 1
    res_ref = refs[pos] if has_res else None
    pos += int(has_res)
    if has_router:
        fg_ref, wr_ref, br_ref = refs[pos:pos + 3]
        pos += 3
    o_ref = refs[pos]
    pos += 1
    if has_router:
        xn_ref, gates_ref = refs[pos:pos + 2]
        pos += 2
    lhs_scr = refs[pos]

    @pl.when(pl.program_id(1) == 0)
    def _():
        off = 0
        for a_ref in lhs_refs:
            a = a_ref[...]
            if has_norm:
                a = _rmsnorm_rows(a, g_ref[...])
            lhs_scr[:, off:off + a.shape[1]] = a.astype(BF16)
            off += a.shape[1]

    acc = jnp.dot(lhs_scr[...], w_ref[...].astype(BF16), preferred_element_type=F32)
    if has_res:
        acc = res_ref[...] + acc
    o_ref[...] = acc
    if has_router:
        xn = _rmsnorm_rows(acc, fg_ref[...])
        xn_ref[...] = xn.astype(BF16)
        gates_ref[...] = _route(xn, wr_ref, br_ref)


def fused_linear(lhs_list, w, *, n_out, gain=None, res=None, router=None, tm=1024, tn=1024, name="linear"):
    n = lhs_list[0].shape[0]
    ks = [a.shape[1] for a in lhs_list]
    ktot = sum(ks)
    assert w.shape[0] == ktot
    tm = _row_tile(n, tm)
    tn = _row_tile(n_out, tn)
    resident = tn == n_out == w.shape[1]
    assert router is None or resident
    in_specs = [pl.BlockSpec((tm, k), lambda i, j: (i, 0)) for k in ks]
    args = list(lhs_list)
    if gain is not None:
        in_specs.append(pl.BlockSpec((1, ktot), lambda i, j: (0, 0)))
        args.append(gain.reshape(1, ktot))
    in_specs.append(pl.BlockSpec((ktot, tn), lambda i, j: (0, j), pipeline_mode=pl.Buffered(1) if resident else None))
    args.append(w)
    if res is not None:
        in_specs.append(pl.BlockSpec((tm, tn), lambda i, j: (i, j)))
        args.append(res)
    out_specs = [pl.BlockSpec((tm, tn), lambda i, j: (i, j))]
    out_shape = [jax.ShapeDtypeStruct((n, n_out), F32)]
    if router is not None:
        fgain, wr, br = router
        in_specs += [pl.BlockSpec((1, n_out), lambda i, j: (0, 0)), pl.BlockSpec((n_out, LANES), lambda i, j: (0, 0)),
                     pl.BlockSpec((1, LANES), lambda i, j: (0, 0))]
        args += [fgain.reshape(1, n_out), wr, br]
        out_specs += [pl.BlockSpec((tm, n_out), lambda i, j: (i, 0)), pl.BlockSpec((tm, LANES), lambda i, j: (i, 0))]
        out_shape += [jax.ShapeDtypeStruct((n, n_out), BF16), jax.ShapeDtypeStruct((n, LANES), F32)]
    kern = functools.partial(_linear_kernel, n_lhs=len(lhs_list), has_norm=gain is not None, has_res=res is not None,
                             has_router=router is not None)
    outs = pl.pallas_call(
        kern,
        grid=(n // tm, n_out // tn),
        in_specs=in_specs,
        out_specs=out_specs,
        out_shape=out_shape,
        scratch_shapes=[pltpu.VMEM((tm, ktot), BF16)],
        compiler_params=_params("parallel", "arbitrary"),
        name=name,
    )(*args)
    return outs if router is not None else outs[0]


def _gla_decay_kernel(x_ref, g_ref, walr_ref, wa2_ref, ba2_ref, o_ref):
    xn = _rmsnorm_rows(x_ref[...], g_ref[...])
    alr = _dot(xn, walr_ref[...])
    o_ref[...] = _log_sigmoid(_dot(alr, wa2_ref[...]) + ba2_ref[...]) * (1.0 / GLA_TAU)


def gla_decay(x, gain, w_alr_pad, wa2_pad, ba2, tm=512):
    n, d = x.shape
    hk = wa2_pad.shape[1]
    tm = _row_tile(n, tm)
    return pl.pallas_call(
        _gla_decay_kernel,
        grid=(n // tm,),
        in_specs=[
            pl.BlockSpec((tm, d), lambda i: (i, 0)),
            pl.BlockSpec((1, d), lambda i: (0, 0)),
            pl.BlockSpec((d, LANES), lambda i: (0, 0)),
            pl.BlockSpec((LANES, hk), lambda i: (0, 0)),
            pl.BlockSpec((1, hk), lambda i: (0, 0)),
        ],
        out_specs=pl.BlockSpec((tm, hk), lambda i: (i, 0)),
        out_shape=jax.ShapeDtypeStruct((n, hk), F32),
        compiler_params=_params("parallel"),
        name="gla_decay",
    )(x, gain.reshape(1, d), w_alr_pad, wa2_pad, ba2.reshape(1, hk))


def _causal_conv(x, prev, w_ref, b_ref, seq_rows):
    r = x.shape[0]
    row = _iota((r, 1), 0)
    per_seq = prev.shape[0] == r
    pos = row & (seq_rows - 1) if per_seq else row
    acc = b_ref[...] + x * w_ref[CONV_W - 1:CONV_W, :]
    for j in range(1, CONV_W):
        if per_seq:
            hist = pltpu.roll(prev, (j - SUBLANES) % r, 0)
        else:
            hist = jnp.concatenate([pltpu.roll(prev, j, 0), x[SUBLANES:]], axis=0)
        shifted = jnp.where(pos < j, hist, pltpu.roll(x, j, 0))
        acc = acc + shifted * w_ref[CONV_W - 1 - j:CONV_W - j, :]
    return acc


def _lru_kernel(xa_ref, ga_ref, prev_ref, h0_ref, cw_ref, cb_ref, wa_ref, ba_ref, wx_ref, bx_ref, lam_ref,
                ya_ref, hl_ref, prev_scr, h_scr, a_scr, b_scr, hs_scr, *, nb, tc):
    c = pl.program_id(1)

    @pl.when(c == 0)
    def _():
        prev_scr[...] = prev_ref[...]
        h_scr[...] = h0_ref[...]

    x = xa_ref[...]
    xc = _causal_conv(x, prev_scr[...], cw_ref, cb_ref, tc)
    if nb == 1:
        prev_scr[...] = x[x.shape[0] - SUBLANES:]
    nblk = x.shape[1] // MXU_WIDTH
    xcb = xc.astype(BF16)
    r_parts, i_parts = [], []
    for t in range(nblk):
        sl = slice(t * MXU_WIDTH, (t + 1) * MXU_WIDTH)
        r_parts.append(jnp.dot(xcb[:, sl], wa_ref[t], preferred_element_type=F32))
        i_parts.append(jnp.dot(xcb[:, sl], wx_ref[t], preferred_element_type=F32))
    r_gate = jax.nn.sigmoid(jnp.concatenate(r_parts, axis=1) + ba_ref[...])
    i_gate = jax.nn.sigmoid(jnp.concatenate(i_parts, axis=1) + bx_ref[...])
    log_a = (-LRU_C) * r_gate * _softplus(-lam_ref[...])
    a = jnp.exp(log_a)
    a_scr[...] = a
    b_scr[...] = jnp.sqrt(-jnp.tanh(log_a) * (a * a + 1.0)) * (i_gate * xc)

    def seq_body(s, carry):
        def tile_body(tl, h):
            base = pl.multiple_of(s * tc + tl * SUBLANES, SUBLANES)
            for i in range(SUBLANES):
                h = a_scr[pl.ds(base + i, 1), :] * h + b_scr[pl.ds(base + i, 1), :]
                hs_scr[pl.ds(base + i, 1), :] = h
            return h

        h_scr[s] = lax.fori_loop(0, tc // SUBLANES, tile_body, h_scr[s])
        return carry

    lax.fori_loop(0, nb, seq_body, 0, unroll=min(nb, 4))
    ya_ref[...] = hs_scr[...] * jax.nn.gelu(ga_ref[...])

    @pl.when(c == pl.num_programs(1) - 1)
    def _():
        hl_ref[...] = h_scr[...]


def lru_branch(proj, prev8, h0, conv_w, conv_b, wa_t, ba, wx_t, bx, lam, *, nseq, t, nb, tc):
    n = proj.shape[0]
    w = h0.shape[1]
    nchunk = t // tc
    assert nseq % nb == 0 and t % tc == 0 and (nb == 1 or tc == t == SUBLANES)
    r = nb * tc
    nblk = w // MXU_WIDTH
    rows = lambda s, c: s * nchunk + c
    full2 = lambda s, c: (0, 0)
    full3 = lambda s, c: (0, 0, 0)
    kern = functools.partial(_lru_kernel, nb=nb, tc=tc)
    ya, hl = pl.pallas_call(
        kern,
        grid=(nseq // nb, nchunk),
        in_specs=[
            pl.BlockSpec((r, w), lambda s, c: (rows(s, c), 0)),
            pl.BlockSpec((r, w), lambda s, c: (rows(s, c), 1)),
            pl.BlockSpec((nb * SUBLANES, w), lambda s, c: (s, 0)),
            pl.BlockSpec((nb, 1, w), lambda s, c: (s, 0, 0)),
            pl.BlockSpec((CONV_W, w), full2),
            pl.BlockSpec((1, w), full2),
            pl.BlockSpec((nblk, MXU_WIDTH, MXU_WIDTH), full3),
            pl.BlockSpec((1, w), full2),
            pl.BlockSpec((nblk, MXU_WIDTH, MXU_WIDTH), full3),
            pl.BlockSpec((1, w), full2),
            pl.BlockSpec((1, w), full2),
        ],
        out_specs=[
            pl.BlockSpec((r, w), lambda s, c: (rows(s, c), 0)),
            pl.BlockSpec((nb, 1, w), lambda s, c: (s, 0, 0)),
        ],
        out_shape=[jax.ShapeDtypeStruct((n, w), F32), jax.ShapeDtypeStruct((nseq, 1, w), F32)],
        scratch_shapes=[
            pltpu.VMEM((nb * SUBLANES, w), F32),
            pltpu.VMEM((nb, 1, w), F32),
            pltpu.VMEM((r, w), F32),
            pltpu.VMEM((r, w), F32),
            pltpu.VMEM((r, w), F32),
        ],
        compiler_params=_params("parallel", "arbitrary"),
        name="lru_branch",
    )(proj, proj, prev8, h0.reshape(nseq, 1, w), conv_w, conv_b.reshape(1, w), wa_t, ba.reshape(1, w), wx_t,
      bx.reshape(1, w), lam.reshape(1, w))
    return ya, hl.reshape(nseq, w)


def _gla_cumdecay(lg, run_rows):
    r = lg.shape[0]
    row = _iota((r, r), 0)
    col = _iota((r, r), 1)
    shift = run_rows.bit_length() - 1
    tri = jnp.logical_and(row >= col, (row >> shift) == (col >> shift))
    return _dot_split(tri.astype(F32), lg)


def _gla_near_att(qs, k, bc, sub):
    r = qs.shape[0]
    row = _iota((r, r), 0)
    col = _iota((r, r), 1)
    posr = _iota((r, 1), 0) & (sub - 1)
    att = jnp.zeros((r, r), F32)
    for d in range(sub):
        kd = k if d == 0 else pltpu.roll(k, d, 0)
        bcd = bc if d == 0 else pltpu.roll(bc, d, 0)
        valid = posr >= d
        prod = qs * kd * jnp.exp(jnp.where(valid, bc - bcd, 0.0))
        diag = jnp.sum(jnp.where(valid, prod, 0.0), axis=1, keepdims=True)
        att = att + jnp.where(col == row - d, diag, 0.0)
    return att


def _gla_far_att(qs, k, bc, sub):
    l, dk = qs.shape
    att = jnp.zeros((l, l), F32)
    for j in range(l // sub - 1):
        lo, hi = j * sub, (j + 1) * sub
        e_j = bc[hi - 1:hi, :]
        kp = k[lo:hi] * jnp.exp(e_j - bc[lo:hi])
        qp = qs[hi:] * jnp.exp(bc[hi:] - e_j)
        k_rows = [jnp.zeros((lo, dk), F32)] * (lo > 0) + [kp, jnp.zeros((l - hi, dk), F32)]
        att = att + _dot_nt(jnp.concatenate([jnp.zeros((hi, dk), F32), qp], axis=0), jnp.concatenate(k_rows, axis=0))
    return att


def _as_column(row_vec):
    d = row_vec.shape[1]
    eye = _iota((d, d), 0) == _iota((d, d), 1)
    return jnp.sum(jnp.where(eye, row_vec, 0.0), axis=1, keepdims=True)


def _gla_kernel(q_ref, k_ref, v_ref, g_ref, lg_ref, s0_ref, gn_ref, yb_ref, sn_ref, *scratch, nb, chunk, sub):
    c = pl.program_id(2)
    scale = q_ref.shape[1] ** -0.5

    def finish(o, g):
        on = o * lax.rsqrt(jnp.mean(o * o, axis=-1, keepdims=True) + EPS) * gn_ref[...]
        return on * jax.nn.silu(g)

    if nb == 1:
        (s_scr,) = scratch

        @pl.when(c == 0)
        def _():
            s_scr[...] = s0_ref[0, 0]

        def body(i, carry):
            rows = pl.ds(pl.multiple_of(i * chunk, chunk), chunk)
            qs = q_ref[rows, :] * scale
            k = k_ref[rows, :]
            v = v_ref[rows, :]
            bc = _gla_cumdecay(lg_ref[rows, :], chunk)
            s_state = s_scr[...]
            o = _dot(qs * jnp.exp(bc), s_state) + _dot(_gla_near_att(qs, k, bc, sub) + _gla_far_att(qs, k, bc, sub), v)
            bl = bc[chunk - 1:chunk, :]
            s_scr[...] = _as_column(jnp.exp(bl)) * s_state + _dot_tn(k * jnp.exp(bl - bc), v)
            yb_ref[rows, :] = finish(o, g_ref[rows, :])
            return carry

        lax.fori_loop(0, q_ref.shape[0] // chunk, body, 0, unroll=2)

        @pl.when(c == pl.num_programs(2) - 1)
        def _():
            sn_ref[0, 0] = s_scr[...]
    else:
        qe_scr, kd_scr, eb_scr, o_scr = scratch
        r = q_ref.shape[0]
        qs = q_ref[...] * scale
        k = k_ref[...]
        bc = _gla_cumdecay(lg_ref[...], chunk)
        o_scr[...] = _dot(_gla_near_att(qs, k, bc, sub), v_ref[...])
        qe_scr[...] = qs * jnp.exp(bc)
        row = _iota((r, r), 0)
        col = _iota((r, r), 1)
        last = (col == (row | (chunk - 1))).astype(F32)
        bl = _dot_split(last, bc)
        kd_scr[...] = k * jnp.exp(bl - bc)
        eb_scr[...] = jnp.exp(bl)

        def body(j, carry):
            rows = pl.ds(pl.multiple_of(j * chunk, chunk), chunk)
            s_state = s0_ref[j, 0]
            o_scr[rows, :] += _dot(qe_scr[rows, :], s_state)
            decay = _as_column(eb_scr[pl.ds(pl.multiple_of(j * chunk, chunk), 1), :])
            sn_ref[j, 0] = decay * s_state + _dot_tn(kd_scr[rows, :], v_ref[rows, :])
            return carry

        lax.fori_loop(0, nb, body, 0, unroll=min(nb, 4))
        yb_ref[...] = finish(o_scr[...], g_ref[...])


def gla_branch(proj, lg, s0, gnorm, *, nseq, t, nb, rows_blk, chunk, sub, q_off, k_off, v_off, g_off):
    n = proj.shape[0]
    _, nh, dk, dv = s0.shape
    ntb = t // rows_blk if nb == 1 else 1
    r = rows_blk if nb == 1 else nb * t
    assert (nb == 1 and t % rows_blk == 0 and rows_blk % chunk == 0) or (chunk == sub == t and nseq % nb == 0)
    rows = lambda s, h, c: s * ntb + c
    kern = functools.partial(_gla_kernel, nb=nb, chunk=chunk, sub=sub)
    if nb == 1:
        scratch = [pltpu.VMEM((dk, dv), F32)]
    else:
        scratch = [pltpu.VMEM((r, dk), F32), pltpu.VMEM((r, dk), F32), pltpu.VMEM((r, dk), F32), pltpu.VMEM((r, dv), F32)]
    yb, sn = pl.pallas_call(
        kern,
        grid=(nseq // nb, nh, ntb),
        in_specs=[
            pl.BlockSpec((r, dk), lambda s, h, c: (rows(s, h, c), q_off // dk + h)),
            pl.BlockSpec((r, dk), lambda s, h, c: (rows(s, h, c), k_off // dk + h)),
            pl.BlockSpec((r, dv), lambda s, h, c: (rows(s, h, c), v_off // dv + h)),
            pl.BlockSpec((r, dv), lambda s, h, c: (rows(s, h, c), g_off // dv + h)),
            pl.BlockSpec((r, dk), lambda s, h, c: (rows(s, h, c), h)),
            pl.BlockSpec((nb, 1, dk, dv), lambda s, h, c: (s, h, 0, 0)),
            pl.BlockSpec((1, dv), lambda s, h, c: (0, h)),
        ],
        out_specs=[
            pl.BlockSpec((r, dv), lambda s, h, c: (rows(s, h, c), h)),
            pl.BlockSpec((nb, 1, dk, dv), lambda s, h, c: (s, h, 0, 0)),
        ],
        out_shape=[jax.ShapeDtypeStruct((n, nh * dv), F32), jax.ShapeDtypeStruct(s0.shape, F32)],
        scratch_shapes=scratch,
        compiler_params=_params("parallel", "parallel", "arbitrary"),
        name="gla_branch",
    )(proj, proj, proj, proj, lg, s0, gnorm.reshape(1, nh * dv))
    return yb, sn


def _mlstm_pre_kernel(xm_ref, prev_ref, cw_ref, cb_ref, wq_ref, wk_ref, wv_ref, wg_ref, bg_ref,
                      q_ref, k_ref, v_ref, xc_ref, gate_ref, prev_scr, *, nb, tc, n_heads):
    c = pl.program_id(1)

    @pl.when(c == 0)
    def _():
        prev_scr[...] = prev_ref[...]

    x = xm_ref[...]
    xc = jax.nn.silu(_causal_conv(x, prev_scr[...], cw_ref, cb_ref, tc))
    if nb == 1:
        prev_scr[...] = x[x.shape[0] - SUBLANES:]
    xc_ref[...] = xc
    di = x.shape[1]
    xcb = xc.astype(BF16)
    xb = x.astype(BF16)
    for t in range(di // MXU_WIDTH):
        sl = slice(t * MXU_WIDTH, (t + 1) * MXU_WIDTH)
        q_ref[:, sl] = jnp.dot(xcb[:, sl], wq_ref[t], preferred_element_type=F32)
        k_ref[:, sl] = jnp.dot(xcb[:, sl], wk_ref[t], preferred_element_type=F32)
        v_ref[:, sl] = jnp.dot(xb[:, sl], wv_ref[t], preferred_element_type=F32)
    pre = (_dot(q_ref[...], wg_ref[0:di, :]) + _dot(k_ref[...], wg_ref[di:2 * di, :])
           + _dot(v_ref[...], wg_ref[2 * di:3 * di, :]) + bg_ref[...])
    gate_ref[...] = jnp.where(_iota(pre.shape, 1) < n_heads, pre, _log_sigmoid(pre))


def mlstm_pre(up, prev8, conv_w, conv_b, wq_t, wk_t, wv_t, wg, bg, *, nseq, t, nb, tc, n_heads):
    n = up.shape[0]
    di = conv_w.shape[1]
    nchunk = t // tc
    assert nseq % nb == 0 and t % tc == 0 and (nb == 1 or tc == t == SUBLANES)
    r = nb * tc
    nblk = di // MXU_WIDTH
    rows = lambda s, c: (s * nchunk + c, 0)
    full2 = lambda s, c: (0, 0)
    full3 = lambda s, c: (0, 0, 0)
    wide = jax.ShapeDtypeStruct((n, di), F32)
    kern = functools.partial(_mlstm_pre_kernel, nb=nb, tc=tc, n_heads=n_heads)
    return pl.pallas_call(
        kern,
        grid=(nseq // nb, nchunk),
        in_specs=[
            pl.BlockSpec((r, di), rows),
            pl.BlockSpec((nb * SUBLANES, di), lambda s, c: (s, 0)),
            pl.BlockSpec((CONV_W, di), full2),
            pl.BlockSpec((1, di), full2),
            pl.BlockSpec((nblk, MXU_WIDTH, MXU_WIDTH), full3),
            pl.BlockSpec((nblk, MXU_WIDTH, MXU_WIDTH), full3),
            pl.BlockSpec((nblk, MXU_WIDTH, MXU_WIDTH), full3),
            pl.BlockSpec((3 * di, LANES), full2),
            pl.BlockSpec((1, LANES), full2),
        ],
        out_specs=[pl.BlockSpec((r, di), rows)] * 4 + [pl.BlockSpec((r, LANES), rows)],
        out_shape=[wide, wide, wide, wide, jax.ShapeDtypeStruct((n, LANES), F32)],
        scratch_shapes=[pltpu.VMEM((nb * SUBLANES, di), F32)],
        compiler_params=_params("parallel", "arbitrary"),
        name="mlstm_pre",
    )(up, prev8, conv_w, conv_b.reshape(1, di), wq_t, wk_t, wv_t, wg, bg)


def _mlstm_chunk(q, k, v, igc, lfc, xc, z, skip, ng, m_prev, n_prev, c_load, c_store):
    l, dh = q.shape
    row = _iota((l, l), 0)
    col = _iota((l, l), 1)
    tril = row >= col
    f_b = jnp.broadcast_to(lfc, (l, l))
    i_b = jnp.broadcast_to(igc, (l, l))
    f_col = _dot_split(tril.astype(F32), f_b)
    row_term = _dot_split(jnp.ones((l, l), F32),
                          jnp.where(row == col, i_b, 0.0) - jnp.where(row <= col, f_b, 0.0))
    dm = jnp.where(tril, f_col + row_term, -jnp.inf)
    fcum = f_col[:, 0:1]
    prev = m_prev + fcum
    mt = jnp.maximum(prev, jnp.max(dm, axis=1, keepdims=True))
    wprev = jnp.exp(prev - mt)
    smat = _dot_nt(q, k) * jnp.exp(dm - mt)
    den = wprev * jnp.sum(q * n_prev, axis=1, keepdims=True) + jnp.sum(smat, axis=1, keepdims=True)
    inv = 1.0 / jnp.maximum(jnp.abs(den), jnp.exp(-mt))
    f_last = fcum[l - 1:l, :]
    m_last = mt[l - 1:l, :]
    w_c = jnp.exp(m_prev + f_last - m_last)
    kw = k * jnp.exp(f_last - fcum + igc - m_last)
    qb = q.astype(BF16)
    sb = smat.astype(BF16)
    kwb = kw.astype(BF16)
    parts = []
    for t in range(dh // MXU_WIDTH):
        cols = slice(t * MXU_WIDTH, (t + 1) * MXU_WIDTH)
        c_blk = c_load(cols)
        vb = v[:, cols].astype(BF16)
        num = (wprev * jnp.dot(qb, c_blk.astype(BF16), preferred_element_type=F32)
               + jnp.dot(sb, vb, preferred_element_type=F32))
        parts.append(num * inv)
        c_store(cols, w_c * c_blk + _dot_tn(kwb, vb))
    n_new = w_c * n_prev + jnp.sum(kw, axis=0, keepdims=True)
    hh = jnp.concatenate(parts, axis=1)
    hc = hh - jnp.mean(hh, axis=-1, keepdims=True)
    hn = hc * lax.rsqrt(jnp.mean(hc * hc, axis=-1, keepdims=True) + EPS) * ng
    return (hn + skip * xc) * jax.nn.silu(z), n_new, m_last


def _mlstm_seq_kernel(*refs, chunk, n_heads, zero_state):
    q_ref, k_ref, v_ref, gate_ref, xc_ref, z_ref = refs[:6]
    pos = 6
    if not zero_state:
        c0_ref, n0_ref, m0_ref = refs[pos:pos + 3]
        pos += 3
    skip_ref, ng_ref, out_ref, cn_ref, nn_ref, mn_ref, c_scr, n_scr, m_scr = refs[pos:pos + 9]
    head = pl.program_id(1)
    c = pl.program_id(2)

    @pl.when(c == 0)
    def _():
        if zero_state:
            c_scr[...] = jnp.zeros_like(c_scr)
            n_scr[...] = jnp.zeros_like(n_scr)
            m_scr[...] = jnp.zeros_like(m_scr)
        else:
            c_scr[...] = c0_ref[0, 0]
            n_scr[...] = n0_ref[0]
            m_scr[...] = m0_ref[0]

    kscale = q_ref.shape[1] ** -0.5

    def c_store(cols, value):
        c_scr[:, cols] = value

    def body(i, carry):
        rows = pl.ds(pl.multiple_of(i * chunk, chunk), chunk)
        gates = gate_ref[rows, :]
        out, n_new, m_last = _mlstm_chunk(
            q_ref[rows, :], k_ref[rows, :] * kscale, v_ref[rows, :], _lane_column(gates, head),
            _lane_column(gates, head + n_heads), xc_ref[rows, :], z_ref[rows, :], skip_ref[...], ng_ref[...],
            m_scr[:, 0:1], n_scr[...], lambda cols: c_scr[:, cols], c_store)
        out_ref[rows, :] = out
        n_scr[...] = n_new
        m_scr[...] = jnp.broadcast_to(m_last, m_scr.shape)
        return carry

    lax.fori_loop(0, q_ref.shape[0] // chunk, body, 0)

    @pl.when(c == pl.num_programs(2) - 1)
    def _():
        cn_ref[0, 0] = c_scr[...]
        nn_ref[0] = n_scr[...]
        mn_ref[0] = m_scr[...]


def _mlstm_step_kernel(q_ref, k_ref, v_ref, gate_ref, xc_ref, z_ref, c0_ref, n0_ref, m0_ref, skip_ref, ng_ref,
                       out_ref, cn_ref, nn_ref, mn_ref, *, hb, n_heads, head_block=None):
    hblk = pl.program_id(1) if head_block is None else head_block
    dh = c0_ref.shape[2]
    kscale = dh ** -0.5
    gates = gate_ref[...]
    for hh in range(hb):
        cols_h = slice(hh * dh, (hh + 1) * dh)
        head = hblk * hb + hh

        def c_store(cols, value, hh=hh):
            cn_ref[0, hh, :, cols] = value

        out, n_new, m_last = _mlstm_chunk(
            q_ref[:, cols_h], k_ref[:, cols_h] * kscale, v_ref[:, cols_h], _lane_column(gates, head),
            _lane_column(gates, head + n_heads), xc_ref[:, cols_h], z_ref[:, cols_h], skip_ref[:, cols_h],
            ng_ref[:, cols_h], m0_ref[hh][:, 0:1], n0_ref[hh], lambda cols, hh=hh: c0_ref[0, hh, :, cols], c_store)
        out_ref[:, cols_h] = out
        nn_ref[hh] = n_new
        mn_ref[hh] = jnp.broadcast_to(m_last, (1, LANES))


def mlstm_recurrence(q, k, v, gates, xc, up, c0, n0, m0, skip, norm_g, *, nseq, t, nh, rows_blk, chunk, hb):
    n, di = q.shape
    dh = di // nh
    zero_state = c0 is None
    out_shape = [
        jax.ShapeDtypeStruct((n, di), F32),
        jax.ShapeDtypeStruct((nseq, nh, dh, dh), F32),
        jax.ShapeDtypeStruct((nseq * nh, 1, dh), F32),
        jax.ShapeDtypeStruct((nseq * nh, 1, LANES), F32),
    ]
    state_args = []
    if not zero_state:
        state_args = [c0, n0.reshape(nseq * nh, 1, dh),
                      jnp.broadcast_to(m0.reshape(nseq * nh, 1, 1), (nseq * nh, 1, LANES))]
    if t == chunk and not zero_state:
        assert nh % hb == 0
        nhb = nh // hb
        wide = lambda s, h: (s, h)
        per_head = lambda s, h: (s * nhb + h, 0, 0)
        mat = lambda s, h: (s, h, 0, 0)
        out, cn, nn, mn = pl.pallas_call(
            functools.partial(_mlstm_step_kernel, hb=hb, n_heads=nh),
            grid=(nseq, nhb),
            in_specs=[pl.BlockSpec((t, hb * dh), wide)] * 3
            + [pl.BlockSpec((t, LANES), lambda s, h: (s, 0)), pl.BlockSpec((t, hb * dh), wide),
               pl.BlockSpec((t, hb * dh), lambda s, h: (s, nhb + h)), pl.BlockSpec((1, hb, dh, dh), mat),
               pl.BlockSpec((hb, 1, dh), per_head), pl.BlockSpec((hb, 1, LANES), per_head),
               pl.BlockSpec((1, hb * dh), lambda s, h: (0, h)), pl.BlockSpec((1, hb * dh), lambda s, h: (0, h))],
            out_specs=[pl.BlockSpec((t, hb * dh), wide), pl.BlockSpec((1, hb, dh, dh), mat),
                       pl.BlockSpec((hb, 1, dh), per_head), pl.BlockSpec((hb, 1, LANES), per_head)],
            out_shape=out_shape,
            compiler_params=_params("parallel", "arbitrary"),
            name="mlstm_step",
        )(q, k, v, gates, xc, up, *state_args, skip.reshape(1, di), norm_g.reshape(1, di))
    else:
        ntb = t // rows_blk
        assert t % rows_blk == 0 and rows_blk % chunk == 0
        rows = lambda s, h, c: (s * ntb + c, h)
        per_head = lambda s, h, c: (s * nh + h, 0, 0)
        mat = lambda s, h, c: (s, h, 0, 0)
        state_specs = [] if zero_state else [pl.BlockSpec((1, 1, dh, dh), mat), pl.BlockSpec((1, 1, dh), per_head),
                                             pl.BlockSpec((1, 1, LANES), per_head)]
        out, cn, nn, mn = pl.pallas_call(
            functools.partial(_mlstm_seq_kernel, chunk=chunk, n_heads=nh, zero_state=zero_state),
            grid=(nseq, nh, ntb),
            in_specs=[pl.BlockSpec((rows_blk, dh), rows)] * 3
            + [pl.BlockSpec((rows_blk, LANES), lambda s, h, c: (s * ntb + c, 0)), pl.BlockSpec((rows_blk, dh), rows),
               pl.BlockSpec((rows_blk, dh), lambda s, h, c: (s * ntb + c, nh + h))]
            + state_specs
            + [pl.BlockSpec((1, dh), lambda s, h, c: (0, h)), pl.BlockSpec((1, dh), lambda s, h, c: (0, h))],
            out_specs=[pl.BlockSpec((rows_blk, dh), rows), pl.BlockSpec((1, 1, dh, dh), mat),
                       pl.BlockSpec((1, 1, dh), per_head), pl.BlockSpec((1, 1, LANES), per_head)],
            out_shape=out_shape,
            scratch_shapes=[pltpu.VMEM((dh, dh), F32), pltpu.VMEM((1, dh), F32), pltpu.VMEM((1, LANES), F32)],
            compiler_params=_params("parallel", "parallel", "arbitrary"),
            name="mlstm_seq",
        )(q, k, v, gates, xc, up, *state_args, skip.reshape(1, di), norm_g.reshape(1, di))
    return out, cn, nn.reshape(nseq, nh, dh), mn[:, 0, 0].reshape(nseq, nh)


def _mlstm_dual_kernel(ql_ref, kl_ref, vl_ref, gl_ref, xcl_ref, zl_ref, skipl_ref, ngl_ref,
                       qs_ref, ks_ref, vs_ref, gs_ref, xcs_ref, zs_ref, c0_ref, n0_ref, m0_ref, skips_ref, ngs_ref,
                       outl_ref, cnl_ref, nnl_ref, mnl_ref, outs_ref, cns_ref, nns_ref, mns_ref, n_scr, m_scr,
                       *, hb, n_heads, chunks_per_seq):
    g = pl.program_id(0)
    c = g % chunks_per_seq
    head_l = (g // chunks_per_seq) % n_heads
    dh = ql_ref.shape[1]
    kscale = dh ** -0.5

    @pl.when(c == 0)
    def _():
        cnl_ref[...] = jnp.zeros_like(cnl_ref)
        n_scr[...] = jnp.zeros_like(n_scr)
        m_scr[...] = jnp.zeros_like(m_scr)

    def cl_store(cols, value):
        cnl_ref[0, 0, :, cols] = value

    gl = gl_ref[...]
    out, n_new, m_last = _mlstm_chunk(
        ql_ref[...], kl_ref[...] * kscale, vl_ref[...], _lane_column(gl, head_l), _lane_column(gl, head_l + n_heads),
        xcl_ref[...], zl_ref[...], skipl_ref[...], ngl_ref[...], m_scr[:, 0:1], n_scr[...],
        lambda cols: cnl_ref[0, 0, :, cols], cl_store)
    outl_ref[...] = out
    n_scr[...] = n_new
    m_scr[...] = jnp.broadcast_to(m_last, m_scr.shape)

    @pl.when(c == chunks_per_seq - 1)
    def _():
        nnl_ref[0] = n_scr[...]
        mnl_ref[0] = m_scr[...]

    _mlstm_step_kernel(qs_ref, ks_ref, vs_ref, gs_ref, xcs_ref, zs_ref, c0_ref, n0_ref, m0_ref, skips_ref, ngs_ref,
                       outs_ref, cns_ref, nns_ref, mns_ref, hb=hb, n_heads=n_heads, head_block=g % (n_heads // hb))


def dual_chunk_rows(nseq_l, t_l, nseq_s, nh, hb):
    steps = nseq_s * (nh // hb)
    total = nseq_l * t_l * nh
    if nh % hb or total % steps:
        return None
    chunk = total // steps
    ok = chunk % SUBLANES == 0 and t_l % chunk == 0 and 64 <= chunk <= M_CHUNK
    return chunk if ok else None


def mlstm_recurrence_dual(long_in, short_in, c0, n0, m0, skip, norm_g, *, nseq_l, t_l, nseq_s, t_s, nh, hb, chunk):
    di = long_in[0].shape[1]
    dh = di // nh
    nhb = nh // hb
    cps = t_l // chunk
    steps = nseq_s * nhb
    assert steps == nseq_l * nh * cps

    def rows_l(g):
        return (g // (nh * cps)) * cps + g % cps

    head_l = lambda g: (g // cps) % nh
    tile_l = lambda g: (rows_l(g), head_l(g))
    per_head_l = lambda g: (g // cps, 0, 0)
    tile_s = lambda g: (g // nhb, g % nhb)
    per_head_s = lambda g: (g, 0, 0)
    mat_s = lambda g: (g // nhb, g % nhb, 0, 0)
    n_l, n_s = long_in[0].shape[0], short_in[0].shape[0]
    in_specs = (
        [pl.BlockSpec((chunk, dh), tile_l)] * 3
        + [pl.BlockSpec((chunk, LANES), lambda g: (rows_l(g), 0)), pl.BlockSpec((chunk, dh), tile_l),
           pl.BlockSpec((chunk, dh), lambda g: (rows_l(g), nh + head_l(g))),
           pl.BlockSpec((1, dh), lambda g: (0, head_l(g))), pl.BlockSpec((1, dh), lambda g: (0, head_l(g)))]
        + [pl.BlockSpec((t_s, hb * dh), tile_s)] * 3
        + [pl.BlockSpec((t_s, LANES), lambda g: (g // nhb, 0)), pl.BlockSpec((t_s, hb * dh), tile_s),
           pl.BlockSpec((t_s, hb * dh), lambda g: (g // nhb, nhb + g % nhb)), pl.BlockSpec((1, hb, dh, dh), mat_s),
           pl.BlockSpec((hb, 1, dh), per_head_s), pl.BlockSpec((hb, 1, LANES), per_head_s),
           pl.BlockSpec((1, hb * dh), lambda g: (0, g % nhb)), pl.BlockSpec((1, hb * dh), lambda g: (0, g % nhb))])
    out_specs = [
        pl.BlockSpec((chunk, dh), tile_l), pl.BlockSpec((1, 1, dh, dh), lambda g: (g // (nh * cps), head_l(g), 0, 0)),
        pl.BlockSpec((1, 1, dh), per_head_l), pl.BlockSpec((1, 1, LANES), per_head_l),
        pl.BlockSpec((t_s, hb * dh), tile_s), pl.BlockSpec((1, hb, dh, dh), mat_s),
        pl.BlockSpec((hb, 1, dh), per_head_s), pl.BlockSpec((hb, 1, LANES), per_head_s)]
    out_shape = [
        jax.ShapeDtypeStruct((n_l, di), F32), jax.ShapeDtypeStruct((nseq_l, nh, dh, dh), F32),
        jax.ShapeDtypeStruct((nseq_l * nh, 1, dh), F32), jax.ShapeDtypeStruct((nseq_l * nh, 1, LANES), F32),
        jax.ShapeDtypeStruct((n_s, di), F32), jax.ShapeDtypeStruct((nseq_s, nh, dh, dh), F32),
        jax.ShapeDtypeStruct((nseq_s * nh, 1, dh), F32), jax.ShapeDtypeStruct((nseq_s * nh, 1, LANES), F32)]
    skip2, ng2 = skip.reshape(1, di), norm_g.reshape(1, di)
    res = pl.pallas_call(
        functools.partial(_mlstm_dual_kernel, hb=hb, n_heads=nh, chunks_per_seq=cps),
        grid=(steps,),
        in_specs=in_specs,
        out_specs=out_specs,
        out_shape=out_shape,
        scratch_shapes=[pltpu.VMEM((1, dh), F32), pltpu.VMEM((1, LANES), F32)],
        compiler_params=_params("arbitrary"),
        name="mlstm_dual",
    )(*long_in, skip2, ng2, *short_in, c0, n0.reshape(nseq_s * nh, 1, dh),
      jnp.broadcast_to(m0.reshape(nseq_s * nh, 1, 1), (nseq_s * nh, 1, LANES)), skip2, ng2)
    unpack = lambda o, cn, nn, mn, nseq: (o, cn, nn.reshape(nseq, nh, dh), mn[:, 0, 0].reshape(nseq, nh))
    return unpack(*res[:4], nseq_l), unpack(*res[4:], nseq_s)


def _moe_expert_kernel(*refs, has_final, eb):
    xn_ref, gates_ref, wg_ref, wu_ref, wd_ref, x_ref = refs[:6]
    fg_ref = refs[6] if has_final else None
    o_ref = refs[6 + int(has_final)]
    e = pl.program_id(1)

    @pl.when(e == 0)
    def _():
        o_ref[...] = x_ref[...]

    xn = xn_ref[...]
    gates = gates_ref[...]
    update = None
    for k in range(eb):
        hg = jnp.dot(xn, wg_ref[0, k].astype(BF16), preferred_element_type=F32)
        hu = jnp.dot(xn, wu_ref[0, k].astype(BF16), preferred_element_type=F32)
        h = jax.nn.silu(hg) * hu * _lane_column(gates, e * eb + k)
        part = _dot(h, wd_ref[0, k])
        update = part if update is None else update + part
    o_ref[...] += update

    if has_final:
        @pl.when(e == pl.num_programs(1) - 1)
        def _():
            o_ref[...] = _rmsnorm_rows(o_ref[...], fg_ref[...])


def moe_experts(xn, gates, w_gate, w_up, w_down, layer, x, final_gain=None, tm=1024, eb=2):
    n, d = x.shape
    _, ne, _, f = w_gate.shape
    tm = _row_tile(n, tm)
    assert ne % eb == 0
    in_specs = [
        pl.BlockSpec((tm, d), lambda i, e: (i, 0)),
        pl.BlockSpec((tm, LANES), lambda i, e: (i, 0)),
        pl.BlockSpec((1, eb, d, f), lambda i, e: (layer, e, 0, 0)),
        pl.BlockSpec((1, eb, d, f), lambda i, e: (layer, e, 0, 0)),
        pl.BlockSpec((1, eb, f, d), lambda i, e: (layer, e, 0, 0)),
        pl.BlockSpec((tm, d), lambda i, e: (i, 0), pipeline_mode=pl.Buffered(1)),
    ]
    args = [xn, gates, w_gate, w_up, w_down, x]
    if final_gain is not None:
        in_specs.append(pl.BlockSpec((1, d), lambda i, e: (0, 0)))
        args.append(final_gain.reshape(1, d))
    return pl.pallas_call(
        functools.partial(_moe_expert_kernel, has_final=final_gain is not None, eb=eb),
        grid=(n // tm, ne // eb),
        in_specs=in_specs,
        out_specs=pl.BlockSpec((tm, d), lambda i, e: (i, 0)),
        out_shape=jax.ShapeDtypeStruct((n, d), F32),
        compiler_params=_params("parallel", "arbitrary"),
        name="moe_experts",
    )(*args)


def _blockdiag_tiles(w):
    nblocks, bi, bo = w.shape
    per = MXU_WIDTH // bi
    ntiles = nblocks // per
    rows_of_tile = w.reshape(ntiles, MXU_WIDTH, bo)
    spread = jnp.broadcast_to(rows_of_tile[:, :, None, :], (ntiles, MXU_WIDTH, per, bo)).reshape(
        ntiles, MXU_WIDTH, MXU_WIDTH)
    on_diag = (_iota((MXU_WIDTH, MXU_WIDTH), 0) // bi) == (_iota((MXU_WIDTH, MXU_WIDTH), 1) // bo)
    return jnp.where(on_diag, spread, 0.0).astype(BF16)


def _pad_cols(w, width=LANES):
    return jnp.pad(w, ((0, 0), (0, width - w.shape[1])))


def _history_tiles(buf):
    nseq, hist, ch = buf.shape
    return jnp.pad(buf, ((0, 0), (SUBLANES - hist, 0), (0, 0))).reshape(nseq * SUBLANES, ch)


def _block_plan(nseq, t):
    long_seq = t > SUBLANES
    return dict(
        lru=dict(nb=1 if long_seq else min(nseq, 64), tc=min(t, 256)),
        gla=dict(nb=1 if long_seq else min(nseq, 16), rows_blk=min(t, 512), chunk=min(t, GLA_CHUNK), sub=min(t, GLA_SUB)),
        pre=dict(nb=1 if long_seq else min(nseq, 16), tc=min(t, 128)),
        rec=dict(rows_blk=min(t, M_CHUNK), chunk=min(t, M_CHUNK), hb=2),
    )


_STATE_KEYS = ("lru_conv", "lru_h", "gla_s", "m_conv", "m_c", "m_n", "m_m")


class _Group:
    def __init__(self, x3, states):
        self.nseq, self.t, self.d = x3.shape
        self.x = x3.reshape(self.nseq * self.t, self.d)
        self.st = dict(zip(_STATE_KEYS, states))
        self.plan = _block_plan(self.nseq, self.t)
        self.outs = {k: [] for k in _STATE_KEYS}
        self.xn = self.gates = None

    def result(self):
        return (self.x.reshape(self.nseq, self.t, self.d),) + tuple(jnp.stack(self.outs[k]) for k in _STATE_KEYS)


def _trunks(groups,
            norm_mix_g, norm_ffn_g, norm_final_g,
            l0_w_in, l0_lru_conv_w, l0_lru_conv_b, l0_lru_wa, l0_lru_ba, l0_lru_wx, l0_lru_bx, l0_lru_lam,
            l0_gla_wa2, l0_gla_ba2, l0_gla_norm_g, l0_w_out,
            l1_w_up, l1_conv_w, l1_conv_b, l1_wq, l1_wk, l1_wv, l1_w_ig, l1_b_ig, l1_w_fg, l1_b_fg, l1_skip,
            l1_norm_g, l1_w_down,
            moe_w_rg, moe_b_rg, moe_w_re, moe_b_re, moe_w_gate, moe_w_up, moe_w_down):
    depth = norm_mix_g.shape[0]
    hist = CONV_W - 1
    for layer in range(depth):
        j = layer // 2
        wr = _pad_cols(jnp.concatenate([moe_w_re[layer], moe_w_rg[layer]], axis=1)).astype(BF16)
        br = _pad_cols(jnp.concatenate([moe_b_re[layer], moe_b_rg[layer]])[None, :])
        router = (norm_ffn_g[layer], wr, br)
        if layer % 2 == 0:
            w_in = l0_w_in[j]
            w_out = l0_w_out[j].astype(BF16)
            wa_t, wx_t = _blockdiag_tiles(l0_lru_wa[j]), _blockdiag_tiles(l0_lru_wx[j])
            rank = l0_gla_wa2.shape[1]
            wa2 = jnp.pad(l0_gla_wa2[j], ((0, LANES - rank), (0, 0))).astype(BF16)
            for g in groups:
                w = g.st["lru_h"].shape[-1]
                _, _, nh, dk, dv = g.st["gla_s"].shape
                main = 2 * w + 2 * nh * dk + 2 * nh * dv
                proj = fused_linear([g.x], w_in, n_out=main, gain=norm_mix_g[layer], name="linear_in")
                lg = gla_decay(g.x, norm_mix_g[layer], _pad_cols(w_in[:, main:main + rank]).astype(BF16), wa2,
                               l0_gla_ba2[j])
                ya, h_last = lru_branch(
                    proj, _history_tiles(g.st["lru_conv"][j]), g.st["lru_h"][j], l0_lru_conv_w[j], l0_lru_conv_b[j],
                    wa_t, l0_lru_ba[j], wx_t, l0_lru_bx[j], l0_lru_lam[j], nseq=g.nseq, t=g.t, **g.plan["lru"])
                yb, s_new = gla_branch(
                    proj, lg, g.st["gla_s"][j], l0_gla_norm_g[j], nseq=g.nseq, t=g.t, **g.plan["gla"], q_off=2 * w,
                    k_off=2 * w + nh * dk, v_off=2 * w + 2 * nh * dk, g_off=2 * w + 2 * nh * dk + nh * dv)
                g.x, g.xn, g.gates = fused_linear([ya, yb], w_out, n_out=g.d, res=g.x, router=router, tm=512,
                                                  tn=g.d, name="linear_out")
                g.outs["lru_conv"].append(proj.reshape(g.nseq, g.t, main)[:, g.t - hist:, :w])
                g.outs["lru_h"].append(h_last)
                g.outs["gla_s"].append(s_new)
        else:
            w_up = l1_w_up[j]
            w_down = l1_w_down[j].astype(BF16)
            wq_t, wk_t, wv_t = (_blockdiag_tiles(m[j]) for m in (l1_wq, l1_wk, l1_wv))
            wg = _pad_cols(jnp.concatenate([l1_w_ig[j], l1_w_fg[j]], axis=1)).astype(BF16)
            bg = _pad_cols(jnp.concatenate([l1_b_ig[j], l1_b_fg[j]])[None, :])
            fronts = []
            for g in groups:
                _, _, nh, dh = g.st["m_n"].shape
                up = fused_linear([g.x], w_up, n_out=2 * nh * dh, gain=norm_mix_g[layer], name="linear_up")
                q, k, v, xc, gates_m = mlstm_pre(
                    up, _history_tiles(g.st["m_conv"][j]), l1_conv_w[j], l1_conv_b[j], wq_t, wk_t, wv_t, wg, bg,
                    nseq=g.nseq, t=g.t, n_heads=nh, **g.plan["pre"])
                fronts.append((q, k, v, gates_m, xc, up))
                g.outs["m_conv"].append(up.reshape(g.nseq, g.t, 2 * nh * dh)[:, g.t - hist:, :nh * dh])
            recs = _mlstm_recurrences(groups, fronts, j, l1_skip[j], l1_norm_g[j])
            for g, (hout, c_new, n_new, m_new) in zip(groups, recs):
                g.x, g.xn, g.gates = fused_linear([hout], w_down, n_out=g.d, res=g.x, router=router, tm=256,
                                                  tn=g.d, name="linear_down")
                g.outs["m_c"].append(c_new)
                g.outs["m_n"].append(n_new)
                g.outs["m_m"].append(m_new)
        for g in groups:
            g.x = moe_experts(g.xn, g.gates, moe_w_gate, moe_w_up, moe_w_down, layer, g.x,
                              final_gain=norm_final_g if layer == depth - 1 else None)
    return [g.result() for g in groups]


def _mlstm_recurrences(groups, fronts, j, skip, norm_g):
    nh = groups[0].st["m_n"].shape[2]
    hb = groups[0].plan["rec"]["hb"]
    if len(groups) == 2:
        for il, i_s in ((0, 1), (1, 0)):
            gl, gs = groups[il], groups[i_s]
            fits = gl.st["m_c"] is None and gs.st["m_c"] is not None and gs.t == SUBLANES and gl.t > SUBLANES
            chunk = dual_chunk_rows(gl.nseq, gl.t, gs.nseq, nh, hb) if fits else None
            if chunk is not None:
                rl, rs = mlstm_recurrence_dual(
                    fronts[il], fronts[i_s], gs.st["m_c"][j], gs.st["m_n"][j], gs.st["m_m"][j], skip, norm_g,
                    nseq_l=gl.nseq, t_l=gl.t, nseq_s=gs.nseq, t_s=gs.t, nh=nh, hb=hb, chunk=chunk)
                return [rl, rs] if il == 0 else [rs, rl]
    return [mlstm_recurrence(*f, None if g.st["m_c"] is None else g.st["m_c"][j], g.st["m_n"][j], g.st["m_m"][j],
                             skip, norm_g, nseq=g.nseq, t=g.t, nh=nh, **g.plan["rec"])
            for g, f in zip(groups, fronts)]


def _trunk(x3, *states_and_weights):
    return _trunks([_Group(x3, states_and_weights[:7])], *states_and_weights[7:])[0]


def kernel(x_prompt, x_sample, state_lru_conv, state_lru_h, state_gla_S, state_mlstm_conv, state_mlstm_C,
           state_mlstm_n, state_mlstm_m, norm_mix_g, norm_ffn_g, norm_final_g, l0_w_in, l0_lru_conv_w,
           l0_lru_conv_b, l0_lru_wa, l0_lru_ba, l0_lru_wx, l0_lru_bx, l0_lru_lam, l0_gla_wa2, l0_gla_ba2,
           l0_gla_norm_g, l0_w_out, l1_w_up, l1_conv_w, l1_conv_b, l1_wq, l1_wk, l1_wv, l1_w_ig, l1_b_ig,
           l1_w_fg, l1_b_fg, l1_skip, l1_norm_g, l1_w_down, moe_w_rg, moe_b_rg, moe_w_re, moe_b_re, moe_w_gate,
           moe_w_up, moe_w_down):
    moe_w_gate, moe_w_up, moe_w_down = (w.astype(BF16) for w in (moe_w_gate, moe_w_up, moe_w_down))
    weights = (norm_mix_g, norm_ffn_g, norm_final_g, l0_w_in, l0_lru_conv_w, l0_lru_conv_b, l0_lru_wa, l0_lru_ba,
               l0_lru_wx, l0_lru_bx, l0_lru_lam, l0_gla_wa2, l0_gla_ba2, l0_gla_norm_g, l0_w_out, l1_w_up,
               l1_conv_w, l1_conv_b, l1_wq, l1_wk, l1_wv, l1_w_ig, l1_b_ig, l1_w_fg, l1_b_fg, l1_skip, l1_norm_g,
               l1_w_down, moe_w_rg, moe_b_rg, moe_w_re, moe_b_re, moe_w_gate, moe_w_up, moe_w_down)
    states = (state_lru_conv, state_lru_h, state_gla_S, state_mlstm_conv, state_mlstm_C, state_mlstm_n,
              state_mlstm_m)
    bp = x_prompt.shape[0]
    zero_states = tuple(None if s is state_mlstm_C else jnp.zeros((s.shape[0], bp) + s.shape[2:], s.dtype)
                        for s in states)
    prompt, sample = _trunks([_Group(x_prompt, zero_states), _Group(x_sample, states)], *weights)
    return (prompt[0], sample[0]) + prompt[1:] + sample[1:]
```

```python
import functools

import jax
import jax.numpy as jnp
from jax import lax
from jax.experimental import pallas as pl
from jax.experimental.pallas import tpu as pltpu

EPS = 1e-6
CONV_W = 4
LRU_C = 8.0
GLA_TAU = 16.0
GLA_CHUNK = 64
GLA_SUB = 8
M_CHUNK = 256
N_GROUPS = 4
E_PER_GROUP = 4
N_EXPERTS = N_GROUPS * E_PER_GROUP

V7X_VMEM_BYTES = 64 * 1024 * 1024
VMEM_LIMIT_BYTES = V7X_VMEM_BYTES - 8 * 1024 * 1024
SUBLANES = 8
LANES = 128
MXU_WIDTH = 256

F32 = jnp.float32
BF16 = jnp.bfloat16


def _params(*semantics):
    return pltpu.CompilerParams(dimension_semantics=semantics, vmem_limit_bytes=VMEM_LIMIT_BYTES)


def _dot(a, b):
    return jnp.dot(a.astype(BF16), b.astype(BF16), preferred_element_type=F32)


def _dot_nt(a, b):
    return lax.dot_general(a.astype(BF16), b.astype(BF16), (((1,), (1,)), ((), ())), preferred_element_type=F32)


def _dot_tn(a, b):
    return lax.dot_general(a.astype(BF16), b.astype(BF16), (((0,), (0,)), ((), ())), preferred_element_type=F32)


def _dot_split(m01, y):
    y_hi = y.astype(BF16)
    y_lo = (y - y_hi.astype(F32)).astype(BF16)
    m = m01.astype(BF16)
    return jnp.dot(m, y_hi, preferred_element_type=F32) + jnp.dot(m, y_lo, preferred_element_type=F32)


def _softplus(x):
    return jnp.maximum(x, 0.0) + jnp.log1p(jnp.exp(-jnp.abs(x)))


def _log_sigmoid(x):
    return -_softplus(-x)


def _rmsnorm_rows(x, g):
    return x * lax.rsqrt(jnp.mean(x * x, axis=-1, keepdims=True) + EPS) * g


def _iota(shape, dim):
    return lax.broadcasted_iota(jnp.int32, shape, dim)


def _lane_column(x, lane_index):
    return jnp.sum(jnp.where(_iota(x.shape, 1) == lane_index, x, 0.0), axis=1, keepdims=True)


def _row_tile(n, target):
    t = min(n, target)
    assert n % t == 0
    return t


def _route(xn, wr_ref, br_ref):
    logits = _dot(xn, wr_ref[...]) + br_ref[...]
    lane = _iota(logits.shape, 1)
    big = jnp.int32(LANES)
    is_g = jnp.logical_and(lane >= N_EXPERTS, lane < N_EXPERTS + N_GROUPS)
    gl = jnp.where(is_g, logits, -jnp.inf)
    gmax = jnp.max(gl, axis=1, keepdims=True)
    gsum = jnp.sum(jnp.where(is_g, jnp.exp(gl - gmax), 0.0), axis=1, keepdims=True)
    p_g = 1.0 / gsum
    g_idx = jnp.min(jnp.where(gl == gmax, lane, big), axis=1, keepdims=True) - N_EXPERTS
    sel = jnp.logical_and(lane < N_EXPERTS, (lane >> 2) == g_idx)
    el = jnp.where(sel, logits, -jnp.inf)
    emax = jnp.max(el, axis=1, keepdims=True)
    eexp = jnp.where(sel, jnp.exp(el - emax), 0.0)
    ep = eexp / jnp.sum(eexp, axis=1, keepdims=True)
    cand = jnp.where(sel, ep, -1.0)
    v1 = jnp.max(cand, axis=1, keepdims=True)
    idx1 = jnp.min(jnp.where(cand == v1, lane, big), axis=1, keepdims=True)
    cand2 = jnp.where(lane == idx1, -1.0, cand)
    v2 = jnp.max(cand2, axis=1, keepdims=True)
    idx2 = jnp.min(jnp.where(cand2 == v2, lane, big), axis=1, keepdims=True)
    tot = v1 + v2
    return jnp.where(lane == idx1, v1 / tot * p_g, 0.0) + jnp.where(lane == idx2, v2 / tot * p_g, 0.0)


def _linear_kernel(*refs, n_lhs, has_norm, has_res, has_router, has_decay):
    lhs_refs = refs[:n_lhs]
    pos = n_lhs
    g_ref = refs[pos] if has_norm else None
    pos += int(has_norm)
    w_ref = refs[pos]
    pos += 1
    res_ref = refs[pos] if has_res else None
    pos += int(has_res)
    if has_router:
        fg_ref, wr_ref, br_ref = refs[pos:pos + 3]
        pos += 3
    if has_decay:
        walr_ref, wa2_ref, ba2_ref = refs[pos:pos + 3]
        pos += 3
    o_ref = refs[pos]
    pos += 1
    if has_router:
        xn_ref, gates_ref = refs[pos:pos + 2]
        pos += 2
    if has_decay:
        lg_ref = refs[pos]
        pos += 1
    lhs_scr = refs[pos]

    @pl.when(pl.program_id(1) == 0)
    def _():
        off = 0
        for a_ref in lhs_refs:
            a = a_ref[...]
            if has_norm:
                a = _rmsnorm_rows(a, g_ref[...])
            lhs_scr[:, off:off + a.shape[1]] = a.astype(BF16)
            off += a.shape[1]
        if has_decay:
            alr = jnp.dot(lhs_scr[...], walr_ref[...], preferred_element_type=F32)
            lg_ref[...] = _log_sigmoid(_dot(alr, wa2_ref[...]) + ba2_ref[...]) * (1.0 / GLA_TAU)

    acc = jnp.dot(lhs_scr[...], w_ref[...].astype(BF16), preferred_element_type=F32)
    if has_res:
        acc = res_ref[...] + acc
    o_ref[...] = acc
    if has_router:
        xn = _rmsnorm_rows(acc, fg_ref[...])
        xn_ref[...] = xn.astype(BF16)
        gates_ref[...] = _route(xn, wr_ref, br_ref)


def fused_linear(lhs_list, w, *, n_out, gain=None, res=None, router=None, decay=None, tm=1024, tn=1024,
                 name="linear"):
    n = lhs_list[0].shape[0]
    ks = [a.shape[1] for a in lhs_list]
    ktot = sum(ks)
    assert w.shape[0] == ktot
    tm = _row_tile(n, tm)
    tn = _row_tile(n_out, tn)
    resident = tn == n_out == w.shape[1]
    assert router is None or resident
    in_specs = [pl.BlockSpec((tm, k), lambda i, j: (i, 0)) for k in ks]
    args = list(lhs_list)
    if gain is not None:
        in_specs.append(pl.BlockSpec((1, ktot), lambda i, j: (0, 0)))
        args.append(gain.reshape(1, ktot))
    in_specs.append(pl.BlockSpec((ktot, tn), lambda i, j: (0, j), pipeline_mode=pl.Buffered(1) if resident else None))
    args.append(w)
    if res is not None:
        in_specs.append(pl.BlockSpec((tm, tn), lambda i, j: (i, j)))
        args.append(res)
    out_specs = [pl.BlockSpec((tm, tn), lambda i, j: (i, j))]
    out_shape = [jax.ShapeDtypeStruct((n, n_out), F32)]
    if router is not None:
        fgain, wr, br = router
        in_specs += [pl.BlockSpec((1, n_out), lambda i, j: (0, 0)), pl.BlockSpec((n_out, LANES), lambda i, j: (0, 0)),
                     pl.BlockSpec((1, LANES), lambda i, j: (0, 0))]
        args += [fgain.reshape(1, n_out), wr, br]
        out_specs += [pl.BlockSpec((tm, n_out), lambda i, j: (i, 0)), pl.BlockSpec((tm, LANES), lambda i, j: (i, 0))]
        out_shape += [jax.ShapeDtypeStruct((n, n_out), BF16), jax.ShapeDtypeStruct((n, LANES), F32)]
    if decay is not None:
        w_alr, wa2, ba2 = decay
        hk = wa2.shape[1]
        in_specs += [pl.BlockSpec((ktot, LANES), lambda i, j: (0, 0)), pl.BlockSpec((LANES, hk), lambda i, j: (0, 0)),
                     pl.BlockSpec((1, hk), lambda i, j: (0, 0))]
        args += [w_alr, wa2, ba2.reshape(1, hk)]
        out_specs.append(pl.BlockSpec((tm, hk), lambda i, j: (i, 0)))
        out_shape.append(jax.ShapeDtypeStruct((n, hk), F32))
    kern = functools.partial(_linear_kernel, n_lhs=len(lhs_list), has_norm=gain is not None, has_res=res is not None,
                             has_router=router is not None, has_decay=decay is not None)
    outs = pl.pallas_call(
        kern,
        grid=(n // tm, n_out // tn),
        in_specs=in_specs,
        out_specs=out_specs,
        out_shape=out_shape,
        scratch_shapes=[pltpu.VMEM((tm, ktot), BF16)],
        compiler_params=_params("parallel", "arbitrary"),
        name=name,
    )(*args)
    return outs if len(outs) > 1 else outs[0]


def _causal_conv_short(x, prev, w_ref, b_ref):
    r = x.shape[0]
    pos = _iota((r, 1), 0) & (SUBLANES - 1)
    acc = b_ref[...] + x * w_ref[CONV_W - 1:CONV_W, :]
    for j in range(1, CONV_W):
        shifted = jnp.where(pos < j, pltpu.roll(prev, (j - SUBLANES) % r, 0), pltpu.roll(x, j, 0))
        acc = acc + shifted * w_ref[CONV_W - 1 - j:CONV_W - j, :]
    return acc


def _causal_conv_chunk(x, hist_scr, w_ref, b_ref):
    r = x.shape[0]
    acc = b_ref[...] + x * w_ref[CONV_W - 1:CONV_W, :]
    head = x[0:SUBLANES]
    hist = hist_scr[...]
    acc_head = b_ref[...] + head * w_ref[CONV_W - 1:CONV_W, :]
    pos = _iota((SUBLANES, 1), 0)
    for j in range(1, CONV_W):
        tap = w_ref[CONV_W - 1 - j:CONV_W - j, :]
        acc = acc + pltpu.roll(x, j, 0) * tap
        acc_head = acc_head + jnp.where(pos < j, pltpu.roll(hist, j, 0), pltpu.roll(head, j, 0)) * tap
    hist_scr[...] = x[r - SUBLANES:]
    return jnp.concatenate([acc_head, acc[SUBLANES:]], axis=0)


def _lru_kernel(xa_ref, ga_ref, prev_ref, h0_ref, cw_ref, cb_ref, wa_ref, ba_ref, wx_ref, bx_ref, lam_ref,
                ya_ref, hl_ref, prev_scr, h_scr, a_scr, b_scr, hs_scr, *, nb, tc):
    c = pl.program_id(1)

    @pl.when(c == 0)
    def _():
        if nb == 1:
            prev_scr[...] = prev_ref[...]
        h_scr[...] = h0_ref[...]

    x = xa_ref[...]
    if nb == 1:
        xc = _causal_conv_chunk(x, prev_scr, cw_ref, cb_ref)
    else:
        xc = _causal_conv_short(x, prev_ref[...], cw_ref, cb_ref)
    nblk = x.shape[1] // MXU_WIDTH
    xcb = xc.astype(BF16)
    r_parts, i_parts = [], []
    for t in range(nblk):
        sl = slice(t * MXU_WIDTH, (t + 1) * MXU_WIDTH)
        r_parts.append(jnp.dot(xcb[:, sl], wa_ref[t], preferred_element_type=F32))
        i_parts.append(jnp.dot(xcb[:, sl], wx_ref[t], preferred_element_type=F32))
    r_gate = jax.nn.sigmoid(jnp.concatenate(r_parts, axis=1) + ba_ref[...])
    i_gate = jax.nn.sigmoid(jnp.concatenate(i_parts, axis=1) + bx_ref[...])
    log_a = (-LRU_C) * r_gate * _softplus(-lam_ref[...])
    a = jnp.exp(log_a)
    a_scr[...] = a
    b_scr[...] = jnp.sqrt(-jnp.tanh(log_a) * (a * a + 1.0)) * (i_gate * xc)

    def seq_body(s, carry):
        def tile_body(tl, h):
            base = pl.multiple_of(s * tc + tl * SUBLANES, SUBLANES)
            for i in range(SUBLANES):
                h = a_scr[pl.ds(base + i, 1), :] * h + b_scr[pl.ds(base + i, 1), :]
                hs_scr[pl.ds(base + i, 1), :] = h
            return h

        h_scr[s] = lax.fori_loop(0, tc // SUBLANES, tile_body, h_scr[s])
        return carry

    lax.fori_loop(0, nb, seq_body, 0, unroll=min(nb, 4))
    ya_ref[...] = hs_scr[...] * jax.nn.gelu(ga_ref[...])

    @pl.when(c == pl.num_programs(1) - 1)
    def _():
        hl_ref[...] = h_scr[...]


def lru_branch(proj, prev8, h0, conv_w, conv_b, wa_t, ba, wx_t, bx, lam, *, nseq, t, nb, tc):
    n = proj.shape[0]
    w = h0.shape[1]
    nchunk = t // tc
    assert nseq % nb == 0 and t % tc == 0 and (nb == 1 or tc == t == SUBLANES)
    r = nb * tc
    nblk = w // MXU_WIDTH
    rows = lambda s, c: s * nchunk + c
    full2 = lambda s, c: (0, 0)
    full3 = lambda s, c: (0, 0, 0)
    kern = functools.partial(_lru_kernel, nb=nb, tc=tc)
    ya, hl = pl.pallas_call(
        kern,
        grid=(nseq // nb, nchunk),
        in_specs=[
            pl.BlockSpec((r, w), lambda s, c: (rows(s, c), 0)),
            pl.BlockSpec((r, w), lambda s, c: (rows(s, c), 1)),
            pl.BlockSpec((nb * SUBLANES, w), lambda s, c: (s, 0)),
            pl.BlockSpec((nb, 1, w), lambda s, c: (s, 0, 0)),
            pl.BlockSpec((CONV_W, w), full2),
            pl.BlockSpec((1, w), full2),
            pl.BlockSpec((nblk, MXU_WIDTH, MXU_WIDTH), full3),
            pl.BlockSpec((1, w), full2),
            pl.BlockSpec((nblk, MXU_WIDTH, MXU_WIDTH), full3),
            pl.BlockSpec((1, w), full2),
            pl.BlockSpec((1, w), full2),
        ],
        out_specs=[
            pl.BlockSpec((r, w), lambda s, c: (rows(s, c), 0)),
            pl.BlockSpec((nb, 1, w), lambda s, c: (s, 0, 0)),
        ],
        out_shape=[jax.ShapeDtypeStruct((n, w), F32), jax.ShapeDtypeStruct((nseq, 1, w), F32)],
        scratch_shapes=[
            pltpu.VMEM((SUBLANES, w), F32),
            pltpu.VMEM((nb, 1, w), F32),
            pltpu.VMEM((r, w), F32),
            pltpu.VMEM((r, w), F32),
            pltpu.VMEM((r, w), F32),
        ],
        compiler_params=_params("parallel", "arbitrary"),
        name="lru_branch",
    )(proj, proj, prev8, h0.reshape(nseq, 1, w), conv_w, conv_b.reshape(1, w), wa_t, ba.reshape(1, w), wx_t,
      bx.reshape(1, w), lam.reshape(1, w))
    return ya, hl.reshape(nseq, w)


def _gla_cumdecay(lg, run_rows):
    r = lg.shape[0]
    row = _iota((r, r), 0)
    col = _iota((r, r), 1)
    shift = run_rows.bit_length() - 1
    tri = jnp.logical_and(row >= col, (row >> shift) == (col >> shift))
    return _dot_split(tri.astype(F32), lg)


def _gla_near_att(qs, k, bc, sub):
    r = qs.shape[0]
    row = _iota((r, r), 0)
    col = _iota((r, r), 1)
    posr = _iota((r, 1), 0) & (sub - 1)
    att = jnp.zeros((r, r), F32)
    for d in range(sub):
        kd = k if d == 0 else pltpu.roll(k, d, 0)
        bcd = bc if d == 0 else pltpu.roll(bc, d, 0)
        valid = posr >= d
        prod = qs * kd * jnp.exp(jnp.where(valid, bc - bcd, 0.0))
        diag = jnp.sum(jnp.where(valid, prod, 0.0), axis=1, keepdims=True)
        att = att + jnp.where(col == row - d, diag, 0.0)
    return att


def _gla_far_att(qs, k, bc, sub):
    l, dk = qs.shape
    att = jnp.zeros((l, l), F32)
    for j in range(l // sub - 1):
        lo, hi = j * sub, (j + 1) * sub
        e_j = bc[hi - 1:hi, :]
        kp = k[lo:hi] * jnp.exp(e_j - bc[lo:hi])
        qp = qs[hi:] * jnp.exp(bc[hi:] - e_j)
        k_rows = [jnp.zeros((lo, dk), F32)] * (lo > 0) + [kp, jnp.zeros((l - hi, dk), F32)]
        att = att + _dot_nt(jnp.concatenate([jnp.zeros((hi, dk), F32), qp], axis=0), jnp.concatenate(k_rows, axis=0))
    return att


def _as_column(row_vec):
    d = row_vec.shape[1]
    eye = _iota((d, d), 0) == _iota((d, d), 1)
    return jnp.sum(jnp.where(eye, row_vec, 0.0), axis=1, keepdims=True)


def _gla_kernel(q_ref, k_ref, v_ref, g_ref, lg_ref, s0_ref, gn_ref, yb_ref, sn_ref, *scratch, nb, chunk, sub):
    c = pl.program_id(2)
    scale = q_ref.shape[1] ** -0.5

    def finish(o, g):
        on = o * lax.rsqrt(jnp.mean(o * o, axis=-1, keepdims=True) + EPS) * gn_ref[...]
        return on * jax.nn.silu(g)

    if nb == 1:
        (s_scr,) = scratch

        @pl.when(c == 0)
        def _():
            s_scr[...] = s0_ref[0, 0]

        def body(i, carry):
            rows = pl.ds(pl.multiple_of(i * chunk, chunk), chunk)
            qs = q_ref[rows, :] * scale
            k = k_ref[rows, :]
            v = v_ref[rows, :]
            bc = _gla_cumdecay(lg_ref[rows, :], chunk)
            s_state = s_scr[...]
            o = _dot(qs * jnp.exp(bc), s_state) + _dot(_gla_near_att(qs, k, bc, sub) + _gla_far_att(qs, k, bc, sub), v)
            bl = bc[chunk - 1:chunk, :]
            s_scr[...] = _as_column(jnp.exp(bl)) * s_state + _dot_tn(k * jnp.exp(bl - bc), v)
            yb_ref[rows, :] = finish(o, g_ref[rows, :])
            return carry

        lax.fori_loop(0, q_ref.shape[0] // chunk, body, 0, unroll=True)

        @pl.when(c == pl.num_programs(2) - 1)
        def _():
            sn_ref[0, 0] = s_scr[...]
    else:
        qe_scr, kd_scr, eb_scr, o_scr = scratch
        r = q_ref.shape[0]
        qs = q_ref[...] * scale
        k = k_ref[...]
        bc = _gla_cumdecay(lg_ref[...], chunk)
        o_scr[...] = _dot(_gla_near_att(qs, k, bc, sub), v_ref[...])
        qe_scr[...] = qs * jnp.exp(bc)
        row = _iota((r, r), 0)
        col = _iota((r, r), 1)
        last = (col == (row | (chunk - 1))).astype(F32)
        bl = _dot_split(last, bc)
        kd_scr[...] = k * jnp.exp(bl - bc)
        eb_scr[...] = jnp.exp(bl)

        def body(j, carry):
            rows = pl.ds(pl.multiple_of(j * chunk, chunk), chunk)
            s_state = s0_ref[j, 0]
            o_scr[rows, :] += _dot(qe_scr[rows, :], s_state)
            decay = _as_column(eb_scr[pl.ds(pl.multiple_of(j * chunk, chunk), 1), :])
            sn_ref[j, 0] = decay * s_state + _dot_tn(kd_scr[rows, :], v_ref[rows, :])
            return carry

        lax.fori_loop(0, nb, body, 0, unroll=min(nb, 8))
        yb_ref[...] = finish(o_scr[...], g_ref[...])


def gla_branch(proj, lg, s0, gnorm, *, nseq, t, nb, rows_blk, chunk, sub, q_off, k_off, v_off, g_off):
    n = proj.shape[0]
    _, nh, dk, dv = s0.shape
    ntb = t // rows_blk if nb == 1 else 1
    r = rows_blk if nb == 1 else nb * t
    assert (nb == 1 and t % rows_blk == 0 and rows_blk % chunk == 0) or (chunk == sub == t and nseq % nb == 0)
    rows = lambda s, h, c: s * ntb + c
    kern = functools.partial(_gla_kernel, nb=nb, chunk=chunk, sub=sub)
    if nb == 1:
        scratch = [pltpu.VMEM((dk, dv), F32)]
    else:
        scratch = [pltpu.VMEM((r, dk), F32), pltpu.VMEM((r, dk), F32), pltpu.VMEM((r, dk), F32), pltpu.VMEM((r, dv), F32)]
    yb, sn = pl.pallas_call(
        kern,
        grid=(nseq // nb, nh, ntb),
        in_specs=[
            pl.BlockSpec((r, dk), lambda s, h, c: (rows(s, h, c), q_off // dk + h)),
            pl.BlockSpec((r, dk), lambda s, h, c: (rows(s, h, c), k_off // dk + h)),
            pl.BlockSpec((r, dv), lambda s, h, c: (rows(s, h, c), v_off // dv + h)),
            pl.BlockSpec((r, dv), lambda s, h, c: (rows(s, h, c), g_off // dv + h)),
            pl.BlockSpec((r, dk), lambda s, h, c: (rows(s, h, c), h)),
            pl.BlockSpec((nb, 1, dk, dv), lambda s, h, c: (s, h, 0, 0)),
            pl.BlockSpec((1, dv), lambda s, h, c: (0, h)),
        ],
        out_specs=[
            pl.BlockSpec((r, dv), lambda s, h, c: (rows(s, h, c), h)),
            pl.BlockSpec((nb, 1, dk, dv), lambda s, h, c: (s, h, 0, 0)),
        ],
        out_shape=[jax.ShapeDtypeStruct((n, nh * dv), F32), jax.ShapeDtypeStruct(s0.shape, F32)],
        scratch_shapes=scratch,
        compiler_params=_params("parallel", "parallel", "arbitrary"),
        name="gla_branch",
    )(proj, proj, proj, proj, lg, s0, gnorm.reshape(1, nh * dv))
    return yb, sn


def _mlstm_pre_kernel(xm_ref, prev_ref, cw_ref, cb_ref, wq_ref, wk_ref, wv_ref, wg_ref, bg_ref,
                      q_ref, k_ref, v_ref, xc_ref, gate_ref, prev_scr, *, nb, tc, n_heads):
    c = pl.program_id(1)

    x = xm_ref[...]
    if nb == 1:
        @pl.when(c == 0)
        def _():
            prev_scr[...] = prev_ref[...]

        xc = jax.nn.silu(_causal_conv_chunk(x, prev_scr, cw_ref, cb_ref))
    else:
        xc = jax.nn.silu(_causal_conv_short(x, prev_ref[...], cw_ref, cb_ref))
    xc_ref[...] = xc
    di = x.shape[1]
    xcb = xc.astype(BF16)
    xb = x.astype(BF16)
    for t in range(di // MXU_WIDTH):
        sl = slice(t * MXU_WIDTH, (t + 1) * MXU_WIDTH)
        q_ref[:, sl] = jnp.dot(xcb[:, sl], wq_ref[t], preferred_element_type=F32)
        k_ref[:, sl] = jnp.dot(xcb[:, sl], wk_ref[t], preferred_element_type=F32)
        v_ref[:, sl] = jnp.dot(xb[:, sl], wv_ref[t], preferred_element_type=F32)
    pre = (_dot(q_ref[...], wg_ref[0:di, :]) + _dot(k_ref[...], wg_ref[di:2 * di, :])
           + _dot(v_ref[...], wg_ref[2 * di:3 * di, :]) + bg_ref[...])
    gate_ref[...] = jnp.where(_iota(pre.shape, 1) < n_heads, pre, _log_sigmoid(pre))


def mlstm_pre(up, prev8, conv_w, conv_b, wq_t, wk_t, wv_t, wg, bg, *, nseq, t, nb, tc, n_heads):
    n = up.shape[0]
    di = conv_w.shape[1]
    nchunk = t // tc
    assert nseq % nb == 0 and t % tc == 0 and (nb == 1 or tc == t == SUBLANES)
    r = nb * tc
    nblk = di // MXU_WIDTH
    rows = lambda s, c: (s * nchunk + c, 0)
    full2 = lambda s, c: (0, 0)
    full3 = lambda s, c: (0, 0, 0)
    wide = jax.ShapeDtypeStruct((n, di), F32)
    kern = functools.partial(_mlstm_pre_kernel, nb=nb, tc=tc, n_heads=n_heads)
    return pl.pallas_call(
        kern,
        grid=(nseq // nb, nchunk),
        in_specs=[
            pl.BlockSpec((r, di), rows),
            pl.BlockSpec((nb * SUBLANES, di), lambda s, c: (s, 0)),
            pl.BlockSpec((CONV_W, di), full2),
            pl.BlockSpec((1, di), full2),
            pl.BlockSpec((nblk, MXU_WIDTH, MXU_WIDTH), full3),
            pl.BlockSpec((nblk, MXU_WIDTH, MXU_WIDTH), full3),
            pl.BlockSpec((nblk, MXU_WIDTH, MXU_WIDTH), full3),
            pl.BlockSpec((3 * di, LANES), full2),
            pl.BlockSpec((1, LANES), full2),
        ],
        out_specs=[pl.BlockSpec((r, di), rows)] * 4 + [pl.BlockSpec((r, LANES), rows)],
        out_shape=[wide, wide, wide, wide, jax.ShapeDtypeStruct((n, LANES), F32)],
        scratch_shapes=[pltpu.VMEM((SUBLANES, di), F32)],
        compiler_params=_params("parallel", "arbitrary"),
        name="mlstm_pre",
    )(up, prev8, conv_w, conv_b.reshape(1, di), wq_t, wk_t, wv_t, wg, bg)


def _mlstm_chunk(q, k, v, igc, lfc, xc, z, skip, ng, m_prev, n_prev, c_load, c_store):
    l, dh = q.shape
    row = _iota((l, l), 0)
    col = _iota((l, l), 1)
    tril = row >= col
    f_b = jnp.broadcast_to(lfc, (l, l))
    i_b = jnp.broadcast_to(igc, (l, l))
    f_col = _dot_split(tril.astype(F32), f_b)
    row_term = _dot_split(jnp.ones((l, l), F32),
                          jnp.where(row == col, i_b, 0.0) - jnp.where(row <= col, f_b, 0.0))
    dm = jnp.where(tril, f_col + row_term, -jnp.inf)
    fcum = f_col[:, 0:1]
    prev = m_prev + fcum
    mt = jnp.maximum(prev, jnp.max(dm, axis=1, keepdims=True))
    wprev = jnp.exp(prev - mt)
    smat = _dot_nt(q, k) * jnp.exp(dm - mt)
    den = wprev * jnp.sum(q * n_prev, axis=1, keepdims=True) + jnp.sum(smat, axis=1, keepdims=True)
    inv = 1.0 / jnp.maximum(jnp.abs(den), jnp.exp(-mt))
    f_last = fcum[l - 1:l, :]
    m_last = mt[l - 1:l, :]
    w_c = jnp.exp(m_prev + f_last - m_last)
    kw = k * jnp.exp(f_last - fcum + igc - m_last)
    qb = q.astype(BF16)
    sb = smat.astype(BF16)
    kwb = kw.astype(BF16)
    parts = []
    for t in range(dh // MXU_WIDTH):
        cols = slice(t * MXU_WIDTH, (t + 1) * MXU_WIDTH)
        c_blk = c_load(cols)
        vb = v[:, cols].astype(BF16)
        num = (wprev * jnp.dot(qb, c_blk.astype(BF16), preferred_element_type=F32)
               + jnp.dot(sb, vb, preferred_element_type=F32))
        parts.append(num * inv)
        c_store(cols, w_c * c_blk + _dot_tn(kwb, vb))
    n_new = w_c * n_prev + jnp.sum(kw, axis=0, keepdims=True)
    hh = jnp.concatenate(parts, axis=1)
    hc = hh - jnp.mean(hh, axis=-1, keepdims=True)
    hn = hc * lax.rsqrt(jnp.mean(hc * hc, axis=-1, keepdims=True) + EPS) * ng
    return (hn + skip * xc) * jax.nn.silu(z), n_new, m_last


def _mlstm_seq_kernel(*refs, chunk, n_heads, zero_state):
    q_ref, k_ref, v_ref, gate_ref, xc_ref, z_ref = refs[:6]
    pos = 6
    if not zero_state:
        c0_ref, n0_ref, m0_ref = refs[pos:pos + 3]
        pos += 3
    skip_ref, ng_ref, out_ref, cn_ref, nn_ref, mn_ref, c_scr, n_scr, m_scr = refs[pos:pos + 9]
    head = pl.program_id(1)
    c = pl.program_id(2)

    @pl.when(c == 0)
    def _():
        if zero_state:
            c_scr[...] = jnp.zeros_like(c_scr)
            n_scr[...] = jnp.zeros_like(n_scr)
            m_scr[...] = jnp.zeros_like(m_scr)
        else:
            c_scr[...] = c0_ref[0, 0]
            n_scr[...] = n0_ref[0]
            m_scr[...] = m0_ref[0]

    kscale = q_ref.shape[1] ** -0.5

    def c_store(cols, value):
        c_scr[:, cols] = value

    def body(i, carry):
        rows = pl.ds(pl.multiple_of(i * chunk, chunk), chunk)
        gates = gate_ref[rows, :]
        out, n_new, m_last = _mlstm_chunk(
            q_ref[rows, :], k_ref[rows, :] * kscale, v_ref[rows, :], _lane_column(gates, head),
            _lane_column(gates, head + n_heads), xc_ref[rows, :], z_ref[rows, :], skip_ref[...], ng_ref[...],
            m_scr[:, 0:1], n_scr[...], lambda cols: c_scr[:, cols], c_store)
        out_ref[rows, :] = out
        n_scr[...] = n_new
        m_scr[...] = jnp.broadcast_to(m_last, m_scr.shape)
        return carry

    lax.fori_loop(0, q_ref.shape[0] // chunk, body, 0)

    @pl.when(c == pl.num_programs(2) - 1)
    def _():
        cn_ref[0, 0] = c_scr[...]
        nn_ref[0] = n_scr[...]
        mn_ref[0] = m_scr[...]


def _mlstm_step_kernel(q_ref, k_ref, v_ref, gate_ref, xc_ref, z_ref, c0_ref, n0_ref, m0_ref, skip_ref, ng_ref,
                       out_ref, cn_ref, nn_ref, mn_ref, *, hb, n_heads, head_block=None):
    hblk = pl.program_id(1) if head_block is None else head_block
    dh = c0_ref.shape[2]
    kscale = dh ** -0.5
    gates = gate_ref[...]
    for hh in range(hb):
        cols_h = slice(hh * dh, (hh + 1) * dh)
        head = hblk * hb + hh

        def c_store(cols, value, hh=hh):
            cn_ref[0, hh, :, cols] = value

        out, n_new, m_last = _mlstm_chunk(
            q_ref[:, cols_h], k_ref[:, cols_h] * kscale, v_ref[:, cols_h], _lane_column(gates, head),
            _lane_column(gates, head + n_heads), xc_ref[:, cols_h], z_ref[:, cols_h], skip_ref[:, cols_h],
            ng_ref[:, cols_h], m0_ref[hh][:, 0:1], n0_ref[hh], lambda cols, hh=hh: c0_ref[0, hh, :, cols], c_store)
        out_ref[:, cols_h] = out
        nn_ref[hh] = n_new
        mn_ref[hh] = jnp.broadcast_to(m_last, (1, LANES))


def mlstm_recurrence(q, k, v, gates, xc, up, c0, n0, m0, skip, norm_g, *, nseq, t, nh, rows_blk, chunk, hb):
    n, di = q.shape
    dh = di // nh
    zero_state = c0 is None
    out_shape = [
        jax.ShapeDtypeStruct((n, di), F32),
        jax.ShapeDtypeStruct((nseq, nh, dh, dh), F32),
        jax.ShapeDtypeStruct((nseq * nh, 1, dh), F32),
        jax.ShapeDtypeStruct((nseq * nh, 1, LANES), F32),
    ]
    state_args = []
    if not zero_state:
        state_args = [c0, n0.reshape(nseq * nh, 1, dh),
                      jnp.broadcast_to(m0.reshape(nseq * nh, 1, 1), (nseq * nh, 1, LANES))]
    if t == chunk and not zero_state:
        assert nh % hb == 0
        nhb = nh // hb
        wide = lambda s, h: (s, h)
        per_head = lambda s, h: (s * nhb + h, 0, 0)
        mat = lambda s, h: (s, h, 0, 0)
        out, cn, nn, mn = pl.pallas_call(
            functools.partial(_mlstm_step_kernel, hb=hb, n_heads=nh),
            grid=(nseq, nhb),
            in_specs=[pl.BlockSpec((t, hb * dh), wide)] * 3
            + [pl.BlockSpec((t, LANES), lambda s, h: (s, 0)), pl.BlockSpec((t, hb * dh), wide),
               pl.BlockSpec((t, hb * dh), lambda s, h: (s, nhb + h)), pl.BlockSpec((1, hb, dh, dh), mat),
               pl.BlockSpec((hb, 1, dh), per_head), pl.BlockSpec((hb, 1, LANES), per_head),
               pl.BlockSpec((1, hb * dh), lambda s, h: (0, h)), pl.BlockSpec((1, hb * dh), lambda s, h: (0, h))],
            out_specs=[pl.BlockSpec((t, hb * dh), wide), pl.BlockSpec((1, hb, dh, dh), mat),
                       pl.BlockSpec((hb, 1, dh), per_head), pl.BlockSpec((hb, 1, LANES), per_head)],
            out_shape=out_shape,
            compiler_params=_params("parallel", "arbitrary"),
            name="mlstm_step",
        )(q, k, v, gates, xc, up, *state_args, skip.reshape(1, di), norm_g.reshape(1, di))
    else:
        ntb = t // rows_blk
        assert t % rows_blk == 0 and rows_blk % chunk == 0
        rows = lambda s, h, c: (s * ntb + c, h)
        per_head = lambda s, h, c: (s * nh + h, 0, 0)
        mat = lambda s, h, c: (s, h, 0, 0)
        state_specs = [] if zero_state else [pl.BlockSpec((1, 1, dh, dh), mat), pl.BlockSpec((1, 1, dh), per_head),
                                             pl.BlockSpec((1, 1, LANES), per_head)]
        out, cn, nn, mn = pl.pallas_call(
            functools.partial(_mlstm_seq_kernel, chunk=chunk, n_heads=nh, zero_state=zero_state),
            grid=(nseq, nh, ntb),
            in_specs=[pl.BlockSpec((rows_blk, dh), rows)] * 3
            + [pl.BlockSpec((rows_blk, LANES), lambda s, h, c: (s * ntb + c, 0)), pl.BlockSpec((rows_blk, dh), rows),
               pl.BlockSpec((rows_blk, dh), lambda s, h, c: (s * ntb + c, nh + h))]
            + state_specs
            + [pl.BlockSpec((1, dh), lambda s, h, c: (0, h)), pl.BlockSpec((1, dh), lambda s, h, c: (0, h))],
            out_specs=[pl.BlockSpec((rows_blk, dh), rows), pl.BlockSpec((1, 1, dh, dh), mat),
                       pl.BlockSpec((1, 1, dh), per_head), pl.BlockSpec((1, 1, LANES), per_head)],
            out_shape=out_shape,
            scratch_shapes=[pltpu.VMEM((dh, dh), F32), pltpu.VMEM((1, dh), F32), pltpu.VMEM((1, LANES), F32)],
            compiler_params=_params("parallel", "parallel", "arbitrary"),
            name="mlstm_seq",
        )(q, k, v, gates, xc, up, *state_args, skip.reshape(1, di), norm_g.reshape(1, di))
    return out, cn, nn.reshape(nseq, nh, dh), mn[:, 0, 0].reshape(nseq, nh)


def _mlstm_dual_kernel(ql_ref, kl_ref, vl_ref, gl_ref, xcl_ref, zl_ref, skipl_ref, ngl_ref,
                       qs_ref, ks_ref, vs_ref, gs_ref, xcs_ref, zs_ref, c0_ref, n0_ref, m0_ref, skips_ref, ngs_ref,
                       outl_ref, cnl_ref, nnl_ref, mnl_ref, outs_ref, cns_ref, nns_ref, mns_ref, n_scr, m_scr,
                       *, hb, n_heads, chunks_per_seq):
    g = pl.program_id(0)
    c = g % chunks_per_seq
    head_l = (g // chunks_per_seq) % n_heads
    dh = ql_ref.shape[1]
    kscale = dh ** -0.5

    @pl.when(c == 0)
    def _():
        cnl_ref[...] = jnp.zeros_like(cnl_ref)
        n_scr[...] = jnp.zeros_like(n_scr)
        m_scr[...] = jnp.zeros_like(m_scr)

    def cl_store(cols, value):
        cnl_ref[0, 0, :, cols] = value

    gl = gl_ref[...]
    out, n_new, m_last = _mlstm_chunk(
        ql_ref[...], kl_ref[...] * kscale, vl_ref[...], _lane_column(gl, head_l), _lane_column(gl, head_l + n_heads),
        xcl_ref[...], zl_ref[...], skipl_ref[...], ngl_ref[...], m_scr[:, 0:1], n_scr[...],
        lambda cols: cnl_ref[0, 0, :, cols], cl_store)
    outl_ref[...] = out
    n_scr[...] = n_new
    m_scr[...] = jnp.broadcast_to(m_last, m_scr.shape)

    @pl.when(c == chunks_per_seq - 1)
    def _():
        nnl_ref[0] = n_scr[...]
        mnl_ref[0] = m_scr[...]

    _mlstm_step_kernel(qs_ref, ks_ref, vs_ref, gs_ref, xcs_ref, zs_ref, c0_ref, n0_ref, m0_ref, skips_ref, ngs_ref,
                       outs_ref, cns_ref, nns_ref, mns_ref, hb=hb, n_heads=n_heads, head_block=g % (n_heads // hb))


def dual_chunk_rows(nseq_l, t_l, nseq_s, nh, hb):
    steps = nseq_s * (nh // hb)
    total = nseq_l * t_l * nh
    if nh % hb or total % steps:
        return None
    chunk = total // steps
    ok = chunk % SUBLANES == 0 and t_l % chunk == 0 and 64 <= chunk <= M_CHUNK
    return chunk if ok else None


def mlstm_recurrence_dual(long_in, short_in, c0, n0, m0, skip, norm_g, *, nseq_l, t_l, nseq_s, t_s, nh, hb, chunk):
    di = long_in[0].shape[1]
    dh = di // nh
    nhb = nh // hb
    cps = t_l // chunk
    steps = nseq_s * nhb
    assert steps == nseq_l * nh * cps

    def rows_l(g):
        return (g // (nh * cps)) * cps + g % cps

    head_l = lambda g: (g // cps) % nh
    tile_l = lambda g: (rows_l(g), head_l(g))
    per_head_l = lambda g: (g // cps, 0, 0)
    tile_s = lambda g: (g // nhb, g % nhb)
    per_head_s = lambda g: (g, 0, 0)
    mat_s = lambda g: (g // nhb, g % nhb, 0, 0)
    n_l, n_s = long_in[0].shape[0], short_in[0].shape[0]
    in_specs = (
        [pl.BlockSpec((chunk, dh), tile_l)] * 3
        + [pl.BlockSpec((chunk, LANES), lambda g: (rows_l(g), 0)), pl.BlockSpec((chunk, dh), tile_l),
           pl.BlockSpec((chunk, dh), lambda g: (rows_l(g), nh + head_l(g))),
           pl.BlockSpec((1, dh), lambda g: (0, head_l(g))), pl.BlockSpec((1, dh), lambda g: (0, head_l(g)))]
        + [pl.BlockSpec((t_s, hb * dh), tile_s)] * 3
        + [pl.BlockSpec((t_s, LANES), lambda g: (g // nhb, 0)), pl.BlockSpec((t_s, hb * dh), tile_s),
           pl.BlockSpec((t_s, hb * dh), lambda g: (g // nhb, nhb + g % nhb)), pl.BlockSpec((1, hb, dh, dh), mat_s),
           pl.BlockSpec((hb, 1, dh), per_head_s), pl.BlockSpec((hb, 1, LANES), per_head_s),
           pl.BlockSpec((1, hb * dh), lambda g: (0, g % nhb)), pl.BlockSpec((1, hb * dh), lambda g: (0, g % nhb))])
    out_specs = [
        pl.BlockSpec((chunk, dh), tile_l), pl.BlockSpec((1, 1, dh, dh), lambda g: (g // (nh * cps), head_l(g), 0, 0)),
        pl.BlockSpec((1, 1, dh), per_head_l), pl.BlockSpec((1, 1, LANES), per_head_l),
        pl.BlockSpec((t_s, hb * dh), tile_s), pl.BlockSpec((1, hb, dh, dh), mat_s),
        pl.BlockSpec((hb, 1, dh), per_head_s), pl.BlockSpec((hb, 1, LANES), per_head_s)]
    out_shape = [
        jax.ShapeDtypeStruct((n_l, di), F32), jax.ShapeDtypeStruct((nseq_l, nh, dh, dh), F32),
        jax.ShapeDtypeStruct((nseq_l * nh, 1, dh), F32), jax.ShapeDtypeStruct((nseq_l * nh, 1, LANES), F32),
        jax.ShapeDtypeStruct((n_s, di), F32), jax.ShapeDtypeStruct((nseq_s, nh, dh, dh), F32),
        jax.ShapeDtypeStruct((nseq_s * nh, 1, dh), F32), jax.ShapeDtypeStruct((nseq_s * nh, 1, LANES), F32)]
    skip2, ng2 = skip.reshape(1, di), norm_g.reshape(1, di)
    res = pl.pallas_call(
        functools.partial(_mlstm_dual_kernel, hb=hb, n_heads=nh, chunks_per_seq=cps),
        grid=(steps,),
        in_specs=in_specs,
        out_specs=out_specs,
        out_shape=out_shape,
        scratch_shapes=[pltpu.VMEM((1, dh), F32), pltpu.VMEM((1, LANES), F32)],
        compiler_params=_params("arbitrary"),
        name="mlstm_dual",
    )(*long_in, skip2, ng2, *short_in, c0, n0.reshape(nseq_s * nh, 1, dh),
      jnp.broadcast_to(m0.reshape(nseq_s * nh, 1, 1), (nseq_s * nh, 1, LANES)), skip2, ng2)
    unpack = lambda o, cn, nn, mn, nseq: (o, cn, nn.reshape(nseq, nh, dh), mn[:, 0, 0].reshape(nseq, nh))
    return unpack(*res[:4], nseq_l), unpack(*res[4:], nseq_s)


def _moe_expert_kernel(*refs, has_final, eb):
    xn_ref, gates_ref, wg_ref, wu_ref, wd_ref, x_ref = refs[:6]
    fg_ref = refs[6] if has_final else None
    o_ref = refs[6 + int(has_final)]
    e = pl.program_id(1)

    @pl.when(e == 0)
    def _():
        o_ref[...] = x_ref[...]

    xn = xn_ref[...]
    gates = gates_ref[...]
    update = None
    for k in range(eb):
        hg = jnp.dot(xn, wg_ref[0, k].astype(BF16), preferred_element_type=F32)
        hu = jnp.dot(xn, wu_ref[0, k].astype(BF16), preferred_element_type=F32)
        h = jax.nn.silu(hg) * hu * _lane_column(gates, e * eb + k)
        part = _dot(h, wd_ref[0, k])
        update = part if update is None else update + part
    o_ref[...] += update

    if has_final:
        @pl.when(e == pl.num_programs(1) - 1)
        def _():
            o_ref[...] = _rmsnorm_rows(o_ref[...], fg_ref[...])


def moe_experts(xn, gates, w_gate, w_up, w_down, layer, x, final_gain=None, tm=1024, eb=2):
    n, d = x.shape
    _, ne, _, f = w_gate.shape
    tm = _row_tile(n, tm)
    assert ne % eb == 0
    in_specs = [
        pl.BlockSpec((tm, d), lambda i, e: (i, 0)),
        pl.BlockSpec((tm, LANES), lambda i, e: (i, 0)),
        pl.BlockSpec((1, eb, d, f), lambda i, e: (layer, e, 0, 0)),
        pl.BlockSpec((1, eb, d, f), lambda i, e: (layer, e, 0, 0)),
        pl.BlockSpec((1, eb, f, d), lambda i, e: (layer, e, 0, 0)),
        pl.BlockSpec((tm, d), lambda i, e: (i, 0), pipeline_mode=pl.Buffered(1)),
    ]
    args = [xn, gates, w_gate, w_up, w_down, x]
    if final_gain is not None:
        in_specs.append(pl.BlockSpec((1, d), lambda i, e: (0, 0)))
        args.append(final_gain.reshape(1, d))
    return pl.pallas_call(
        functools.partial(_moe_expert_kernel, has_final=final_gain is not None, eb=eb),
        grid=(n // tm, ne // eb),
        in_specs=in_specs,
        out_specs=pl.BlockSpec((tm, d), lambda i, e: (i, 0)),
        out_shape=jax.ShapeDtypeStruct((n, d), F32),
        compiler_params=_params("parallel", "arbitrary"),
        name="moe_experts",
    )(*args)


def _blockdiag_tiles(w):
    nblocks, bi, bo = w.shape
    per = MXU_WIDTH // bi
    ntiles = nblocks // per
    rows_of_tile = w.reshape(ntiles, MXU_WIDTH, bo)
    spread = jnp.broadcast_to(rows_of_tile[:, :, None, :], (ntiles, MXU_WIDTH, per, bo)).reshape(
        ntiles, MXU_WIDTH, MXU_WIDTH)
    on_diag = (_iota((MXU_WIDTH, MXU_WIDTH), 0) // bi) == (_iota((MXU_WIDTH, MXU_WIDTH), 1) // bo)
    return jnp.where(on_diag, spread, 0.0).astype(BF16)


def _pad_cols(w, width=LANES):
    return jnp.pad(w, ((0, 0), (0, width - w.shape[1])))


def _history_tiles(buf):
    nseq, hist, ch = buf.shape
    return jnp.pad(buf, ((0, 0), (SUBLANES - hist, 0), (0, 0))).reshape(nseq * SUBLANES, ch)


def _block_plan(nseq, t):
    long_seq = t > SUBLANES
    return dict(
        lru=dict(nb=1 if long_seq else min(nseq, 64), tc=min(t, 256)),
        gla=dict(nb=1 if long_seq else min(nseq, 16), rows_blk=min(t, 512), chunk=min(t, GLA_CHUNK), sub=min(t, GLA_SUB)),
        pre=dict(nb=1 if long_seq else min(nseq, 16), tc=min(t, 128)),
        rec=dict(rows_blk=min(t, M_CHUNK), chunk=min(t, M_CHUNK), hb=2),
    )


_STATE_KEYS = ("lru_conv", "lru_h", "gla_s", "m_conv", "m_c", "m_n", "m_m")


class _Group:
    def __init__(self, x3, states):
        self.nseq, self.t, self.d = x3.shape
        self.x = x3.reshape(self.nseq * self.t, self.d)
        self.st = dict(zip(_STATE_KEYS, states))
        self.plan = _block_plan(self.nseq, self.t)
        self.outs = {k: [] for k in _STATE_KEYS}
        self.xn = self.gates = None

    def result(self):
        return (self.x.reshape(self.nseq, self.t, self.d),) + tuple(jnp.stack(self.outs[k]) for k in _STATE_KEYS)


def _trunks(groups,
            norm_mix_g, norm_ffn_g, norm_final_g,
            l0_w_in, l0_lru_conv_w, l0_lru_conv_b, l0_lru_wa, l0_lru_ba, l0_lru_wx, l0_lru_bx, l0_lru_lam,
            l0_gla_wa2, l0_gla_ba2, l0_gla_norm_g, l0_w_out,
            l1_w_up, l1_conv_w, l1_conv_b, l1_wq, l1_wk, l1_wv, l1_w_ig, l1_b_ig, l1_w_fg, l1_b_fg, l1_skip,
            l1_norm_g, l1_w_down,
            moe_w_rg, moe_b_rg, moe_w_re, moe_b_re, moe_w_gate, moe_w_up, moe_w_down):
    depth = norm_mix_g.shape[0]
    hist = CONV_W - 1
    for layer in range(depth):
        j = layer // 2
        wr = _pad_cols(jnp.concatenate([moe_w_re[layer], moe_w_rg[layer]], axis=1)).astype(BF16)
        br = _pad_cols(jnp.concatenate([moe_b_re[layer], moe_b_rg[layer]])[None, :])
        router = (norm_ffn_g[layer], wr, br)
        if layer % 2 == 0:
            w_in = l0_w_in[j].astype(BF16)
            w_out = l0_w_out[j].astype(BF16)
            wa_t, wx_t = _blockdiag_tiles(l0_lru_wa[j]), _blockdiag_tiles(l0_lru_wx[j])
            rank = l0_gla_wa2.shape[1]
            wa2 = jnp.pad(l0_gla_wa2[j], ((0, LANES - rank), (0, 0))).astype(BF16)
            for g in groups:
                w = g.st["lru_h"].shape[-1]
                _, _, nh, dk, dv = g.st["gla_s"].shape
                main = 2 * w + 2 * nh * dk + 2 * nh * dv
                proj, lg = fused_linear([g.x], w_in, n_out=main, gain=norm_mix_g[layer], name="linear_in",
                                        decay=(_pad_cols(w_in[:, main:main + rank]), wa2, l0_gla_ba2[j]))
                ya, h_last = lru_branch(
                    proj, _history_tiles(g.st["lru_conv"][j]), g.st["lru_h"][j], l0_lru_conv_w[j], l0_lru_conv_b[j],
                    wa_t, l0_lru_ba[j], wx_t, l0_lru_bx[j], l0_lru_lam[j], nseq=g.nseq, t=g.t, **g.plan["lru"])
                yb, s_new = gla_branch(
                    proj, lg, g.st["gla_s"][j], l0_gla_norm_g[j], nseq=g.nseq, t=g.t, **g.plan["gla"], q_off=2 * w,
                    k_off=2 * w + nh * dk, v_off=2 * w + 2 * nh * dk, g_off=2 * w + 2 * nh * dk + nh * dv)
                g.x, g.xn, g.gates = fused_linear([ya, yb], w_out, n_out=g.d, res=g.x, router=router, tm=512,
                                                  tn=g.d, name="linear_out")
                g.outs["lru_conv"].append(proj.reshape(g.nseq, g.t, main)[:, g.t - hist:, :w])
                g.outs["lru_h"].append(h_last)
                g.outs["gla_s"].append(s_new)
        else:
            w_up = l1_w_up[j].astype(BF16)
            w_down = l1_w_down[j].astype(BF16)
            wq_t, wk_t, wv_t = (_blockdiag_tiles(m[j]) for m in (l1_wq, l1_wk, l1_wv))
            wg = _pad_cols(jnp.concatenate([l1_w_ig[j], l1_w_fg[j]], axis=1)).astype(BF16)
            bg = _pad_cols(jnp.concatenate([l1_b_ig[j], l1_b_fg[j]])[None, :])
            fronts = []
            for g in groups:
                _, _, nh, dh = g.st["m_n"].shape
                up = fused_linear([g.x], w_up, n_out=2 * nh * dh, gain=norm_mix_g[layer], name="linear_up")
                q, k, v, xc, gates_m = mlstm_pre(
                    up, _history_tiles(g.st["m_conv"][j]), l1_conv_w[j], l1_conv_b[j], wq_t, wk_t, wv_t, wg, bg,
                    nseq=g.nseq, t=g.t, n_heads=nh, **g.plan["pre"])
                fronts.append((q, k, v, gates_m, xc, up))
                g.outs["m_conv"].append(up.reshape(g.nseq, g.t, 2 * nh * dh)[:, g.t - hist:, :nh * dh])
            recs = _mlstm_recurrences(groups, fronts, j, l1_skip[j], l1_norm_g[j])
            for g, (hout, c_new, n_new, m_new) in zip(groups, recs):
                g.x, g.xn, g.gates = fused_linear([hout], w_down, n_out=g.d, res=g.x, router=router, tm=256,
                                                  tn=g.d, name="linear_down")
                g.outs["m_c"].append(c_new)
                g.outs["m_n"].append(n_new)
                g.outs["m_m"].append(m_new)
        for g in groups:
            g.x = moe_experts(g.xn, g.gates, moe_w_gate, moe_w_up, moe_w_down, layer, g.x,
                              final_gain=norm_final_g if layer == depth - 1 else None)
    return [g.result() for g in groups]


def _mlstm_recurrences(groups, fronts, j, skip, norm_g):
    nh = groups[0].st["m_n"].shape[2]
    hb = groups[0].plan["rec"]["hb"]
    if len(groups) == 2:
        for il, i_s in ((0, 1), (1, 0)):
            gl, gs = groups[il], groups[i_s]
            fits = gl.st["m_c"] is None and gs.st["m_c"] is not None and gs.t == SUBLANES and gl.t > SUBLANES
            chunk = dual_chunk_rows(gl.nseq, gl.t, gs.nseq, nh, hb) if fits else None
            if chunk is not None:
                rl, rs = mlstm_recurrence_dual(
                    fronts[il], fronts[i_s], gs.st["m_c"][j], gs.st["m_n"][j], gs.st["m_m"][j], skip, norm_g,
                    nseq_l=gl.nseq, t_l=gl.t, nseq_s=gs.nseq, t_s=gs.t, nh=nh, hb=hb, chunk=chunk)
                return [rl, rs] if il == 0 else [rs, rl]
    return [mlstm_recurrence(*f, None if g.st["m_c"] is None else g.st["m_c"][j], g.st["m_n"][j], g.st["m_m"][j],
                             skip, norm_g, nseq=g.nseq, t=g.t, nh=nh, **g.plan["rec"])
            for g, f in zip(groups, fronts)]


def _trunk(x3, *states_and_weights):
    return _trunks([_Group(x3, states_and_weights[:7])], *states_and_weights[7:])[0]


def kernel(x_prompt, x_sample, state_lru_conv, state_lru_h, state_gla_S, state_mlstm_conv, state_mlstm_C,
           state_mlstm_n, state_mlstm_m, norm_mix_g, norm_ffn_g, norm_final_g, l0_w_in, l0_lru_conv_w,
           l0_lru_conv_b, l0_lru_wa, l0_lru_ba, l0_lru_wx, l0_lru_bx, l0_lru_lam, l0_gla_wa2, l0_gla_ba2,
           l0_gla_norm_g, l0_w_out, l1_w_up, l1_conv_w, l1_conv_b, l1_wq, l1_wk, l1_wv, l1_w_ig, l1_b_ig,
           l1_w_fg, l1_b_fg, l1_skip, l1_norm_g, l1_w_down, moe_w_rg, moe_b_rg, moe_w_re, moe_b_re, moe_w_gate,
           moe_w_up, moe_w_down):
    moe_w_gate, moe_w_up, moe_w_down = (w.astype(BF16) for w in (moe_w_gate, moe_w_up, moe_w_down))
    weights = (norm_mix_g, norm_ffn_g, norm_final_g, l0_w_in, l0_lru_conv_w, l0_lru_conv_b, l0_lru_wa, l0_lru_ba,
               l0_lru_wx, l0_lru_bx, l0_lru_lam, l0_gla_wa2, l0_gla_ba2, l0_gla_norm_g, l0_w_out, l1_w_up,
               l1_conv_w, l1_conv_b, l1_wq, l1_wk, l1_wv, l1_w_ig, l1_b_ig, l1_w_fg, l1_b_fg, l1_skip, l1_norm_g,
               l1_w_down, moe_w_rg, moe_b_rg, moe_w_re, moe_b_re, moe_w_gate, moe_w_up, moe_w_down)
    states = (state_lru_conv, state_lru_h, state_gla_S, state_mlstm_conv, state_mlstm_C, state_mlstm_n,
              state_mlstm_m)
    bp = x_prompt.shape[0]
    zero_states = tuple(None if s is state_mlstm_C else jnp.zeros((s.shape[0], bp) + s.shape[2:], s.dtype)
                        for s in states)
    prompt, sample = _trunks([_Group(x_prompt, zero_states), _Group(x_sample, states)], *weights)
    return (prompt[0], sample[0]) + prompt[1:] + sample[1:]
```

```python
import functools

import jax
import jax.numpy as jnp
from jax import lax
from jax.experimental import pallas as pl
from jax.experimental.pallas import tpu as pltpu

EPS = 1e-6
CONV_W = 4
LRU_C = 8.0
GLA_TAU = 16.0
GLA_CHUNK = 64
GLA_SUB = 8
M_CHUNK = 256
N_GROUPS = 4
E_PER_GROUP = 4
N_EXPERTS = N_GROUPS * E_PER_GROUP

V7X_VMEM_BYTES = 64 * 1024 * 1024
VMEM_LIMIT_BYTES = V7X_VMEM_BYTES - 8 * 1024 * 1024
SUBLANES = 8
LANES = 128
MXU_WIDTH = 256

F32 = jnp.float32
BF16 = jnp.bfloat16


def _params(*semantics):
    return pltpu.CompilerParams(dimension_semantics=semantics, vmem_limit_bytes=VMEM_LIMIT_BYTES)


def _dot(a, b):
    return jnp.dot(a.astype(BF16), b.astype(BF16), preferred_element_type=F32)


def _dot_nt(a, b):
    return lax.dot_general(a.astype(BF16), b.astype(BF16), (((1,), (1,)), ((), ())), preferred_element_type=F32)


def _dot_tn(a, b):
    return lax.dot_general(a.astype(BF16), b.astype(BF16), (((0,), (0,)), ((), ())), preferred_element_type=F32)


def _dot_split(m01, y):
    y_hi = y.astype(BF16)
    y_lo = (y - y_hi.astype(F32)).astype(BF16)
    m = m01.astype(BF16)
    return jnp.dot(m, y_hi, preferred_element_type=F32) + jnp.dot(m, y_lo, preferred_element_type=F32)


def _softplus(x):
    return jnp.maximum(x, 0.0) + jnp.log1p(jnp.exp(-jnp.abs(x)))


def _log_sigmoid(x):
    return -_softplus(-x)


def _rmsnorm_rows(x, g):
    return x * lax.rsqrt(jnp.mean(x * x, axis=-1, keepdims=True) + EPS) * g


def _iota(shape, dim):
    return lax.broadcasted_iota(jnp.int32, shape, dim)


def _lane_column(x, lane_index):
    return jnp.sum(jnp.where(_iota(x.shape, 1) == lane_index, x, 0.0), axis=1, keepdims=True)


def _row_tile(n, target):
    t = min(n, target)
    assert n % t == 0
    return t


def _route(xn, wr_ref, br_ref):
    logits = _dot(xn, wr_ref[...]) + br_ref[...]
    lane = _iota(logits.shape, 1)
    big = jnp.int32(LANES)
    is_g = jnp.logical_and(lane >= N_EXPERTS, lane < N_EXPERTS + N_GROUPS)
    gl = jnp.where(is_g, logits, -jnp.inf)
    gmax = jnp.max(gl, axis=1, keepdims=True)
    gsum = jnp.sum(jnp.where(is_g, jnp.exp(gl - gmax), 0.0), axis=1, keepdims=True)
    p_g = 1.0 / gsum
    g_idx = jnp.min(jnp.where(gl == gmax, lane, big), axis=1, keepdims=True) - N_EXPERTS
    sel = jnp.logical_and(lane < N_EXPERTS, (lane >> 2) == g_idx)
    el = jnp.where(sel, logits, -jnp.inf)
    emax = jnp.max(el, axis=1, keepdims=True)
    eexp = jnp.where(sel, jnp.exp(el - emax), 0.0)
    ep = eexp / jnp.sum(eexp, axis=1, keepdims=True)
    cand = jnp.where(sel, ep, -1.0)
    v1 = jnp.max(cand, axis=1, keepdims=True)
    idx1 = jnp.min(jnp.where(cand == v1, lane, big), axis=1, keepdims=True)
    cand2 = jnp.where(lane == idx1, -1.0, cand)
    v2 = jnp.max(cand2, axis=1, keepdims=True)
    idx2 = jnp.min(jnp.where(cand2 == v2, lane, big), axis=1, keepdims=True)
    tot = v1 + v2
    return jnp.where(lane == idx1, v1 / tot * p_g, 0.0) + jnp.where(lane == idx2, v2 / tot * p_g, 0.0)


def _linear_kernel(*refs, n_lhs, has_norm, has_res, has_router, has_decay):
    lhs_refs = refs[:n_lhs]
    pos = n_lhs
    g_ref = refs[pos] if has_norm else None
    pos += int(has_norm)
    w_ref = refs[pos]
    pos += 1
    res_ref = refs[pos] if has_res else None
    pos += int(has_res)
    if has_router:
        fg_ref, wr_ref, br_ref = refs[pos:pos + 3]
        pos += 3
    if has_decay:
        walr_ref, wa2_ref, ba2_ref = refs[pos:pos + 3]
        pos += 3
    o_ref = refs[pos]
    pos += 1
    if has_router:
        xn_ref, gates_ref = refs[pos:pos + 2]
        pos += 2
    if has_decay:
        lg_ref = refs[pos]
        pos += 1
    lhs_scr = refs[pos]

    @pl.when(pl.program_id(1) == 0)
    def _():
        off = 0
        for a_ref in lhs_refs:
            a = a_ref[...]
            if has_norm:
                a = _rmsnorm_rows(a, g_ref[...])
            lhs_scr[:, off:off + a.shape[1]] = a.astype(BF16)
            off += a.shape[1]
        if has_decay:
            alr = jnp.dot(lhs_scr[...], walr_ref[...], preferred_element_type=F32)
            lg_ref[...] = _log_sigmoid(_dot(alr, wa2_ref[...]) + ba2_ref[...]) * (1.0 / GLA_TAU)

    acc = jnp.dot(lhs_scr[...], w_ref[...].astype(BF16), preferred_element_type=F32)
    if has_res:
        acc = res_ref[...] + acc
    o_ref[...] = acc
    if has_router:
        xn = _rmsnorm_rows(acc, fg_ref[...])
        xn_ref[...] = xn.astype(BF16)
        gates_ref[...] = _route(xn, wr_ref, br_ref)


def fused_linear(lhs_list, w, *, n_out, gain=None, res=None, router=None, decay=None, tm=1024, tn=1024,
                 name="linear"):
    n = lhs_list[0].shape[0]
    ks = [a.shape[1] for a in lhs_list]
    ktot = sum(ks)
    assert w.shape[0] == ktot
    tm = _row_tile(n, tm)
    tn = _row_tile(n_out, tn)
    resident = tn == n_out == w.shape[1]
    assert router is None or resident
    in_specs = [pl.BlockSpec((tm, k), lambda i, j: (i, 0)) for k in ks]
    args = list(lhs_list)
    if gain is not None:
        in_specs.append(pl.BlockSpec((1, ktot), lambda i, j: (0, 0)))
        args.append(gain.reshape(1, ktot))
    in_specs.append(pl.BlockSpec((ktot, tn), lambda i, j: (0, j), pipeline_mode=pl.Buffered(1) if resident else None))
    args.append(w)
    if res is not None:
        in_specs.append(pl.BlockSpec((tm, tn), lambda i, j: (i, j)))
        args.append(res)
    out_specs = [pl.BlockSpec((tm, tn), lambda i, j: (i, j))]
    out_shape = [jax.ShapeDtypeStruct((n, n_out), F32)]
    if router is not None:
        fgain, wr, br = router
        in_specs += [pl.BlockSpec((1, n_out), lambda i, j: (0, 0)), pl.BlockSpec((n_out, LANES), lambda i, j: (0, 0)),
                     pl.BlockSpec((1, LANES), lambda i, j: (0, 0))]
        args += [fgain.reshape(1, n_out), wr, br]
        out_specs += [pl.BlockSpec((tm, n_out), lambda i, j: (i, 0)), pl.BlockSpec((tm, LANES), lambda i, j: (i, 0))]
        out_shape += [jax.ShapeDtypeStruct((n, n_out), BF16), jax.ShapeDtypeStruct((n, LANES), F32)]
    if decay is not None:
        w_alr, wa2, ba2 = decay
        hk = wa2.shape[1]
        in_specs += [pl.BlockSpec((ktot, LANES), lambda i, j: (0, 0)), pl.BlockSpec((LANES, hk), lambda i, j: (0, 0)),
                     pl.BlockSpec((1, hk), lambda i, j: (0, 0))]
        args += [w_alr, wa2, ba2.reshape(1, hk)]
        out_specs.append(pl.BlockSpec((tm, hk), lambda i, j: (i, 0)))
        out_shape.append(jax.ShapeDtypeStruct((n, hk), F32))
    kern = functools.partial(_linear_kernel, n_lhs=len(lhs_list), has_norm=gain is not None, has_res=res is not None,
                             has_router=router is not None, has_decay=decay is not None)
    outs = pl.pallas_call(
        kern,
        grid=(n // tm, n_out // tn),
        in_specs=in_specs,
        out_specs=out_specs,
        out_shape=out_shape,
        scratch_shapes=[pltpu.VMEM((tm, ktot), BF16)],
        compiler_params=_params("parallel", "arbitrary"),
        name=name,
    )(*args)
    return outs if len(outs) > 1 else outs[0]


def _causal_conv_short(x, prev, w_ref, b_ref):
    r = x.shape[0]
    pos = _iota((r, 1), 0) & (SUBLANES - 1)
    acc = b_ref[...] + x * w_ref[CONV_W - 1:CONV_W, :]
    for j in range(1, CONV_W):
        shifted = jnp.where(pos < j, pltpu.roll(prev, (j - SUBLANES) % r, 0), pltpu.roll(x, j, 0))
        acc = acc + shifted * w_ref[CONV_W - 1 - j:CONV_W - j, :]
    return acc


def _causal_conv_chunk(x, hist_scr, w_ref, b_ref):
    r = x.shape[0]
    acc = b_ref[...] + x * w_ref[CONV_W - 1:CONV_W, :]
    head = x[0:SUBLANES]
    hist = hist_scr[...]
    acc_head = b_ref[...] + head * w_ref[CONV_W - 1:CONV_W, :]
    pos = _iota((SUBLANES, 1), 0)
    for j in range(1, CONV_W):
        tap = w_ref[CONV_W - 1 - j:CONV_W - j, :]
        acc = acc + pltpu.roll(x, j, 0) * tap
        acc_head = acc_head + jnp.where(pos < j, pltpu.roll(hist, j, 0), pltpu.roll(head, j, 0)) * tap
    hist_scr[...] = x[r - SUBLANES:]
    return jnp.concatenate([acc_head, acc[SUBLANES:]], axis=0)


def _lru_kernel(xa_ref, ga_ref, prev_ref, h0_ref, cw_ref, cb_ref, wa_ref, ba_ref, wx_ref, bx_ref, lam_ref,
                ya_ref, hl_ref, prev_scr, h_scr, a_scr, b_scr, hs_scr, *, nb, tc):
    c = pl.program_id(1)

    @pl.when(c == 0)
    def _():
        if nb == 1:
            prev_scr[...] = prev_ref[...]
        h_scr[...] = h0_ref[...]

    x = xa_ref[...]
    if nb == 1:
        xc = _causal_conv_chunk(x, prev_scr, cw_ref, cb_ref)
    else:
        xc = _causal_conv_short(x, prev_ref[...], cw_ref, cb_ref)
    nblk = x.shape[1] // MXU_WIDTH
    xcb = xc.astype(BF16)
    r_parts, i_parts = [], []
    for t in range(nblk):
        sl = slice(t * MXU_WIDTH, (t + 1) * MXU_WIDTH)
        r_parts.append(jnp.dot(xcb[:, sl], wa_ref[t], preferred_element_type=F32))
        i_parts.append(jnp.dot(xcb[:, sl], wx_ref[t], preferred_element_type=F32))
    r_gate = jax.nn.sigmoid(jnp.concatenate(r_parts, axis=1) + ba_ref[...])
    i_gate = jax.nn.sigmoid(jnp.concatenate(i_parts, axis=1) + bx_ref[...])
    log_a = (-LRU_C) * r_gate * _softplus(-lam_ref[...])
    a = jnp.exp(log_a)
    a_scr[...] = a
    b_scr[...] = jnp.sqrt(-jnp.tanh(log_a) * (a * a + 1.0)) * (i_gate * xc)

    def seq_body(s, carry):
        def tile_body(tl, h):
            base = pl.multiple_of(s * tc + tl * SUBLANES, SUBLANES)
            for i in range(SUBLANES):
                h = a_scr[pl.ds(base + i, 1), :] * h + b_scr[pl.ds(base + i, 1), :]
                hs_scr[pl.ds(base + i, 1), :] = h
            return h

        h_scr[s] = lax.fori_loop(0, tc // SUBLANES, tile_body, h_scr[s])
        return carry

    lax.fori_loop(0, nb, seq_body, 0, unroll=min(nb, 4))
    ya_ref[...] = hs_scr[...] * jax.nn.gelu(ga_ref[...])

    @pl.when(c == pl.num_programs(1) - 1)
    def _():
        hl_ref[...] = h_scr[...]


def lru_branch(proj, prev8, h0, conv_w, conv_b, wa_t, ba, wx_t, bx, lam, *, nseq, t, nb, tc):
    n = proj.shape[0]
    w = h0.shape[1]
    nchunk = t // tc
    assert nseq % nb == 0 and t % tc == 0 and (nb == 1 or tc == t == SUBLANES)
    r = nb * tc
    nblk = w // MXU_WIDTH
    rows = lambda s, c: s * nchunk + c
    full2 = lambda s, c: (0, 0)
    full3 = lambda s, c: (0, 0, 0)
    kern = functools.partial(_lru_kernel, nb=nb, tc=tc)
    ya, hl = pl.pallas_call(
        kern,
        grid=(nseq // nb, nchunk),
        in_specs=[
            pl.BlockSpec((r, w), lambda s, c: (rows(s, c), 0)),
            pl.BlockSpec((r, w), lambda s, c: (rows(s, c), 1)),
            pl.BlockSpec((nb * SUBLANES, w), lambda s, c: (s, 0)),
            pl.BlockSpec((nb, 1, w), lambda s, c: (s, 0, 0)),
            pl.BlockSpec((CONV_W, w), full2),
            pl.BlockSpec((1, w), full2),
            pl.BlockSpec((nblk, MXU_WIDTH, MXU_WIDTH), full3),
            pl.BlockSpec((1, w), full2),
            pl.BlockSpec((nblk, MXU_WIDTH, MXU_WIDTH), full3),
            pl.BlockSpec((1, w), full2),
            pl.BlockSpec((1, w), full2),
        ],
        out_specs=[
            pl.BlockSpec((r, w), lambda s, c: (rows(s, c), 0)),
            pl.BlockSpec((nb, 1, w), lambda s, c: (s, 0, 0)),
        ],
        out_shape=[jax.ShapeDtypeStruct((n, w), F32), jax.ShapeDtypeStruct((nseq, 1, w), F32)],
        scratch_shapes=[
            pltpu.VMEM((SUBLANES, w), F32),
            pltpu.VMEM((nb, 1, w), F32),
            pltpu.VMEM((r, w), F32),
            pltpu.VMEM((r, w), F32),
            pltpu.VMEM((r, w), F32),
        ],
        compiler_params=_params("parallel", "arbitrary"),
        name="lru_branch",
    )(proj, proj, prev8, h0.reshape(nseq, 1, w), conv_w, conv_b.reshape(1, w), wa_t, ba.reshape(1, w), wx_t,
      bx.reshape(1, w), lam.reshape(1, w))
    return ya, hl.reshape(nseq, w)


def _gla_cumdecay(lg, run_rows):
    r = lg.shape[0]
    row = _iota((r, r), 0)
    col = _iota((r, r), 1)
    shift = run_rows.bit_length() - 1
    tri = jnp.logical_and(row >= col, (row >> shift) == (col >> shift))
    return _dot_split(tri.astype(F32), lg)


def _gla_near_att(qs, k, bc, sub):
    r = qs.shape[0]
    row = _iota((r, r), 0)
    col = _iota((r, r), 1)
    posr = _iota((r, 1), 0) & (sub - 1)
    att = jnp.zeros((r, r), F32)
    for d in range(sub):
        kd = k if d == 0 else pltpu.roll(k, d, 0)
        bcd = bc if d == 0 else pltpu.roll(bc, d, 0)
        valid = posr >= d
        prod = qs * kd * jnp.exp(jnp.where(valid, bc - bcd, 0.0))
        diag = jnp.sum(jnp.where(valid, prod, 0.0), axis=1, keepdims=True)
        att = att + jnp.where(col == row - d, diag, 0.0)
    return att


def _gla_far_att(qs, k, bc, sub):
    l, dk = qs.shape
    att = jnp.zeros((l, l), F32)
    for j in range(l // sub - 1):
        lo, hi = j * sub, (j + 1) * sub
        e_j = bc[hi - 1:hi, :]
        kp = k[lo:hi] * jnp.exp(e_j - bc[lo:hi])
        qp = qs[hi:] * jnp.exp(bc[hi:] - e_j)
        k_rows = [jnp.zeros((lo, dk), F32)] * (lo > 0) + [kp, jnp.zeros((l - hi, dk), F32)]
        att = att + _dot_nt(jnp.concatenate([jnp.zeros((hi, dk), F32), qp], axis=0), jnp.concatenate(k_rows, axis=0))
    return att


def _as_column(row_vec):
    d = row_vec.shape[1]
    eye = _iota((d, d), 0) == _iota((d, d), 1)
    return jnp.sum(jnp.where(eye, row_vec, 0.0), axis=1, keepdims=True)


def _gla_kernel(q_ref, k_ref, v_ref, g_ref, lg_ref, s0_ref, gn_ref, yb_ref, sn_ref, *scratch, nb, chunk, sub):
    c = pl.program_id(2)
    scale = q_ref.shape[1] ** -0.5

    def finish(o, g):
        on = o * lax.rsqrt(jnp.mean(o * o, axis=-1, keepdims=True) + EPS) * gn_ref[...]
        return on * jax.nn.silu(g)

    if nb == 1:
        (s_scr,) = scratch

        @pl.when(c == 0)
        def _():
            s_scr[...] = s0_ref[0, 0]

        def body(i, carry):
            rows = pl.ds(pl.multiple_of(i * chunk, chunk), chunk)
            qs = q_ref[rows, :] * scale
            k = k_ref[rows, :]
            v = v_ref[rows, :]
            bc = _gla_cumdecay(lg_ref[rows, :], chunk)
            s_state = s_scr[...]
            o = _dot(qs * jnp.exp(bc), s_state) + _dot(_gla_near_att(qs, k, bc, sub) + _gla_far_att(qs, k, bc, sub), v)
            bl = bc[chunk - 1:chunk, :]
            s_scr[...] = _as_column(jnp.exp(bl)) * s_state + _dot_tn(k * jnp.exp(bl - bc), v)
            yb_ref[rows, :] = finish(o, g_ref[rows, :])
            return carry

        lax.fori_loop(0, q_ref.shape[0] // chunk, body, 0, unroll=True)

        @pl.when(c == pl.num_programs(2) - 1)
        def _():
            sn_ref[0, 0] = s_scr[...]
    else:
        qe_scr, kd_scr, eb_scr, o_scr = scratch
        r = q_ref.shape[0]
        qs = q_ref[...] * scale
        k = k_ref[...]
        bc = _gla_cumdecay(lg_ref[...], chunk)
        o_scr[...] = _dot(_gla_near_att(qs, k, bc, sub), v_ref[...])
        qe_scr[...] = qs * jnp.exp(bc)
        row = _iota((r, r), 0)
        col = _iota((r, r), 1)
        last = (col == (row | (chunk - 1))).astype(F32)
        bl = _dot_split(last, bc)
        kd_scr[...] = k * jnp.exp(bl - bc)
        eb_scr[...] = jnp.exp(bl)

        def body(j, carry):
            rows = pl.ds(pl.multiple_of(j * chunk, chunk), chunk)
            s_state = s0_ref[j, 0]
            o_scr[rows, :] += _dot(qe_scr[rows, :], s_state)
            decay = _as_column(eb_scr[pl.ds(pl.multiple_of(j * chunk, chunk), 1), :])
            sn_ref[j, 0] = decay * s_state + _dot_tn(kd_scr[rows, :], v_ref[rows, :])
            return carry

        lax.fori_loop(0, nb, body, 0, unroll=min(nb, 8))
        yb_ref[...] = finish(o_scr[...], g_ref[...])


def gla_branch(proj, lg, s0, gnorm, *, nseq, t, nb, rows_blk, chunk, sub, q_off, k_off, v_off, g_off):
    n = proj.shape[0]
    _, nh, dk, dv = s0.shape
    ntb = t // rows_blk if nb == 1 else 1
    r = rows_blk if nb == 1 else nb * t
    assert (nb == 1 and t % rows_blk == 0 and rows_blk % chunk == 0) or (chunk == sub == t and nseq % nb == 0)
    rows = lambda s, h, c: s * ntb + c
    kern = functools.partial(_gla_kernel, nb=nb, chunk=chunk, sub=sub)
    if nb == 1:
        scratch = [pltpu.VMEM((dk, dv), F32)]
    else:
        scratch = [pltpu.VMEM((r, dk), F32), pltpu.VMEM((r, dk), F32), pltpu.VMEM((r, dk), F32), pltpu.VMEM((r, dv), F32)]
    yb, sn = pl.pallas_call(
        kern,
        grid=(nseq // nb, nh, ntb),
        in_specs=[
            pl.BlockSpec((r, dk), lambda s, h, c: (rows(s, h, c), q_off // dk + h)),
            pl.BlockSpec((r, dk), lambda s, h, c: (rows(s, h, c), k_off // dk + h)),
            pl.BlockSpec((r, dv), lambda s, h, c: (rows(s, h, c), v_off // dv + h)),
            pl.BlockSpec((r, dv), lambda s, h, c: (rows(s, h, c), g_off // dv + h)),
            pl.BlockSpec((r, dk), lambda s, h, c: (rows(s, h, c), h)),
            pl.BlockSpec((nb, 1, dk, dv), lambda s, h, c: (s, h, 0, 0)),
            pl.BlockSpec((1, dv), lambda s, h, c: (0, h)),
        ],
        out_specs=[
            pl.BlockSpec((r, dv), lambda s, h, c: (rows(s, h, c), h)),
            pl.BlockSpec((nb, 1, dk, dv), lambda s, h, c: (s, h, 0, 0)),
        ],
        out_shape=[jax.ShapeDtypeStruct((n, nh * dv), F32), jax.ShapeDtypeStruct(s0.shape, F32)],
        scratch_shapes=scratch,
        compiler_params=_params("parallel", "parallel", "arbitrary"),
        name="gla_branch",
    )(proj, proj, proj, proj, lg, s0, gnorm.reshape(1, nh * dv))
    return yb, sn


def _mlstm_pre_kernel(xm_ref, prev_ref, cw_ref, cb_ref, wq_ref, wk_ref, wv_ref, wg_ref, bg_ref,
                      q_ref, k_ref, v_ref, xc_ref, gate_ref, prev_scr, *, nb, tc, n_heads):
    c = pl.program_id(1)

    x = xm_ref[...]
    if nb == 1:
        @pl.when(c == 0)
        def _():
            prev_scr[...] = prev_ref[...]

        xc = jax.nn.silu(_causal_conv_chunk(x, prev_scr, cw_ref, cb_ref))
    else:
        xc = jax.nn.silu(_causal_conv_short(x, prev_ref[...], cw_ref, cb_ref))
    xc_ref[...] = xc
    di = x.shape[1]
    xcb = xc.astype(BF16)
    xb = x.astype(BF16)
    for t in range(di // MXU_WIDTH):
        sl = slice(t * MXU_WIDTH, (t + 1) * MXU_WIDTH)
        q_ref[:, sl] = jnp.dot(xcb[:, sl], wq_ref[t], preferred_element_type=F32)
        k_ref[:, sl] = jnp.dot(xcb[:, sl], wk_ref[t], preferred_element_type=F32)
        v_ref[:, sl] = jnp.dot(xb[:, sl], wv_ref[t], preferred_element_type=F32)
    pre = (_dot(q_ref[...], wg_ref[0:di, :]) + _dot(k_ref[...], wg_ref[di:2 * di, :])
           + _dot(v_ref[...], wg_ref[2 * di:3 * di, :]) + bg_ref[...])
    gate_ref[...] = jnp.where(_iota(pre.shape, 1) < n_heads, pre, _log_sigmoid(pre))


def mlstm_pre(up, prev8, conv_w, conv_b, wq_t, wk_t, wv_t, wg, bg, *, nseq, t, nb, tc, n_heads):
    n = up.shape[0]
    di = conv_w.shape[1]
    nchunk = t // tc
    assert nseq % nb == 0 and t % tc == 0 and (nb == 1 or tc == t == SUBLANES)
    r = nb * tc
    nblk = di // MXU_WIDTH
    rows = lambda s, c: (s * nchunk + c, 0)
    full2 = lambda s, c: (0, 0)
    full3 = lambda s, c: (0, 0, 0)
    wide = jax.ShapeDtypeStruct((n, di), F32)
    kern = functools.partial(_mlstm_pre_kernel, nb=nb, tc=tc, n_heads=n_heads)
    return pl.pallas_call(
        kern,
        grid=(nseq // nb, nchunk),
        in_specs=[
            pl.BlockSpec((r, di), rows),
            pl.BlockSpec((nb * SUBLANES, di), lambda s, c: (s, 0)),
            pl.BlockSpec((CONV_W, di), full2),
            pl.BlockSpec((1, di), full2),
            pl.BlockSpec((nblk, MXU_WIDTH, MXU_WIDTH), full3),
            pl.BlockSpec((nblk, MXU_WIDTH, MXU_WIDTH), full3),
            pl.BlockSpec((nblk, MXU_WIDTH, MXU_WIDTH), full3),
            pl.BlockSpec((3 * di, LANES), full2),
            pl.BlockSpec((1, LANES), full2),
        ],
        out_specs=[pl.BlockSpec((r, di), rows)] * 4 + [pl.BlockSpec((r, LANES), rows)],
        out_shape=[wide, wide, wide, wide, jax.ShapeDtypeStruct((n, LANES), F32)],
        scratch_shapes=[pltpu.VMEM((SUBLANES, di), F32)],
        compiler_params=_params("parallel", "arbitrary"),
        name="mlstm_pre",
    )(up, prev8, conv_w, conv_b.reshape(1, di), wq_t, wk_t, wv_t, wg, bg)


def _mlstm_chunk(q, k, v, igc, lfc, xc, z, skip, ng, m_prev, n_prev, c_load, c_store):
    l, dh = q.shape
    row = _iota((l, l), 0)
    col = _iota((l, l), 1)
    tril = row >= col
    f_b = jnp.broadcast_to(lfc, (l, l))
    i_b = jnp.broadcast_to(igc, (l, l))
    f_col = _dot_split(tril.astype(F32), f_b)
    row_term = _dot_split(jnp.ones((l, l), F32),
                          jnp.where(row == col, i_b, 0.0) - jnp.where(row <= col, f_b, 0.0))
    dm = jnp.where(tril, f_col + row_term, -jnp.inf)
    fcum = f_col[:, 0:1]
    prev = m_prev + fcum
    mt = jnp.maximum(prev, jnp.max(dm, axis=1, keepdims=True))
    wprev = jnp.exp(prev - mt)
    smat = _dot_nt(q, k) * jnp.exp(dm - mt)
    den = wprev * jnp.sum(q * n_prev, axis=1, keepdims=True) + jnp.sum(smat, axis=1, keepdims=True)
    inv = 1.0 / jnp.maximum(jnp.abs(den), jnp.exp(-mt))
    f_last = fcum[l - 1:l, :]
    m_last = mt[l - 1:l, :]
    w_c = jnp.exp(m_prev + f_last - m_last)
    kw = k * jnp.exp(f_last - fcum + igc - m_last)
    qb = q.astype(BF16)
    sb = smat.astype(BF16)
    kwb = kw.astype(BF16)
    parts = []
    for t in range(dh // MXU_WIDTH):
        cols = slice(t * MXU_WIDTH, (t + 1) * MXU_WIDTH)
        c_blk = c_load(cols)
        vb = v[:, cols].astype(BF16)
        num = (wprev * jnp.dot(qb, c_blk.astype(BF16), preferred_element_type=F32)
               + jnp.dot(sb, vb, preferred_element_type=F32))
        parts.append(num * inv)
        c_store(cols, w_c * c_blk + _dot_tn(kwb, vb))
    n_new = w_c * n_prev + jnp.sum(kw, axis=0, keepdims=True)
    hh = jnp.concatenate(parts, axis=1)
    hc = hh - jnp.mean(hh, axis=-1, keepdims=True)
    hn = hc * lax.rsqrt(jnp.mean(hc * hc, axis=-1, keepdims=True) + EPS) * ng
    return (hn + skip * xc) * jax.nn.silu(z), n_new, m_last


def _mlstm_seq_kernel(*refs, chunk, n_heads, zero_state):
    q_ref, k_ref, v_ref, gate_ref, xc_ref, z_ref = refs[:6]
    pos = 6
    if not zero_state:
        c0_ref, n0_ref, m0_ref = refs[pos:pos + 3]
        pos += 3
    skip_ref, ng_ref, out_ref, cn_ref, nn_ref, mn_ref, c_scr, n_scr, m_scr = refs[pos:pos + 9]
    head = pl.program_id(1)
    c = pl.program_id(2)

    @pl.when(c == 0)
    def _():
        if zero_state:
            c_scr[...] = jnp.zeros_like(c_scr)
            n_scr[...] = jnp.zeros_like(n_scr)
            m_scr[...] = jnp.zeros_like(m_scr)
        else:
            c_scr[...] = c0_ref[0, 0]
            n_scr[...] = n0_ref[0]
            m_scr[...] = m0_ref[0]

    kscale = q_ref.shape[1] ** -0.5

    def c_store(cols, value):
        c_scr[:, cols] = value

    def body(i, carry):
        rows = pl.ds(pl.multiple_of(i * chunk, chunk), chunk)
        gates = gate_ref[rows, :]
        out, n_new, m_last = _mlstm_chunk(
            q_ref[rows, :], k_ref[rows, :] * kscale, v_ref[rows, :], _lane_column(gates, head),
            _lane_column(gates, head + n_heads), xc_ref[rows, :], z_ref[rows, :], skip_ref[...], ng_ref[...],
            m_scr[:, 0:1], n_scr[...], lambda cols: c_scr[:, cols], c_store)
        out_ref[rows, :] = out
        n_scr[...] = n_new
        m_scr[...] = jnp.broadcast_to(m_last, m_scr.shape)
        return carry

    lax.fori_loop(0, q_ref.shape[0] // chunk, body, 0)

    @pl.when(c == pl.num_programs(2) - 1)
    def _():
        cn_ref[0, 0] = c_scr[...]
        nn_ref[0] = n_scr[...]
        mn_ref[0] = m_scr[...]


def _mlstm_step_kernel(q_ref, k_ref, v_ref, gate_ref, xc_ref, z_ref, c0_ref, n0_ref, m0_ref, skip_ref, ng_ref,
                       out_ref, cn_ref, nn_ref, mn_ref, *, hb, n_heads, head_block=None):
    hblk = pl.program_id(1) if head_block is None else head_block
    dh = c0_ref.shape[2]
    kscale = dh ** -0.5
    gates = gate_ref[...]
    for hh in range(hb):
        cols_h = slice(hh * dh, (hh + 1) * dh)
        head = hblk * hb + hh

        def c_store(cols, value, hh=hh):
            cn_ref[0, hh, :, cols] = value

        out, n_new, m_last = _mlstm_chunk(
            q_ref[:, cols_h], k_ref[:, cols_h] * kscale, v_ref[:, cols_h], _lane_column(gates, head),
            _lane_column(gates, head + n_heads), xc_ref[:, cols_h], z_ref[:, cols_h], skip_ref[:, cols_h],
            ng_ref[:, cols_h], m0_ref[hh][:, 0:1], n0_ref[hh], lambda cols, hh=hh: c0_ref[0, hh, :, cols], c_store)
        out_ref[:, cols_h] = out
        nn_ref[hh] = n_new
        mn_ref[hh] = jnp.broadcast_to(m_last, (1, LANES))


def mlstm_recurrence(q, k, v, gates, xc, up, c0, n0, m0, skip, norm_g, *, nseq, t, nh, rows_blk, chunk, hb):
    n, di = q.shape
    dh = di // nh
    zero_state = c0 is None
    out_shape = [
        jax.ShapeDtypeStruct((n, di), F32),
        jax.ShapeDtypeStruct((nseq, nh, dh, dh), F32),
        jax.ShapeDtypeStruct((nseq * nh, 1, dh), F32),
        jax.ShapeDtypeStruct((nseq * nh, 1, LANES), F32),
    ]
    state_args = []
    if not zero_state:
        state_args = [c0, n0.reshape(nseq * nh, 1, dh),
                      jnp.broadcast_to(m0.reshape(nseq * nh, 1, 1), (nseq * nh, 1, LANES))]
    if t == chunk and not zero_state:
        assert nh % hb == 0
        nhb = nh // hb
        wide = lambda s, h: (s, h)
        per_head = lambda s, h: (s * nhb + h, 0, 0)
        mat = lambda s, h: (s, h, 0, 0)
        out, cn, nn, mn = pl.pallas_call(
            functools.partial(_mlstm_step_kernel, hb=hb, n_heads=nh),
            grid=(nseq, nhb),
            in_specs=[pl.BlockSpec((t, hb * dh), wide)] * 3
            + [pl.BlockSpec((t, LANES), lambda s, h: (s, 0)), pl.BlockSpec((t, hb * dh), wide),
               pl.BlockSpec((t, hb * dh), lambda s, h: (s, nhb + h)), pl.BlockSpec((1, hb, dh, dh), mat),
               pl.BlockSpec((hb, 1, dh), per_head), pl.BlockSpec((hb, 1, LANES), per_head),
               pl.BlockSpec((1, hb * dh), lambda s, h: (0, h)), pl.BlockSpec((1, hb * dh), lambda s, h: (0, h))],
            out_specs=[pl.BlockSpec((t, hb * dh), wide), pl.BlockSpec((1, hb, dh, dh), mat),
                       pl.BlockSpec((hb, 1, dh), per_head), pl.BlockSpec((hb, 1, LANES), per_head)],
            out_shape=out_shape,
            compiler_params=_params("parallel", "arbitrary"),
            name="mlstm_step",
        )(q, k, v, gates, xc, up, *state_args, skip.reshape(1, di), norm_g.reshape(1, di))
    else:
        ntb = t // rows_blk
        assert t % rows_blk == 0 and rows_blk % chunk == 0
        rows = lambda s, h, c: (s * ntb + c, h)
        per_head = lambda s, h, c: (s * nh + h, 0, 0)
        mat = lambda s, h, c: (s, h, 0, 0)
        state_specs = [] if zero_state else [pl.BlockSpec((1, 1, dh, dh), mat), pl.BlockSpec((1, 1, dh), per_head),
                                             pl.BlockSpec((1, 1, LANES), per_head)]
        out, cn, nn, mn = pl.pallas_call(
            functools.partial(_mlstm_seq_kernel, chunk=chunk, n_heads=nh, zero_state=zero_state),
            grid=(nseq, nh, ntb),
            in_specs=[pl.BlockSpec((rows_blk, dh), rows)] * 3
            + [pl.BlockSpec((rows_blk, LANES), lambda s, h, c: (s * ntb + c, 0)), pl.BlockSpec((rows_blk, dh), rows),
               pl.BlockSpec((rows_blk, dh), lambda s, h, c: (s * ntb + c, nh + h))]
            + state_specs
            + [pl.BlockSpec((1, dh), lambda s, h, c: (0, h)), pl.BlockSpec((1, dh), lambda s, h, c: (0, h))],
            out_specs=[pl.BlockSpec((rows_blk, dh), rows), pl.BlockSpec((1, 1, dh, dh), mat),
                       pl.BlockSpec((1, 1, dh), per_head), pl.BlockSpec((1, 1, LANES), per_head)],
            out_shape=out_shape,
            scratch_shapes=[pltpu.VMEM((dh, dh), F32), pltpu.VMEM((1, dh), F32), pltpu.VMEM((1, LANES), F32)],
            compiler_params=_params("parallel", "parallel", "arbitrary"),
            name="mlstm_seq",
        )(q, k, v, gates, xc, up, *state_args, skip.reshape(1, di), norm_g.reshape(1, di))
    return out, cn, nn.reshape(nseq, nh, dh), mn[:, 0, 0].reshape(nseq, nh)


def _mlstm_dual_kernel(ql_ref, kl_ref, vl_ref, gl_ref, xcl_ref, zl_ref, skipl_ref, ngl_ref,
                       qs_ref, ks_ref, vs_ref, gs_ref, xcs_ref, zs_ref, c0_ref, n0_ref, m0_ref, skips_ref, ngs_ref,
                       outl_ref, cnl_ref, nnl_ref, mnl_ref, outs_ref, cns_ref, nns_ref, mns_ref, n_scr, m_scr,
                       *, hb, n_heads, chunks_per_seq):
    g = pl.program_id(0)
    c = g % chunks_per_seq
    head_l = (g // chunks_per_seq) % n_heads
    dh = ql_ref.shape[1]
    kscale = dh ** -0.5

    @pl.when(c == 0)
    def _():
        cnl_ref[...] = jnp.zeros_like(cnl_ref)
        n_scr[...] = jnp.zeros_like(n_scr)
        m_scr[...] = jnp.zeros_like(m_scr)

    def cl_store(cols, value):
        cnl_ref[0, 0, :, cols] = value

    gl = gl_ref[...]
    out, n_new, m_last = _mlstm_chunk(
        ql_ref[...], kl_ref[...] * kscale, vl_ref[...], _lane_column(gl, head_l), _lane_column(gl, head_l + n_heads),
        xcl_ref[...], zl_ref[...], skipl_ref[...], ngl_ref[...], m_scr[:, 0:1], n_scr[...],
        lambda cols: cnl_ref[0, 0, :, cols], cl_store)
    outl_ref[...] = out
    n_scr[...] = n_new
    m_scr[...] = jnp.broadcast_to(m_last, m_scr.shape)

    @pl.when(c == chunks_per_seq - 1)
    def _():
        nnl_ref[0] = n_scr[...]
        mnl_ref[0] = m_scr[...]

    _mlstm_step_kernel(qs_ref, ks_ref, vs_ref, gs_ref, xcs_ref, zs_ref, c0_ref, n0_ref, m0_ref, skips_ref, ngs_ref,
                       outs_ref, cns_ref, nns_ref, mns_ref, hb=hb, n_heads=n_heads, head_block=g % (n_heads // hb))


def dual_chunk_rows(nseq_l, t_l, nseq_s, nh, hb):
    steps = nseq_s * (nh // hb)
    total = nseq_l * t_l * nh
    if nh % hb or total % steps:
        return None
    chunk = total // steps
    ok = chunk % SUBLANES == 0 and t_l % chunk == 0 and 64 <= chunk <= M_CHUNK
    return chunk if ok else None


def mlstm_recurrence_dual(long_in, short_in, c0, n0, m0, skip, norm_g, *, nseq_l, t_l, nseq_s, t_s, nh, hb, chunk):
    di = long_in[0].shape[1]
    dh = di // nh
    nhb = nh // hb
    cps = t_l // chunk
    steps = nseq_s * nhb
    assert steps == nseq_l * nh * cps

    def rows_l(g):
        return (g // (nh * cps)) * cps + g % cps

    head_l = lambda g: (g // cps) % nh
    tile_l = lambda g: (rows_l(g), head_l(g))
    per_head_l = lambda g: (g // cps, 0, 0)
    tile_s = lambda g: (g // nhb, g % nhb)
    per_head_s = lambda g: (g, 0, 0)
    mat_s = lambda g: (g // nhb, g % nhb, 0, 0)
    n_l, n_s = long_in[0].shape[0], short_in[0].shape[0]
    in_specs = (
        [pl.BlockSpec((chunk, dh), tile_l)] * 3
        + [pl.BlockSpec((chunk, LANES), lambda g: (rows_l(g), 0)), pl.BlockSpec((chunk, dh), tile_l),
           pl.BlockSpec((chunk, dh), lambda g: (rows_l(g), nh + head_l(g))),
           pl.BlockSpec((1, dh), lambda g: (0, head_l(g))), pl.BlockSpec((1, dh), lambda g: (0, head_l(g)))]
        + [pl.BlockSpec((t_s, hb * dh), tile_s)] * 3
        + [pl.BlockSpec((t_s, LANES), lambda g: (g // nhb, 0)), pl.BlockSpec((t_s, hb * dh), tile_s),
           pl.BlockSpec((t_s, hb * dh), lambda g: (g // nhb, nhb + g % nhb)), pl.BlockSpec((1, hb, dh, dh), mat_s),
           pl.BlockSpec((hb, 1, dh), per_head_s), pl.BlockSpec((hb, 1, LANES), per_head_s),
           pl.BlockSpec((1, hb * dh), lambda g: (0, g % nhb)), pl.BlockSpec((1, hb * dh), lambda g: (0, g % nhb))])
    out_specs = [
        pl.BlockSpec((chunk, dh), tile_l), pl.BlockSpec((1, 1, dh, dh), lambda g: (g // (nh * cps), head_l(g), 0, 0)),
        pl.BlockSpec((1, 1, dh), per_head_l), pl.BlockSpec((1, 1, LANES), per_head_l),
        pl.BlockSpec((t_s, hb * dh), tile_s), pl.BlockSpec((1, hb, dh, dh), mat_s),
        pl.BlockSpec((hb, 1, dh), per_head_s), pl.BlockSpec((hb, 1, LANES), per_head_s)]
    out_shape = [
        jax.ShapeDtypeStruct((n_l, di), F32), jax.ShapeDtypeStruct((nseq_l, nh, dh, dh), F32),
        jax.ShapeDtypeStruct((nseq_l * nh, 1, dh), F32), jax.ShapeDtypeStruct((nseq_l * nh, 1, LANES), F32),
        jax.ShapeDtypeStruct((n_s, di), F32), jax.ShapeDtypeStruct((nseq_s, nh, dh, dh), F32),
        jax.ShapeDtypeStruct((nseq_s * nh, 1, dh), F32), jax.ShapeDtypeStruct((nseq_s * nh, 1, LANES), F32)]
    skip2, ng2 = skip.reshape(1, di), norm_g.reshape(1, di)
    res = pl.pallas_call(
        functools.partial(_mlstm_dual_kernel, hb=hb, n_heads=nh, chunks_per_seq=cps),
        grid=(steps,),
        in_specs=in_specs,
        out_specs=out_specs,
        out_shape=out_shape,
        scratch_shapes=[pltpu.VMEM((1, dh), F32), pltpu.VMEM((1, LANES), F32)],
        compiler_params=_params("arbitrary"),
        name="mlstm_dual",
    )(*long_in, skip2, ng2, *short_in, c0, n0.reshape(nseq_s * nh, 1, dh),
      jnp.broadcast_to(m0.reshape(nseq_s * nh, 1, 1), (nseq_s * nh, 1, LANES)), skip2, ng2)
    unpack = lambda o, cn, nn, mn, nseq: (o, cn, nn.reshape(nseq, nh, dh), mn[:, 0, 0].reshape(nseq, nh))
    return unpack(*res[:4], nseq_l), unpack(*res[4:], nseq_s)


def _moe_expert_kernel(*refs, has_final, eb):
    xn_ref, gates_ref, wg_ref, wu_ref, wd_ref, x_ref = refs[:6]
    fg_ref = refs[6] if has_final else None
    o_ref = refs[6 + int(has_final)]
    e = pl.program_id(1)

    @pl.when(e == 0)
    def _():
        o_ref[...] = x_ref[...]

    xn = xn_ref[...]
    gates = gates_ref[...]
    update = None
    for k in range(eb):
        hg = jnp.dot(xn, wg_ref[0, k].astype(BF16), preferred_element_type=F32)
        hu = jnp.dot(xn, wu_ref[0, k].astype(BF16), preferred_element_type=F32)
        h = jax.nn.silu(hg) * hu * _lane_column(gates, e * eb + k)
        part = _dot(h, wd_ref[0, k])
        update = part if update is None else update + part
    o_ref[...] += update

    if has_final:
        @pl.when(e == pl.num_programs(1) - 1)
        def _():
            o_ref[...] = _rmsnorm_rows(o_ref[...], fg_ref[...])


def moe_experts(xn, gates, w_gate, w_up, w_down, layer, x, final_gain=None, tm=1024, eb=1):
    n, d = x.shape
    _, ne, _, f = w_gate.shape
    tm = _row_tile(n, tm)
    assert ne % eb == 0
    in_specs = [
        pl.BlockSpec((tm, d), lambda i, e: (i, 0)),
        pl.BlockSpec((tm, LANES), lambda i, e: (i, 0)),
        pl.BlockSpec((1, eb, d, f), lambda i, e: (layer, e, 0, 0)),
        pl.BlockSpec((1, eb, d, f), lambda i, e: (layer, e, 0, 0)),
        pl.BlockSpec((1, eb, f, d), lambda i, e: (layer, e, 0, 0)),
        pl.BlockSpec((tm, d), lambda i, e: (i, 0), pipeline_mode=pl.Buffered(1)),
    ]
    args = [xn, gates, w_gate, w_up, w_down, x]
    if final_gain is not None:
        in_specs.append(pl.BlockSpec((1, d), lambda i, e: (0, 0)))
        args.append(final_gain.reshape(1, d))
    return pl.pallas_call(
        functools.partial(_moe_expert_kernel, has_final=final_gain is not None, eb=eb),
        grid=(n // tm, ne // eb),
        in_specs=in_specs,
        out_specs=pl.BlockSpec((tm, d), lambda i, e: (i, 0)),
        out_shape=jax.ShapeDtypeStruct((n, d), F32),
        compiler_params=_params("parallel", "arbitrary"),
        name="moe_experts",
    )(*args)


def _blockdiag_tiles(w):
    nblocks, bi, bo = w.shape
    per = MXU_WIDTH // bi
    ntiles = nblocks // per
    rows_of_tile = w.reshape(ntiles, MXU_WIDTH, bo)
    spread = jnp.broadcast_to(rows_of_tile[:, :, None, :], (ntiles, MXU_WIDTH, per, bo)).reshape(
        ntiles, MXU_WIDTH, MXU_WIDTH)
    on_diag = (_iota((MXU_WIDTH, MXU_WIDTH), 0) // bi) == (_iota((MXU_WIDTH, MXU_WIDTH), 1) // bo)
    return jnp.where(on_diag, spread, 0.0).astype(BF16)


def _pad_cols(w, width=LANES):
    return jnp.pad(w, ((0, 0), (0, width - w.shape[1])))


def _history_tiles(buf):
    nseq, hist, ch = buf.shape
    return jnp.pad(buf, ((0, 0), (SUBLANES - hist, 0), (0, 0))).reshape(nseq * SUBLANES, ch)


def _block_plan(nseq, t):
    long_seq = t > SUBLANES
    return dict(
        lru=dict(nb=1 if long_seq else min(nseq, 64), tc=min(t, 256)),
        gla=dict(nb=1 if long_seq else min(nseq, 16), rows_blk=min(t, 512), chunk=min(t, GLA_CHUNK), sub=min(t, GLA_SUB)),
        pre=dict(nb=1 if long_seq else min(nseq, 16), tc=min(t, 128)),
        rec=dict(rows_blk=min(t, M_CHUNK), chunk=min(t, M_CHUNK), hb=2),
    )


_STATE_KEYS = ("lru_conv", "lru_h", "gla_s", "m_conv", "m_c", "m_n", "m_m")


class _Group:
    def __init__(self, x3, states):
        self.nseq, self.t, self.d = x3.shape
        self.x = x3.reshape(self.nseq * self.t, self.d)
        self.st = dict(zip(_STATE_KEYS, states))
        self.plan = _block_plan(self.nseq, self.t)
        self.outs = {k: [] for k in _STATE_KEYS}
        self.xn = self.gates = None

    def result(self):
        return (self.x.reshape(self.nseq, self.t, self.d),) + tuple(jnp.stack(self.outs[k]) for k in _STATE_KEYS)


def _trunks(groups,
            norm_mix_g, norm_ffn_g, norm_final_g,
            l0_w_in, l0_lru_conv_w, l0_lru_conv_b, l0_lru_wa, l0_lru_ba, l0_lru_wx, l0_lru_bx, l0_lru_lam,
            l0_gla_wa2, l0_gla_ba2, l0_gla_norm_g, l0_w_out,
            l1_w_up, l1_conv_w, l1_conv_b, l1_wq, l1_wk, l1_wv, l1_w_ig, l1_b_ig, l1_w_fg, l1_b_fg, l1_skip,
            l1_norm_g, l1_w_down,
            moe_w_rg, moe_b_rg, moe_w_re, moe_b_re, moe_w_gate, moe_w_up, moe_w_down):
    depth = norm_mix_g.shape[0]
    hist = CONV_W - 1
    for layer in range(depth):
        j = layer // 2
        wr = _pad_cols(jnp.concatenate([moe_w_re[layer], moe_w_rg[layer]], axis=1)).astype(BF16)
        br = _pad_cols(jnp.concatenate([moe_b_re[layer], moe_b_rg[layer]])[None, :])
        router = (norm_ffn_g[layer], wr, br)
        if layer % 2 == 0:
            w_in = l0_w_in[j].astype(BF16)
            w_out = l0_w_out[j].astype(BF16)
            wa_t, wx_t = _blockdiag_tiles(l0_lru_wa[j]), _blockdiag_tiles(l0_lru_wx[j])
            rank = l0_gla_wa2.shape[1]
            wa2 = jnp.pad(l0_gla_wa2[j], ((0, LANES - rank), (0, 0))).astype(BF16)
            for g in groups:
                w = g.st["lru_h"].shape[-1]
                _, _, nh, dk, dv = g.st["gla_s"].shape
                main = 2 * w + 2 * nh * dk + 2 * nh * dv
                proj, lg = fused_linear([g.x], w_in, n_out=main, gain=norm_mix_g[layer], name="linear_in",
                                        decay=(_pad_cols(w_in[:, main:main + rank]), wa2, l0_gla_ba2[j]))
                ya, h_last = lru_branch(
                    proj, _history_tiles(g.st["lru_conv"][j]), g.st["lru_h"][j], l0_lru_conv_w[j], l0_lru_conv_b[j],
                    wa_t, l0_lru_ba[j], wx_t, l0_lru_bx[j], l0_lru_lam[j], nseq=g.nseq, t=g.t, **g.plan["lru"])
                yb, s_new = gla_branch(
                    proj, lg, g.st["gla_s"][j], l0_gla_norm_g[j], nseq=g.nseq, t=g.t, **g.plan["gla"], q_off=2 * w,
                    k_off=2 * w + nh * dk, v_off=2 * w + 2 * nh * dk, g_off=2 * w + 2 * nh * dk + nh * dv)
                g.x, g.xn, g.gates = fused_linear([ya, yb], w_out, n_out=g.d, res=g.x, router=router, tm=512,
                                                  tn=g.d, name="linear_out")
                g.outs["lru_conv"].append(proj.reshape(g.nseq, g.t, main)[:, g.t - hist:, :w])
                g.outs["lru_h"].append(h_last)
                g.outs["gla_s"].append(s_new)
        else:
            w_up = l1_w_up[j].astype(BF16)
            w_down = l1_w_down[j].astype(BF16)
            wq_t, wk_t, wv_t = (_blockdiag_tiles(m[j]) for m in (l1_wq, l1_wk, l1_wv))
            wg = _pad_cols(jnp.concatenate([l1_w_ig[j], l1_w_fg[j]], axis=1)).astype(BF16)
            bg = _pad_cols(jnp.concatenate([l1_b_ig[j], l1_b_fg[j]])[None, :])
            fronts = []
            for g in groups:
                _, _, nh, dh = g.st["m_n"].shape
                up = fused_linear([g.x], w_up, n_out=2 * nh * dh, gain=norm_mix_g[layer], name="linear_up")
                q, k, v, xc, gates_m = mlstm_pre(
                    up, _history_tiles(g.st["m_conv"][j]), l1_conv_w[j], l1_conv_b[j], wq_t, wk_t, wv_t, wg, bg,
                    nseq=g.nseq, t=g.t, n_heads=nh, **g.plan["pre"])
                fronts.append((q, k, v, gates_m, xc, up))
                g.outs["m_conv"].append(up.reshape(g.nseq, g.t, 2 * nh * dh)[:, g.t - hist:, :nh * dh])
            recs = _mlstm_recurrences(groups, fronts, j, l1_skip[j], l1_norm_g[j])
            for g, (hout, c_new, n_new, m_new) in zip(groups, recs):
                g.x, g.xn, g.gates = fused_linear([hout], w_down, n_out=g.d, res=g.x, router=router, tm=256,
                                                  tn=g.d, name="linear_down")
                g.outs["m_c"].append(c_new)
                g.outs["m_n"].append(n_new)
                g.outs["m_m"].append(m_new)
        for g in groups:
            g.x = moe_experts(g.xn, g.gates, moe_w_gate, moe_w_up, moe_w_down, layer, g.x,
                              final_gain=norm_final_g if layer == depth - 1 else None)
    return [g.result() for g in groups]


def _mlstm_recurrences(groups, fronts, j, skip, norm_g):
    nh = groups[0].st["m_n"].shape[2]
    hb = groups[0].plan["rec"]["hb"]
    if len(groups) == 2:
        for il, i_s in ((0, 1), (1, 0)):
            gl, gs = groups[il], groups[i_s]
            fits = gl.st["m_c"] is None and gs.st["m_c"] is not None and gs.t == SUBLANES and gl.t > SUBLANES
            chunk = dual_chunk_rows(gl.nseq, gl.t, gs.nseq, nh, hb) if fits else None
            if chunk is not None:
                rl, rs = mlstm_recurrence_dual(
                    fronts[il], fronts[i_s], gs.st["m_c"][j], gs.st["m_n"][j], gs.st["m_m"][j], skip, norm_g,
                    nseq_l=gl.nseq, t_l=gl.t, nseq_s=gs.nseq, t_s=gs.t, nh=nh, hb=hb, chunk=chunk)
                return [rl, rs] if il == 0 else [rs, rl]
    return [mlstm_recurrence(*f, None if g.st["m_c"] is None else g.st["m_c"][j], g.st["m_n"][j], g.st["m_m"][j],
                             skip, norm_g, nseq=g.nseq, t=g.t, nh=nh, **g.plan["rec"])
            for g, f in zip(groups, fronts)]


def _trunk(x3, *states_and_weights):
    return _trunks([_Group(x3, states_and_weights[:7])], *states_and_weights[7:])[0]


def kernel(x_prompt, x_sample, state_lru_conv, state_lru_h, state_gla_S, state_mlstm_conv, state_mlstm_C,
           state_mlstm_n, state_mlstm_m, norm_mix_g, norm_ffn_g, norm_final_g, l0_w_in, l0_lru_conv_w,
           l0_lru_conv_b, l0_lru_wa, l0_lru_ba, l0_lru_wx, l0_lru_bx, l0_lru_lam, l0_gla_wa2, l0_gla_ba2,
           l0_gla_norm_g, l0_w_out, l1_w_up, l1_conv_w, l1_conv_b, l1_wq, l1_wk, l1_wv, l1_w_ig, l1_b_ig,
           l1_w_fg, l1_b_fg, l1_skip, l1_norm_g, l1_w_down, moe_w_rg, moe_b_rg, moe_w_re, moe_b_re, moe_w_gate,
           moe_w_up, moe_w_down):
    weights = (norm_mix_g, norm_ffn_g, norm_final_g, l0_w_in, l0_lru_conv_w, l0_lru_conv_b, l0_lru_wa, l0_lru_ba,
               l0_lru_wx, l0_lru_bx, l0_lru_lam, l0_gla_wa2, l0_gla_ba2, l0_gla_norm_g, l0_w_out, l1_w_up,
               l1_conv_w, l1_conv_b, l1_wq, l1_wk, l1_wv, l1_w_ig, l1_b_ig, l1_w_fg, l1_b_fg, l1_skip, l1_norm_g,
               l1_w_down, moe_w_rg, moe_b_rg, moe_w_re, moe_b_re, moe_w_gate, moe_w_up, moe_w_down)
    states = (state_lru_conv, state_lru_h, state_gla_S, state_mlstm_conv, state_mlstm_C, state_mlstm_n,
              state_mlstm_m)
    bp = x_prompt.shape[0]
    zero_states = tuple(None if s is state_mlstm_C else jnp.zeros((s.shape[0], bp) + s.shape[2:], s.dtype)
                        for s in states)
    prompt, sample = _trunks([_Group(x_prompt, zero_states), _Group(x_sample, states)], *weights)
    return (prompt[0], sample[0]) + prompt[1:] + sample[1:]
```

```python
import functools

import jax
import jax.numpy as jnp
from jax import lax
from jax.experimental import pallas as pl
from jax.experimental.pallas import tpu as pltpu

EPS = 1e-6
CONV_W = 4
LRU_C = 8.0
GLA_TAU = 16.0
GLA_CHUNK = 64
GLA_SUB = 8
M_CHUNK = 256
N_GROUPS = 4
E_PER_GROUP = 4
N_EXPERTS = N_GROUPS * E_PER_GROUP

V7X_VMEM_BYTES = 64 * 1024 * 1024
VMEM_LIMIT_BYTES = V7X_VMEM_BYTES - 8 * 1024 * 1024
SUBLANES = 8
LANES = 128
MXU_WIDTH = 256

F32 = jnp.float32
BF16 = jnp.bfloat16


def _params(*semantics):
    return pltpu.CompilerParams(dimension_semantics=semantics, vmem_limit_bytes=VMEM_LIMIT_BYTES)


def _dot(a, b):
    return jnp.dot(a.astype(BF16), b.astype(BF16), preferred_element_type=F32)


def _dot_nt(a, b):
    return lax.dot_general(a.astype(BF16), b.astype(BF16), (((1,), (1,)), ((), ())), preferred_element_type=F32)


def _dot_tn(a, b):
    return lax.dot_general(a.astype(BF16), b.astype(BF16), (((0,), (0,)), ((), ())), preferred_element_type=F32)


def _dot_split(m01, y):
    y_hi = y.astype(BF16)
    y_lo = (y - y_hi.astype(F32)).astype(BF16)
    m = m01.astype(BF16)
    return jnp.dot(m, y_hi, preferred_element_type=F32) + jnp.dot(m, y_lo, preferred_element_type=F32)


def _softplus(x):
    return jnp.maximum(x, 0.0) + jnp.log1p(jnp.exp(-jnp.abs(x)))


def _log_sigmoid(x):
    return -_softplus(-x)


def _rmsnorm_rows(x, g):
    return x * lax.rsqrt(jnp.mean(x * x, axis=-1, keepdims=True) + EPS) * g


def _iota(shape, dim):
    return lax.broadcasted_iota(jnp.int32, shape, dim)


def _lane_column(x, lane_index):
    return jnp.sum(jnp.where(_iota(x.shape, 1) == lane_index, x, 0.0), axis=1, keepdims=True)


def _row_tile(n, target):
    t = min(n, target)
    assert n % t == 0
    return t


def _route(xn, wr_ref, br_ref):
    logits = _dot(xn, wr_ref[...]) + br_ref[...]
    lane = _iota(logits.shape, 1)
    big = jnp.int32(LANES)
    is_g = jnp.logical_and(lane >= N_EXPERTS, lane < N_EXPERTS + N_GROUPS)
    gl = jnp.where(is_g, logits, -jnp.inf)
    gmax = jnp.max(gl, axis=1, keepdims=True)
    gsum = jnp.sum(jnp.where(is_g, jnp.exp(gl - gmax), 0.0), axis=1, keepdims=True)
    p_g = 1.0 / gsum
    g_idx = jnp.min(jnp.where(gl == gmax, lane, big), axis=1, keepdims=True) - N_EXPERTS
    sel = jnp.logical_and(lane < N_EXPERTS, (lane >> 2) == g_idx)
    el = jnp.where(sel, logits, -jnp.inf)
    emax = jnp.max(el, axis=1, keepdims=True)
    eexp = jnp.where(sel, jnp.exp(el - emax), 0.0)
    ep = eexp / jnp.sum(eexp, axis=1, keepdims=True)
    cand = jnp.where(sel, ep, -1.0)
    v1 = jnp.max(cand, axis=1, keepdims=True)
    idx1 = jnp.min(jnp.where(cand == v1, lane, big), axis=1, keepdims=True)
    cand2 = jnp.where(lane == idx1, -1.0, cand)
    v2 = jnp.max(cand2, axis=1, keepdims=True)
    idx2 = jnp.min(jnp.where(cand2 == v2, lane, big), axis=1, keepdims=True)
    tot = v1 + v2
    return jnp.where(lane == idx1, v1 / tot * p_g, 0.0) + jnp.where(lane == idx2, v2 / tot * p_g, 0.0)


def _linear_kernel(*refs, n_lhs, has_norm, has_res, has_router, has_decay):
    lhs_refs = refs[:n_lhs]
    pos = n_lhs
    g_ref = refs[pos] if has_norm else None
    pos += int(has_norm)
    w_ref = refs[pos]
    pos += 1
    res_ref = refs[pos] if has_res else None
    pos += int(has_res)
    if has_router:
        fg_ref, wr_ref, br_ref = refs[pos:pos + 3]
        pos += 3
    if has_decay:
        walr_ref, wa2_ref, ba2_ref = refs[pos:pos + 3]
        pos += 3
    o_ref = refs[pos]
    pos += 1
    if has_router:
        xn_ref, gates_ref = refs[pos:pos + 2]
        pos += 2
    if has_decay:
        lg_ref = refs[pos]
        pos += 1
    lhs_scr = refs[pos]

    @pl.when(pl.program_id(1) == 0)
    def _():
        off = 0
        for a_ref in lhs_refs:
            a = a_ref[...]
            if has_norm:
                a = _rmsnorm_rows(a, g_ref[...])
            lhs_scr[:, off:off + a.shape[1]] = a.astype(BF16)
            off += a.shape[1]
        if has_decay:
            alr = jnp.dot(lhs_scr[...], walr_ref[...], preferred_element_type=F32)
            lg_ref[...] = _log_sigmoid(_dot(alr, wa2_ref[...]) + ba2_ref[...]) * (1.0 / GLA_TAU)

    acc = jnp.dot(lhs_scr[...], w_ref[...].astype(BF16), preferred_element_type=F32)
    if has_res:
        acc = res_ref[...] + acc
    o_ref[...] = acc
    if has_router:
        xn = _rmsnorm_rows(acc, fg_ref[...])
        xn_ref[...] = xn.astype(BF16)
        gates_ref[...] = _route(xn, wr_ref, br_ref)


def fused_linear(lhs_list, w, *, n_out, gain=None, res=None, router=None, decay=None, tm=1024, tn=1024,
                 name="linear"):
    n = lhs_list[0].shape[0]
    ks = [a.shape[1] for a in lhs_list]
    ktot = sum(ks)
    assert w.shape[0] == ktot
    tm = _row_tile(n, tm)
    tn = _row_tile(n_out, tn)
    resident = tn == n_out == w.shape[1]
    assert router is None or resident
    in_specs = [pl.BlockSpec((tm, k), lambda i, j: (i, 0)) for k in ks]
    args = list(lhs_list)
    if gain is not None:
        in_specs.append(pl.BlockSpec((1, ktot), lambda i, j: (0, 0)))
        args.append(gain.reshape(1, ktot))
    in_specs.append(pl.BlockSpec((ktot, tn), lambda i, j: (0, j), pipeline_mode=pl.Buffered(1) if resident else None))
    args.append(w)
    if res is not None:
        in_specs.append(pl.BlockSpec((tm, tn), lambda i, j: (i, j)))
        args.append(res)
    out_specs = [pl.BlockSpec((tm, tn), lambda i, j: (i, j))]
    out_shape = [jax.ShapeDtypeStruct((n, n_out), F32)]
    if router is not None:
        fgain, wr, br = router
        in_specs += [pl.BlockSpec((1, n_out), lambda i, j: (0, 0)), pl.BlockSpec((n_out, LANES), lambda i, j: (0, 0)),
                     pl.BlockSpec((1, LANES), lambda i, j: (0, 0))]
        args += [fgain.reshape(1, n_out), wr, br]
        out_specs += [pl.BlockSpec((tm, n_out), lambda i, j: (i, 0)), pl.BlockSpec((tm, LANES), lambda i, j: (i, 0))]
        out_shape += [jax.ShapeDtypeStruct((n, n_out), BF16), jax.ShapeDtypeStruct((n, LANES), F32)]
    if decay is not None:
        w_alr, wa2, ba2 = decay
        hk = wa2.shape[1]
        in_specs += [pl.BlockSpec((ktot, LANES), lambda i, j: (0, 0)), pl.BlockSpec((LANES, hk), lambda i, j: (0, 0)),
                     pl.BlockSpec((1, hk), lambda i, j: (0, 0))]
        args += [w_alr, wa2, ba2.reshape(1, hk)]
        out_specs.append(pl.BlockSpec((tm, hk), lambda i, j: (i, 0)))
        out_shape.append(jax.ShapeDtypeStruct((n, hk), F32))
    kern = functools.partial(_linear_kernel, n_lhs=len(lhs_list), has_norm=gain is not None, has_res=res is not None,
                             has_router=router is not None, has_decay=decay is not None)
    outs = pl.pallas_call(
        kern,
        grid=(n // tm, n_out // tn),
        in_specs=in_specs,
        out_specs=out_specs,
        out_shape=out_shape,
        scratch_shapes=[pltpu.VMEM((tm, ktot), BF16)],
        compiler_params=_params("parallel", "arbitrary"),
        name=name,
    )(*args)
    return outs if len(outs) > 1 else outs[0]


def _causal_conv_short(x, prev, w_ref, b_ref):
    r = x.shape[0]
    pos = _iota((r, 1), 0) & (SUBLANES - 1)
    acc = b_ref[...] + x * w_ref[CONV_W - 1:CONV_W, :]
    for j in range(1, CONV_W):
        shifted = jnp.where(pos < j, pltpu.roll(prev, (j - SUBLANES) % r, 0), pltpu.roll(x, j, 0))
        acc = acc + shifted * w_ref[CONV_W - 1 - j:CONV_W - j, :]
    return acc


def _causal_conv_chunk(x, hist_scr, w_ref, b_ref):
    r = x.shape[0]
    acc = b_ref[...] + x * w_ref[CONV_W - 1:CONV_W, :]
    head = x[0:SUBLANES]
    hist = hist_scr[...]
    acc_head = b_ref[...] + head * w_ref[CONV_W - 1:CONV_W, :]
    pos = _iota((SUBLANES, 1), 0)
    for j in range(1, CONV_W):
        tap = w_ref[CONV_W - 1 - j:CONV_W - j, :]
        acc = acc + pltpu.roll(x, j, 0) * tap
        acc_head = acc_head + jnp.where(pos < j, pltpu.roll(hist, j, 0), pltpu.roll(head, j, 0)) * tap
    hist_scr[...] = x[r - SUBLANES:]
    return jnp.concatenate([acc_head, acc[SUBLANES:]], axis=0)


def _lru_kernel(xa_ref, ga_ref, prev_ref, h0_ref, cw_ref, cb_ref, wa_ref, ba_ref, wx_ref, bx_ref, lam_ref,
                ya_ref, hl_ref, prev_scr, h_scr, a_scr, b_scr, hs_scr, *, nb, tc):
    c = pl.program_id(1)

    @pl.when(c == 0)
    def _():
        if nb == 1:
            prev_scr[...] = prev_ref[...]
        h_scr[...] = h0_ref[...]

    x = xa_ref[...]
    if nb == 1:
        xc = _causal_conv_chunk(x, prev_scr, cw_ref, cb_ref)
    else:
        xc = _causal_conv_short(x, prev_ref[...], cw_ref, cb_ref)
    nblk = x.shape[1] // MXU_WIDTH
    xcb = xc.astype(BF16)
    r_parts, i_parts = [], []
    for t in range(nblk):
        sl = slice(t * MXU_WIDTH, (t + 1) * MXU_WIDTH)
        r_parts.append(jnp.dot(xcb[:, sl], wa_ref[t], preferred_element_type=F32))
        i_parts.append(jnp.dot(xcb[:, sl], wx_ref[t], preferred_element_type=F32))
    r_gate = jax.nn.sigmoid(jnp.concatenate(r_parts, axis=1) + ba_ref[...])
    i_gate = jax.nn.sigmoid(jnp.concatenate(i_parts, axis=1) + bx_ref[...])
    log_a = (-LRU_C) * r_gate * _softplus(-lam_ref[...])
    a = jnp.exp(log_a)
    a_scr[...] = a
    b_scr[...] = jnp.sqrt(-jnp.tanh(log_a) * (a * a + 1.0)) * (i_gate * xc)

    def seq_body(s, carry):
        def tile_body(tl, h):
            base = pl.multiple_of(s * tc + tl * SUBLANES, SUBLANES)
            for i in range(SUBLANES):
                h = a_scr[pl.ds(base + i, 1), :] * h + b_scr[pl.ds(base + i, 1), :]
                hs_scr[pl.ds(base + i, 1), :] = h
            return h

        h_scr[s] = lax.fori_loop(0, tc // SUBLANES, tile_body, h_scr[s])
        return carry

    lax.fori_loop(0, nb, seq_body, 0, unroll=min(nb, 4))
    ya_ref[...] = hs_scr[...] * jax.nn.gelu(ga_ref[...])

    @pl.when(c == pl.num_programs(1) - 1)
    def _():
        hl_ref[...] = h_scr[...]


def lru_branch(proj, prev8, h0, conv_w, conv_b, wa_t, ba, wx_t, bx, lam, *, nseq, t, nb, tc):
    n = proj.shape[0]
    w = h0.shape[1]
    nchunk = t // tc
    assert nseq % nb == 0 and t % tc == 0 and (nb == 1 or tc == t == SUBLANES)
    r = nb * tc
    nblk = w // MXU_WIDTH
    rows = lambda s, c: s * nchunk + c
    full2 = lambda s, c: (0, 0)
    full3 = lambda s, c: (0, 0, 0)
    kern = functools.partial(_lru_kernel, nb=nb, tc=tc)
    ya, hl = pl.pallas_call(
        kern,
        grid=(nseq // nb, nchunk),
        in_specs=[
            pl.BlockSpec((r, w), lambda s, c: (rows(s, c), 0)),
            pl.BlockSpec((r, w), lambda s, c: (rows(s, c), 1)),
            pl.BlockSpec((nb * SUBLANES, w), lambda s, c: (s, 0)),
            pl.BlockSpec((nb, 1, w), lambda s, c: (s, 0, 0)),
            pl.BlockSpec((CONV_W, w), full2),
            pl.BlockSpec((1, w), full2),
            pl.BlockSpec((nblk, MXU_WIDTH, MXU_WIDTH), full3),
            pl.BlockSpec((1, w), full2),
            pl.BlockSpec((nblk, MXU_WIDTH, MXU_WIDTH), full3),
            pl.BlockSpec((1, w), full2),
            pl.BlockSpec((1, w), full2),
        ],
        out_specs=[
            pl.BlockSpec((r, w), lambda s, c: (rows(s, c), 0)),
            pl.BlockSpec((nb, 1, w), lambda s, c: (s, 0, 0)),
        ],
        out_shape=[jax.ShapeDtypeStruct((n, w), F32), jax.ShapeDtypeStruct((nseq, 1, w), F32)],
        scratch_shapes=[
            pltpu.VMEM((SUBLANES, w), F32),
            pltpu.VMEM((nb, 1, w), F32),
            pltpu.VMEM((r, w), F32),
            pltpu.VMEM((r, w), F32),
            pltpu.VMEM((r, w), F32),
        ],
        compiler_params=_params("parallel", "arbitrary"),
        name="lru_branch",
    )(proj, proj, prev8, h0.reshape(nseq, 1, w), conv_w, conv_b.reshape(1, w), wa_t, ba.reshape(1, w), wx_t,
      bx.reshape(1, w), lam.reshape(1, w))
    return ya, hl.reshape(nseq, w)


def _gla_cumdecay(lg, run_rows):
    r = lg.shape[0]
    row = _iota((r, r), 0)
    col = _iota((r, r), 1)
    shift = run_rows.bit_length() - 1
    tri = jnp.logical_and(row >= col, (row >> shift) == (col >> shift))
    return _dot_split(tri.astype(F32), lg)


def _gla_near_att(qs, k, bc, sub):
    r = qs.shape[0]
    row = _iota((r, r), 0)
    col = _iota((r, r), 1)
    posr = _iota((r, 1), 0) & (sub - 1)
    att = jnp.zeros((r, r), F32)
    for d in range(sub):
        kd = k if d == 0 else pltpu.roll(k, d, 0)
        bcd = bc if d == 0 else pltpu.roll(bc, d, 0)
        valid = posr >= d
        prod = qs * kd * jnp.exp(jnp.where(valid, bc - bcd, 0.0))
        diag = jnp.sum(jnp.where(valid, prod, 0.0), axis=1, keepdims=True)
        att = att + jnp.where(col == row - d, diag, 0.0)
    return att


def _gla_far_att(qs, k, bc, sub):
    l, dk = qs.shape
    att = jnp.zeros((l, l), F32)
    for j in range(l // sub - 1):
        lo, hi = j * sub, (j + 1) * sub
        e_j = bc[hi - 1:hi, :]
        kp = k[lo:hi] * jnp.exp(e_j - bc[lo:hi])
        qp = qs[hi:] * jnp.exp(bc[hi:] - e_j)
        k_rows = [jnp.zeros((lo, dk), F32)] * (lo > 0) + [kp, jnp.zeros((l - hi, dk), F32)]
        att = att + _dot_nt(jnp.concatenate([jnp.zeros((hi, dk), F32), qp], axis=0), jnp.concatenate(k_rows, axis=0))
    return att


def _as_column(row_vec):
    d = row_vec.shape[1]
    eye = _iota((d, d), 0) == _iota((d, d), 1)
    return jnp.sum(jnp.where(eye, row_vec, 0.0), axis=1, keepdims=True)


def _gla_kernel(q_ref, k_ref, v_ref, g_ref, lg_ref, s0_ref, gn_ref, yb_ref, sn_ref, *scratch, nb, chunk, sub):
    c = pl.program_id(2)
    scale = q_ref.shape[1] ** -0.5

    def finish(o, g):
        on = o * lax.rsqrt(jnp.mean(o * o, axis=-1, keepdims=True) + EPS) * gn_ref[...]
        return on * jax.nn.silu(g)

    if nb == 1:
        (s_scr,) = scratch

        @pl.when(c == 0)
        def _():
            s_scr[...] = s0_ref[0, 0]

        def body(i, carry):
            rows = pl.ds(pl.multiple_of(i * chunk, chunk), chunk)
            qs = q_ref[rows, :] * scale
            k = k_ref[rows, :]
            v = v_ref[rows, :]
            bc = _gla_cumdecay(lg_ref[rows, :], chunk)
            s_state = s_scr[...]
            o = _dot(qs * jnp.exp(bc), s_state) + _dot(_gla_near_att(qs, k, bc, sub) + _gla_far_att(qs, k, bc, sub), v)
            bl = bc[chunk - 1:chunk, :]
            s_scr[...] = _as_column(jnp.exp(bl)) * s_state + _dot_tn(k * jnp.exp(bl - bc), v)
            yb_ref[rows, :] = finish(o, g_ref[rows, :])
            return carry

        lax.fori_loop(0, q_ref.shape[0] // chunk, body, 0, unroll=True)

        @pl.when(c == pl.num_programs(2) - 1)
        def _():
            sn_ref[0, 0] = s_scr[...]
    else:
        qe_scr, kd_scr, eb_scr, o_scr = scratch
        r = q_ref.shape[0]
        qs = q_ref[...] * scale
        k = k_ref[...]
        bc = _gla_cumdecay(lg_ref[...], chunk)
        o_scr[...] = _dot(_gla_near_att(qs, k, bc, sub), v_ref[...])
        qe_scr[...] = qs * jnp.exp(bc)
        row = _iota((r, r), 0)
        col = _iota((r, r), 1)
        last = (col == (row | (chunk - 1))).astype(F32)
        bl = _dot_split(last, bc)
        kd_scr[...] = k * jnp.exp(bl - bc)
        eb_scr[...] = jnp.exp(bl)

        def body(j, carry):
            rows = pl.ds(pl.multiple_of(j * chunk, chunk), chunk)
            s_state = s0_ref[j, 0]
            o_scr[rows, :] += _dot(qe_scr[rows, :], s_state)
            decay = _as_column(eb_scr[pl.ds(pl.multiple_of(j * chunk, chunk), 1), :])
            sn_ref[j, 0] = decay * s_state + _dot_tn(kd_scr[rows, :], v_ref[rows, :])
            return carry

        lax.fori_loop(0, nb, body, 0, unroll=min(nb, 8))
        yb_ref[...] = finish(o_scr[...], g_ref[...])


def gla_branch(proj, lg, s0, gnorm, *, nseq, t, nb, rows_blk, chunk, sub, q_off, k_off, v_off, g_off):
    n = proj.shape[0]
    _, nh, dk, dv = s0.shape
    ntb = t // rows_blk if nb == 1 else 1
    r = rows_blk if nb == 1 else nb * t
    assert (nb == 1 and t % rows_blk == 0 and rows_blk % chunk == 0) or (chunk == sub == t and nseq % nb == 0)
    rows = lambda s, h, c: s * ntb + c
    kern = functools.partial(_gla_kernel, nb=nb, chunk=chunk, sub=sub)
    if nb == 1:
        scratch = [pltpu.VMEM((dk, dv), F32)]
    else:
        scratch = [pltpu.VMEM((r, dk), F32), pltpu.VMEM((r, dk), F32), pltpu.VMEM((r, dk), F32), pltpu.VMEM((r, dv), F32)]
    yb, sn = pl.pallas_call(
        kern,
        grid=(nseq // nb, nh, ntb),
        in_specs=[
            pl.BlockSpec((r, dk), lambda s, h, c: (rows(s, h, c), q_off // dk + h)),
            pl.BlockSpec((r, dk), lambda s, h, c: (rows(s, h, c), k_off // dk + h)),
            pl.BlockSpec((r, dv), lambda s, h, c: (rows(s, h, c), v_off // dv + h)),
            pl.BlockSpec((r, dv), lambda s, h, c: (rows(s, h, c), g_off // dv + h)),
            pl.BlockSpec((r, dk), lambda s, h, c: (rows(s, h, c), h)),
            pl.BlockSpec((nb, 1, dk, dv), lambda s, h, c: (s, h, 0, 0)),
            pl.BlockSpec((1, dv), lambda s, h, c: (0, h)),
        ],
        out_specs=[
            pl.BlockSpec((r, dv), lambda s, h, c: (rows(s, h, c), h)),
            pl.BlockSpec((nb, 1, dk, dv), lambda s, h, c: (s, h, 0, 0)),
        ],
        out_shape=[jax.ShapeDtypeStruct((n, nh * dv), F32), jax.ShapeDtypeStruct(s0.shape, F32)],
        scratch_shapes=scratch,
        compiler_params=_params("parallel", "parallel", "arbitrary"),
        name="gla_branch",
    )(proj, proj, proj, proj, lg, s0, gnorm.reshape(1, nh * dv))
    return yb, sn


def _mlstm_pre_kernel(xm_ref, prev_ref, cw_ref, cb_ref, wq_ref, wk_ref, wv_ref, wg_ref, bg_ref,
                      q_ref, k_ref, v_ref, xc_ref, gate_ref, prev_scr, *, nb, tc, n_heads):
    c = pl.program_id(1)

    x = xm_ref[...]
    if nb == 1:
        @pl.when(c == 0)
        def _():
            prev_scr[...] = prev_ref[...]

        xc = jax.nn.silu(_causal_conv_chunk(x, prev_scr, cw_ref, cb_ref))
    else:
        xc = jax.nn.silu(_causal_conv_short(x, prev_ref[...], cw_ref, cb_ref))
    xc_ref[...] = xc
    di = x.shape[1]
    xcb = xc.astype(BF16)
    xb = x.astype(BF16)
    for t in range(di // MXU_WIDTH):
        sl = slice(t * MXU_WIDTH, (t + 1) * MXU_WIDTH)
        q_ref[:, sl] = jnp.dot(xcb[:, sl], wq_ref[t], preferred_element_type=F32).astype(q_ref.dtype)
        k_ref[:, sl] = jnp.dot(xcb[:, sl], wk_ref[t], preferred_element_type=F32).astype(k_ref.dtype)
        v_ref[:, sl] = jnp.dot(xb[:, sl], wv_ref[t], preferred_element_type=F32).astype(v_ref.dtype)
    pre = (_dot(q_ref[...], wg_ref[0:di, :]) + _dot(k_ref[...], wg_ref[di:2 * di, :])
           + _dot(v_ref[...], wg_ref[2 * di:3 * di, :]) + bg_ref[...])
    gate_ref[...] = jnp.where(_iota(pre.shape, 1) < n_heads, pre, _log_sigmoid(pre))


def mlstm_pre(up, prev8, conv_w, conv_b, wq_t, wk_t, wv_t, wg, bg, *, nseq, t, nb, tc, n_heads, qkv_dtype):
    n = up.shape[0]
    di = conv_w.shape[1]
    nchunk = t // tc
    assert nseq % nb == 0 and t % tc == 0 and (nb == 1 or tc == t == SUBLANES)
    r = nb * tc
    nblk = di // MXU_WIDTH
    rows = lambda s, c: (s * nchunk + c, 0)
    full2 = lambda s, c: (0, 0)
    full3 = lambda s, c: (0, 0, 0)
    wide = jax.ShapeDtypeStruct((n, di), F32)
    kern = functools.partial(_mlstm_pre_kernel, nb=nb, tc=tc, n_heads=n_heads)
    return pl.pallas_call(
        kern,
        grid=(nseq // nb, nchunk),
        in_specs=[
            pl.BlockSpec((r, di), rows),
            pl.BlockSpec((nb * SUBLANES, di), lambda s, c: (s, 0)),
            pl.BlockSpec((CONV_W, di), full2),
            pl.BlockSpec((1, di), full2),
            pl.BlockSpec((nblk, MXU_WIDTH, MXU_WIDTH), full3),
            pl.BlockSpec((nblk, MXU_WIDTH, MXU_WIDTH), full3),
            pl.BlockSpec((nblk, MXU_WIDTH, MXU_WIDTH), full3),
            pl.BlockSpec((3 * di, LANES), full2),
            pl.BlockSpec((1, LANES), full2),
        ],
        out_specs=[pl.BlockSpec((r, di), rows)] * 4 + [pl.BlockSpec((r, LANES), rows)],
        out_shape=[jax.ShapeDtypeStruct((n, di), qkv_dtype)] * 3 + [wide, jax.ShapeDtypeStruct((n, LANES), F32)],
        scratch_shapes=[pltpu.VMEM((SUBLANES, di), F32)],
        compiler_params=_params("parallel", "arbitrary"),
        name="mlstm_pre",
    )(up, prev8, conv_w, conv_b.reshape(1, di), wq_t, wk_t, wv_t, wg, bg)


def _mlstm_chunk(q, k, v, igc, lfc, xc, z, skip, ng, m_prev, n_prev, c_load, c_store):
    l, dh = q.shape
    qb = q.astype(BF16)
    q = q.astype(F32)
    row = _iota((l, l), 0)
    col = _iota((l, l), 1)
    tril = row >= col
    f_b = jnp.broadcast_to(lfc, (l, l))
    i_b = jnp.broadcast_to(igc, (l, l))
    f_col = _dot_split(tril.astype(F32), f_b)
    row_term = _dot_split(jnp.ones((l, l), F32),
                          jnp.where(row == col, i_b, 0.0) - jnp.where(row <= col, f_b, 0.0))
    dm = jnp.where(tril, f_col + row_term, -jnp.inf)
    fcum = f_col[:, 0:1]
    prev = m_prev + fcum
    mt = jnp.maximum(prev, jnp.max(dm, axis=1, keepdims=True))
    wprev = jnp.exp(prev - mt)
    smat = _dot_nt(qb, k) * jnp.exp(dm - mt)
    den = wprev * jnp.sum(q * n_prev, axis=1, keepdims=True) + jnp.sum(smat, axis=1, keepdims=True)
    inv = 1.0 / jnp.maximum(jnp.abs(den), jnp.exp(-mt))
    f_last = fcum[l - 1:l, :]
    m_last = mt[l - 1:l, :]
    w_c = jnp.exp(m_prev + f_last - m_last)
    kw = k * jnp.exp(f_last - fcum + igc - m_last)
    sb = smat.astype(BF16)
    kwb = kw.astype(BF16)
    parts = []
    for t in range(dh // MXU_WIDTH):
        cols = slice(t * MXU_WIDTH, (t + 1) * MXU_WIDTH)
        c_blk = c_load(cols)
        vb = v[:, cols].astype(BF16)
        num = (wprev * jnp.dot(qb, c_blk.astype(BF16), preferred_element_type=F32)
               + jnp.dot(sb, vb, preferred_element_type=F32))
        parts.append(num * inv)
        c_store(cols, w_c * c_blk + _dot_tn(kwb, vb))
    n_new = w_c * n_prev + jnp.sum(kw, axis=0, keepdims=True)
    hh = jnp.concatenate(parts, axis=1)
    hc = hh - jnp.mean(hh, axis=-1, keepdims=True)
    hn = hc * lax.rsqrt(jnp.mean(hc * hc, axis=-1, keepdims=True) + EPS) * ng
    return (hn + skip * xc) * jax.nn.silu(z), n_new, m_last


def _mlstm_seq_kernel(*refs, chunk, n_heads, zero_state):
    q_ref, k_ref, v_ref, gate_ref, xc_ref, z_ref = refs[:6]
    pos = 6
    if not zero_state:
        c0_ref, n0_ref, m0_ref = refs[pos:pos + 3]
        pos += 3
    skip_ref, ng_ref, out_ref, cn_ref, nn_ref, mn_ref, c_scr, n_scr, m_scr = refs[pos:pos + 9]
    head = pl.program_id(1)
    c = pl.program_id(2)

    @pl.when(c == 0)
    def _():
        if zero_state:
            c_scr[...] = jnp.zeros_like(c_scr)
            n_scr[...] = jnp.zeros_like(n_scr)
            m_scr[...] = jnp.zeros_like(m_scr)
        else:
            c_scr[...] = c0_ref[0, 0]
            n_scr[...] = n0_ref[0]
            m_scr[...] = m0_ref[0]

    kscale = q_ref.shape[1] ** -0.5

    def c_store(cols, value):
        c_scr[:, cols] = value

    def body(i, carry):
        rows = pl.ds(pl.multiple_of(i * chunk, chunk), chunk)
        gates = gate_ref[rows, :]
        out, n_new, m_last = _mlstm_chunk(
            q_ref[rows, :], k_ref[rows, :].astype(F32) * kscale, v_ref[rows, :], _lane_column(gates, head),
            _lane_column(gates, head + n_heads), xc_ref[rows, :], z_ref[rows, :], skip_ref[...], ng_ref[...],
            m_scr[:, 0:1], n_scr[...], lambda cols: c_scr[:, cols], c_store)
        out_ref[rows, :] = out
        n_scr[...] = n_new
        m_scr[...] = jnp.broadcast_to(m_last, m_scr.shape)
        return carry

    lax.fori_loop(0, q_ref.shape[0] // chunk, body, 0)

    @pl.when(c == pl.num_programs(2) - 1)
    def _():
        cn_ref[0, 0] = c_scr[...]
        nn_ref[0] = n_scr[...]
        mn_ref[0] = m_scr[...]


def _mlstm_step_kernel(q_ref, k_ref, v_ref, gate_ref, xc_ref, z_ref, c0_ref, n0_ref, m0_ref, skip_ref, ng_ref,
                       out_ref, cn_ref, nn_ref, mn_ref, *, hb, n_heads, head_block=None):
    hblk = pl.program_id(1) if head_block is None else head_block
    dh = c0_ref.shape[2]
    kscale = dh ** -0.5
    gates = gate_ref[...]
    for hh in range(hb):
        cols_h = slice(hh * dh, (hh + 1) * dh)
        head = hblk * hb + hh

        def c_store(cols, value, hh=hh):
            cn_ref[0, hh, :, cols] = value

        out, n_new, m_last = _mlstm_chunk(
            q_ref[:, cols_h], k_ref[:, cols_h].astype(F32) * kscale, v_ref[:, cols_h], _lane_column(gates, head),
            _lane_column(gates, head + n_heads), xc_ref[:, cols_h], z_ref[:, cols_h], skip_ref[:, cols_h],
            ng_ref[:, cols_h], m0_ref[hh][:, 0:1], n0_ref[hh], lambda cols, hh=hh: c0_ref[0, hh, :, cols], c_store)
        out_ref[:, cols_h] = out
        nn_ref[hh] = n_new
        mn_ref[hh] = jnp.broadcast_to(m_last, (1, LANES))


def mlstm_recurrence(q, k, v, gates, xc, up, c0, n0, m0, skip, norm_g, *, nseq, t, nh, rows_blk, chunk, hb):
    n, di = q.shape
    dh = di // nh
    zero_state = c0 is None
    out_shape = [
        jax.ShapeDtypeStruct((n, di), F32),
        jax.ShapeDtypeStruct((nseq, nh, dh, dh), F32),
        jax.ShapeDtypeStruct((nseq * nh, 1, dh), F32),
        jax.ShapeDtypeStruct((nseq * nh, 1, LANES), F32),
    ]
    state_args = []
    if not zero_state:
        state_args = [c0, n0.reshape(nseq * nh, 1, dh),
                      jnp.broadcast_to(m0.reshape(nseq * nh, 1, 1), (nseq * nh, 1, LANES))]
    if t == chunk and not zero_state:
        assert nh % hb == 0
        nhb = nh // hb
        wide = lambda s, h: (s, h)
        per_head = lambda s, h: (s * nhb + h, 0, 0)
        mat = lambda s, h: (s, h, 0, 0)
        out, cn, nn, mn = pl.pallas_call(
            functools.partial(_mlstm_step_kernel, hb=hb, n_heads=nh),
            grid=(nseq, nhb),
            in_specs=[pl.BlockSpec((t, hb * dh), wide)] * 3
            + [pl.BlockSpec((t, LANES), lambda s, h: (s, 0)), pl.BlockSpec((t, hb * dh), wide),
               pl.BlockSpec((t, hb * dh), lambda s, h: (s, nhb + h)), pl.BlockSpec((1, hb, dh, dh), mat),
               pl.BlockSpec((hb, 1, dh), per_head), pl.BlockSpec((hb, 1, LANES), per_head),
               pl.BlockSpec((1, hb * dh), lambda s, h: (0, h)), pl.BlockSpec((1, hb * dh), lambda s, h: (0, h))],
            out_specs=[pl.BlockSpec((t, hb * dh), wide), pl.BlockSpec((1, hb, dh, dh), mat),
                       pl.BlockSpec((hb, 1, dh), per_head), pl.BlockSpec((hb, 1, LANES), per_head)],
            out_shape=out_shape,
            compiler_params=_params("parallel", "arbitrary"),
            name="mlstm_step",
        )(q, k, v, gates, xc, up, *state_args, skip.reshape(1, di), norm_g.reshape(1, di))
    else:
        ntb = t // rows_blk
        assert t % rows_blk == 0 and rows_blk % chunk == 0
        rows = lambda s, h, c: (s * ntb + c, h)
        per_head = lambda s, h, c: (s * nh + h, 0, 0)
        mat = lambda s, h, c: (s, h, 0, 0)
        state_specs = [] if zero_state else [pl.BlockSpec((1, 1, dh, dh), mat), pl.BlockSpec((1, 1, dh), per_head),
                                             pl.BlockSpec((1, 1, LANES), per_head)]
        out, cn, nn, mn = pl.pallas_call(
            functools.partial(_mlstm_seq_kernel, chunk=chunk, n_heads=nh, zero_state=zero_state),
            grid=(nseq, nh, ntb),
            in_specs=[pl.BlockSpec((rows_blk, dh), rows)] * 3
            + [pl.BlockSpec((rows_blk, LANES), lambda s, h, c: (s * ntb + c, 0)), pl.BlockSpec((rows_blk, dh), rows),
               pl.BlockSpec((rows_blk, dh), lambda s, h, c: (s * ntb + c, nh + h))]
            + state_specs
            + [pl.BlockSpec((1, dh), lambda s, h, c: (0, h)), pl.BlockSpec((1, dh), lambda s, h, c: (0, h))],
            out_specs=[pl.BlockSpec((rows_blk, dh), rows), pl.BlockSpec((1, 1, dh, dh), mat),
                       pl.BlockSpec((1, 1, dh), per_head), pl.BlockSpec((1, 1, LANES), per_head)],
            out_shape=out_shape,
            scratch_shapes=[pltpu.VMEM((dh, dh), F32), pltpu.VMEM((1, dh), F32), pltpu.VMEM((1, LANES), F32)],
            compiler_params=_params("parallel", "parallel", "arbitrary"),
            name="mlstm_seq",
        )(q, k, v, gates, xc, up, *state_args, skip.reshape(1, di), norm_g.reshape(1, di))
    return out, cn, nn.reshape(nseq, nh, dh), mn[:, 0, 0].reshape(nseq, nh)


def _mlstm_dual_kernel(ql_ref, kl_ref, vl_ref, gl_ref, xcl_ref, zl_ref, skipl_ref, ngl_ref,
                       qs_ref, ks_ref, vs_ref, gs_ref, xcs_ref, zs_ref, c0_ref, n0_ref, m0_ref, skips_ref, ngs_ref,
                       outl_ref, cnl_ref, nnl_ref, mnl_ref, outs_ref, cns_ref, nns_ref, mns_ref, n_scr, m_scr,
                       *, hb, n_heads, chunks_per_seq):
    g = pl.program_id(0)
    c = g % chunks_per_seq
    head_l = (g // chunks_per_seq) % n_heads
    dh = ql_ref.shape[1]
    kscale = dh ** -0.5

    @pl.when(c == 0)
    def _():
        cnl_ref[...] = jnp.zeros_like(cnl_ref)
        n_scr[...] = jnp.zeros_like(n_scr)
        m_scr[...] = jnp.zeros_like(m_scr)

    def cl_store(cols, value):
        cnl_ref[0, 0, :, cols] = value

    gl = gl_ref[...]
    out, n_new, m_last = _mlstm_chunk(
        ql_ref[...], kl_ref[...].astype(F32) * kscale, vl_ref[...], _lane_column(gl, head_l),
        _lane_column(gl, head_l + n_heads),
        xcl_ref[...], zl_ref[...], skipl_ref[...], ngl_ref[...], m_scr[:, 0:1], n_scr[...],
        lambda cols: cnl_ref[0, 0, :, cols], cl_store)
    outl_ref[...] = out
    n_scr[...] = n_new
    m_scr[...] = jnp.broadcast_to(m_last, m_scr.shape)

    @pl.when(c == chunks_per_seq - 1)
    def _():
        nnl_ref[0] = n_scr[...]
        mnl_ref[0] = m_scr[...]

    _mlstm_step_kernel(qs_ref, ks_ref, vs_ref, gs_ref, xcs_ref, zs_ref, c0_ref, n0_ref, m0_ref, skips_ref, ngs_ref,
                       outs_ref, cns_ref, nns_ref, mns_ref, hb=hb, n_heads=n_heads, head_block=g % (n_heads // hb))


def dual_chunk_rows(nseq_l, t_l, nseq_s, nh, hb):
    steps = nseq_s * (nh // hb)
    total = nseq_l * t_l * nh
    if nh % hb or total % steps:
        return None
    chunk = total // steps
    ok = chunk % SUBLANES == 0 and t_l % chunk == 0 and 64 <= chunk <= M_CHUNK
    return chunk if ok else None


def mlstm_recurrence_dual(long_in, short_in, c0, n0, m0, skip, norm_g, *, nseq_l, t_l, nseq_s, t_s, nh, hb, chunk):
    di = long_in[0].shape[1]
    dh = di // nh
    nhb = nh // hb
    cps = t_l // chunk
    steps = nseq_s * nhb
    assert steps == nseq_l * nh * cps

    def rows_l(g):
        return (g // (nh * cps)) * cps + g % cps

    head_l = lambda g: (g // cps) % nh
    tile_l = lambda g: (rows_l(g), head_l(g))
    per_head_l = lambda g: (g // cps, 0, 0)
    tile_s = lambda g: (g // nhb, g % nhb)
    per_head_s = lambda g: (g, 0, 0)
    mat_s = lambda g: (g // nhb, g % nhb, 0, 0)
    n_l, n_s = long_in[0].shape[0], short_in[0].shape[0]
    in_specs = (
        [pl.BlockSpec((chunk, dh), tile_l)] * 3
        + [pl.BlockSpec((chunk, LANES), lambda g: (rows_l(g), 0)), pl.BlockSpec((chunk, dh), tile_l),
           pl.BlockSpec((chunk, dh), lambda g: (rows_l(g), nh + head_l(g))),
           pl.BlockSpec((1, dh), lambda g: (0, head_l(g))), pl.BlockSpec((1, dh), lambda g: (0, head_l(g)))]
        + [pl.BlockSpec((t_s, hb * dh), tile_s)] * 3
        + [pl.BlockSpec((t_s, LANES), lambda g: (g // nhb, 0)), pl.BlockSpec((t_s, hb * dh), tile_s),
           pl.BlockSpec((t_s, hb * dh), lambda g: (g // nhb, nhb + g % nhb)), pl.BlockSpec((1, hb, dh, dh), mat_s),
           pl.BlockSpec((hb, 1, dh), per_head_s), pl.BlockSpec((hb, 1, LANES), per_head_s),
           pl.BlockSpec((1, hb * dh), lambda g: (0, g % nhb)), pl.BlockSpec((1, hb * dh), lambda g: (0, g % nhb))])
    out_specs = [
        pl.BlockSpec((chunk, dh), tile_l), pl.BlockSpec((1, 1, dh, dh), lambda g: (g // (nh * cps), head_l(g), 0, 0)),
        pl.BlockSpec((1, 1, dh), per_head_l), pl.BlockSpec((1, 1, LANES), per_head_l),
        pl.BlockSpec((t_s, hb * dh), tile_s), pl.BlockSpec((1, hb, dh, dh), mat_s),
        pl.BlockSpec((hb, 1, dh), per_head_s), pl.BlockSpec((hb, 1, LANES), per_head_s)]
    out_shape = [
        jax.ShapeDtypeStruct((n_l, di), F32), jax.ShapeDtypeStruct((nseq_l, nh, dh, dh), F32),
        jax.ShapeDtypeStruct((nseq_l * nh, 1, dh), F32), jax.ShapeDtypeStruct((nseq_l * nh, 1, LANES), F32),
        jax.ShapeDtypeStruct((n_s, di), F32), jax.ShapeDtypeStruct((nseq_s, nh, dh, dh), F32),
        jax.ShapeDtypeStruct((nseq_s * nh, 1, dh), F32), jax.ShapeDtypeStruct((nseq_s * nh, 1, LANES), F32)]
    skip2, ng2 = skip.reshape(1, di), norm_g.reshape(1, di)
    res = pl.pallas_call(
        functools.partial(_mlstm_dual_kernel, hb=hb, n_heads=nh, chunks_per_seq=cps),
        grid=(steps,),
        in_specs=in_specs,
        out_specs=out_specs,
        out_shape=out_shape,
        scratch_shapes=[pltpu.VMEM((1, dh), F32), pltpu.VMEM((1, LANES), F32)],
        compiler_params=_params("arbitrary"),
        name="mlstm_dual",
    )(*long_in, skip2, ng2, *short_in, c0, n0.reshape(nseq_s * nh, 1, dh),
      jnp.broadcast_to(m0.reshape(nseq_s * nh, 1, 1), (nseq_s * nh, 1, LANES)), skip2, ng2)
    unpack = lambda o, cn, nn, mn, nseq: (o, cn, nn.reshape(nseq, nh, dh), mn[:, 0, 0].reshape(nseq, nh))
    return unpack(*res[:4], nseq_l), unpack(*res[4:], nseq_s)


def _moe_expert_kernel(*refs, has_final, eb):
    xn_ref, gates_ref, wg_ref, wu_ref, wd_ref, x_ref = refs[:6]
    fg_ref = refs[6] if has_final else None
    o_ref = refs[6 + int(has_final)]
    e = pl.program_id(1)

    @pl.when(e == 0)
    def _():
        o_ref[...] = x_ref[...]

    xn = xn_ref[...]
    gates = gates_ref[...]
    update = None
    for k in range(eb):
        hg = jnp.dot(xn, wg_ref[0, k].astype(BF16), preferred_element_type=F32)
        hu = jnp.dot(xn, wu_ref[0, k].astype(BF16), preferred_element_type=F32)
        h = jax.nn.silu(hg) * hu * _lane_column(gates, e * eb + k)
        part = _dot(h, wd_ref[0, k])
        update = part if update is None else update + part
    o_ref[...] += update

    if has_final:
        @pl.when(e == pl.num_programs(1) - 1)
        def _():
            o_ref[...] = _rmsnorm_rows(o_ref[...], fg_ref[...])


def moe_experts(xn, gates, w_gate, w_up, w_down, layer, x, final_gain=None, tm=1024, eb=1):
    n, d = x.shape
    _, ne, _, f = w_gate.shape
    tm = _row_tile(n, tm)
    assert ne % eb == 0
    in_specs = [
        pl.BlockSpec((tm, d), lambda i, e: (i, 0)),
        pl.BlockSpec((tm, LANES), lambda i, e: (i, 0)),
        pl.BlockSpec((1, eb, d, f), lambda i, e: (layer, e, 0, 0)),
        pl.BlockSpec((1, eb, d, f), lambda i, e: (layer, e, 0, 0)),
        pl.BlockSpec((1, eb, f, d), lambda i, e: (layer, e, 0, 0)),
        pl.BlockSpec((tm, d), lambda i, e: (i, 0), pipeline_mode=pl.Buffered(1)),
    ]
    args = [xn, gates, w_gate, w_up, w_down, x]
    if final_gain is not None:
        in_specs.append(pl.BlockSpec((1, d), lambda i, e: (0, 0)))
        args.append(final_gain.reshape(1, d))
    return pl.pallas_call(
        functools.partial(_moe_expert_kernel, has_final=final_gain is not None, eb=eb),
        grid=(n // tm, ne // eb),
        in_specs=in_specs,
        out_specs=pl.BlockSpec((tm, d), lambda i, e: (i, 0)),
        out_shape=jax.ShapeDtypeStruct((n, d), F32),
        compiler_params=_params("parallel", "arbitrary"),
        name="moe_experts",
    )(*args)


def _blockdiag_tiles(w):
    nblocks, bi, bo = w.shape
    per = MXU_WIDTH // bi
    ntiles = nblocks // per
    rows_of_tile = w.reshape(ntiles, MXU_WIDTH, bo)
    spread = jnp.broadcast_to(rows_of_tile[:, :, None, :], (ntiles, MXU_WIDTH, per, bo)).reshape(
        ntiles, MXU_WIDTH, MXU_WIDTH)
    on_diag = (_iota((MXU_WIDTH, MXU_WIDTH), 0) // bi) == (_iota((MXU_WIDTH, MXU_WIDTH), 1) // bo)
    return jnp.where(on_diag, spread, 0.0).astype(BF16)


def _pad_cols(w, width=LANES):
    return jnp.pad(w, ((0, 0), (0, width - w.shape[1])))


def _history_tiles(buf):
    nseq, hist, ch = buf.shape
    return jnp.pad(buf, ((0, 0), (SUBLANES - hist, 0), (0, 0))).reshape(nseq * SUBLANES, ch)


def _block_plan(nseq, t):
    long_seq = t > SUBLANES
    return dict(
        lru=dict(nb=1 if long_seq else min(nseq, 64), tc=min(t, 256)),
        gla=dict(nb=1 if long_seq else min(nseq, 16), rows_blk=min(t, 512), chunk=min(t, GLA_CHUNK), sub=min(t, GLA_SUB)),
        pre=dict(nb=1 if long_seq else min(nseq, 16), tc=min(t, 128)),
        rec=dict(rows_blk=min(t, M_CHUNK), chunk=min(t, M_CHUNK), hb=2),
    )


_STATE_KEYS = ("lru_conv", "lru_h", "gla_s", "m_conv", "m_c", "m_n", "m_m")


class _Group:
    def __init__(self, x3, states):
        self.nseq, self.t, self.d = x3.shape
        self.x = x3.reshape(self.nseq * self.t, self.d)
        self.st = dict(zip(_STATE_KEYS, states))
        self.plan = _block_plan(self.nseq, self.t)
        self.outs = {k: [] for k in _STATE_KEYS}
        self.xn = self.gates = None

    def result(self):
        return (self.x.reshape(self.nseq, self.t, self.d),) + tuple(jnp.stack(self.outs[k]) for k in _STATE_KEYS)


def _trunks(groups,
            norm_mix_g, norm_ffn_g, norm_final_g,
            l0_w_in, l0_lru_conv_w, l0_lru_conv_b, l0_lru_wa, l0_lru_ba, l0_lru_wx, l0_lru_bx, l0_lru_lam,
            l0_gla_wa2, l0_gla_ba2, l0_gla_norm_g, l0_w_out,
            l1_w_up, l1_conv_w, l1_conv_b, l1_wq, l1_wk, l1_wv, l1_w_ig, l1_b_ig, l1_w_fg, l1_b_fg, l1_skip,
            l1_norm_g, l1_w_down,
            moe_w_rg, moe_b_rg, moe_w_re, moe_b_re, moe_w_gate, moe_w_up, moe_w_down):
    depth = norm_mix_g.shape[0]
    hist = CONV_W - 1
    for layer in range(depth):
        j = layer // 2
        wr = _pad_cols(jnp.concatenate([moe_w_re[layer], moe_w_rg[layer]], axis=1)).astype(BF16)
        br = _pad_cols(jnp.concatenate([moe_b_re[layer], moe_b_rg[layer]])[None, :])
        router = (norm_ffn_g[layer], wr, br)
        if layer % 2 == 0:
            w_in = l0_w_in[j].astype(BF16)
            w_out = l0_w_out[j].astype(BF16)
            wa_t, wx_t = _blockdiag_tiles(l0_lru_wa[j]), _blockdiag_tiles(l0_lru_wx[j])
            rank = l0_gla_wa2.shape[1]
            wa2 = jnp.pad(l0_gla_wa2[j], ((0, LANES - rank), (0, 0))).astype(BF16)
            for g in groups:
                w = g.st["lru_h"].shape[-1]
                _, _, nh, dk, dv = g.st["gla_s"].shape
                main = 2 * w + 2 * nh * dk + 2 * nh * dv
                proj, lg = fused_linear([g.x], w_in, n_out=main, gain=norm_mix_g[layer], name="linear_in",
                                        decay=(_pad_cols(w_in[:, main:main + rank]), wa2, l0_gla_ba2[j]))
                ya, h_last = lru_branch(
                    proj, _history_tiles(g.st["lru_conv"][j]), g.st["lru_h"][j], l0_lru_conv_w[j], l0_lru_conv_b[j],
                    wa_t, l0_lru_ba[j], wx_t, l0_lru_bx[j], l0_lru_lam[j], nseq=g.nseq, t=g.t, **g.plan["lru"])
                yb, s_new = gla_branch(
                    proj, lg, g.st["gla_s"][j], l0_gla_norm_g[j], nseq=g.nseq, t=g.t, **g.plan["gla"], q_off=2 * w,
                    k_off=2 * w + nh * dk, v_off=2 * w + 2 * nh * dk, g_off=2 * w + 2 * nh * dk + nh * dv)
                g.x, g.xn, g.gates = fused_linear([ya, yb], w_out, n_out=g.d, res=g.x, router=router, tm=512,
                                                  tn=g.d, name="linear_out")
                g.outs["lru_conv"].append(proj.reshape(g.nseq, g.t, main)[:, g.t - hist:, :w])
                g.outs["lru_h"].append(h_last)
                g.outs["gla_s"].append(s_new)
        else:
            w_up = l1_w_up[j].astype(BF16)
            w_down = l1_w_down[j].astype(BF16)
            wq_t, wk_t, wv_t = (_blockdiag_tiles(m[j]) for m in (l1_wq, l1_wk, l1_wv))
            wg = _pad_cols(jnp.concatenate([l1_w_ig[j], l1_w_fg[j]], axis=1)).astype(BF16)
            bg = _pad_cols(jnp.concatenate([l1_b_ig[j], l1_b_fg[j]])[None, :])
            fronts = []
            for g in groups:
                _, _, nh, dh = g.st["m_n"].shape
                up = fused_linear([g.x], w_up, n_out=2 * nh * dh, gain=norm_mix_g[layer], name="linear_up")
                q, k, v, xc, gates_m = mlstm_pre(
                    up, _history_tiles(g.st["m_conv"][j]), l1_conv_w[j], l1_conv_b[j], wq_t, wk_t, wv_t, wg, bg,
                    nseq=g.nseq, t=g.t, n_heads=nh, qkv_dtype=BF16 if g.t > SUBLANES else F32, **g.plan["pre"])
                fronts.append((q, k, v, gates_m, xc, up))
                g.outs["m_conv"].append(up.reshape(g.nseq, g.t, 2 * nh * dh)[:, g.t - hist:, :nh * dh])
            recs = _mlstm_recurrences(groups, fronts, j, l1_skip[j], l1_norm_g[j])
            for g, (hout, c_new, n_new, m_new) in zip(groups, recs):
                g.x, g.xn, g.gates = fused_linear([hout], w_down, n_out=g.d, res=g.x, router=router, tm=256,
                                                  tn=g.d, name="linear_down")
                g.outs["m_c"].append(c_new)
                g.outs["m_n"].append(n_new)
                g.outs["m_m"].append(m_new)
        for g in groups:
            g.x = moe_experts(g.xn, g.gates, moe_w_gate, moe_w_up, moe_w_down, layer, g.x,
                              final_gain=norm_final_g if layer == depth - 1 else None)
    return [g.result() for g in groups]


def _mlstm_recurrences(groups, fronts, j, skip, norm_g):
    nh = groups[0].st["m_n"].shape[2]
    hb = groups[0].plan["rec"]["hb"]
    if len(groups) == 2:
        for il, i_s in ((0, 1), (1, 0)):
            gl, gs = groups[il], groups[i_s]
            fits = gl.st["m_c"] is None and gs.st["m_c"] is not None and gs.t == SUBLANES and gl.t > SUBLANES
            chunk = dual_chunk_rows(gl.nseq, gl.t, gs.nseq, nh, hb) if fits else None
            if chunk is not None:
                rl, rs = mlstm_recurrence_dual(
                    fronts[il], fronts[i_s], gs.st["m_c"][j], gs.st["m_n"][j], gs.st["m_m"][j], skip, norm_g,
                    nseq_l=gl.nseq, t_l=gl.t, nseq_s=gs.nseq, t_s=gs.t, nh=nh, hb=hb, chunk=chunk)
                return [rl, rs] if il == 0 else [rs, rl]
    return [mlstm_recurrence(*f, None if g.st["m_c"] is None else g.st["m_c"][j], g.st["m_n"][j], g.st["m_m"][j],
                             skip, norm_g, nseq=g.nseq, t=g.t, nh=nh, **g.plan["rec"])
            for g, f in zip(groups, fronts)]


def _trunk(x3, *states_and_weights):
    return _trunks([_Group(x3, states_and_weights[:7])], *states_and_weights[7:])[0]


def kernel(x_prompt, x_sample, state_lru_conv, state_lru_h, state_gla_S, state_mlstm_conv, state_mlstm_C,
           state_mlstm_n, state_mlstm_m, norm_mix_g, norm_ffn_g, norm_final_g, l0_w_in, l0_lru_conv_w,
           l0_lru_conv_b, l0_lru_wa, l0_lru_ba, l0_lru_wx, l0_lru_bx, l0_lru_lam, l0_gla_wa2, l0_gla_ba2,
           l0_gla_norm_g, l0_w_out, l1_w_up, l1_conv_w, l1_conv_b, l1_wq, l1_wk, l1_wv, l1_w_ig, l1_b_ig,
           l1_w_fg, l1_b_fg, l1_skip, l1_norm_g, l1_w_down, moe_w_rg, moe_b_rg, moe_w_re, moe_b_re, moe_w_gate,
           moe_w_up, moe_w_down):
    weights = (norm_mix_g, norm_ffn_g, norm_final_g, l0_w_in, l0_lru_conv_w, l0_lru_conv_b, l0_lru_wa, l0_lru_ba,
               l0_lru_wx, l0_lru_bx, l0_lru_lam, l0_gla_wa2, l0_gla_ba2, l0_gla_norm_g, l0_w_out, l1_w_up,
               l1_conv_w, l1_conv_b, l1_wq, l1_wk, l1_wv, l1_w_ig, l1_b_ig, l1_w_fg, l1_b_fg, l1_skip, l1_norm_g,
               l1_w_down, moe_w_rg, moe_b_rg, moe_w_re, moe_b_re, moe_w_gate, moe_w_up, moe_w_down)
    states = (state_lru_conv, state_lru_h, state_gla_S, state_mlstm_conv, state_mlstm_C, state_mlstm_n,
              state_mlstm_m)
    bp = x_prompt.shape[0]
    zero_states = tuple(None if s is state_mlstm_C else jnp.zeros((s.shape[0], bp) + s.shape[2:], s.dtype)
                        for s in states)
    prompt, sample = _trunks([_Group(x_prompt, zero_states), _Group(x_sample, states)], *weights)
    return (prompt[0], sample[0]) + prompt[1:] + sample[1:]
```

```python
import functools

import jax
import jax.numpy as jnp
from jax import lax
from jax.experimental import pallas as pl
from jax.experimental.pallas import tpu as pltpu

EPS = 1e-6
CONV_W = 4
LRU_C = 8.0
GLA_TAU = 16.0
GLA_CHUNK = 64
GLA_SUB = 8
M_CHUNK = 256
N_GROUPS = 4
E_PER_GROUP = 4
N_EXPERTS = N_GROUPS * E_PER_GROUP

V7X_VMEM_BYTES = 64 * 1024 * 1024
VMEM_LIMIT_BYTES = V7X_VMEM_BYTES - 8 * 1024 * 1024
SUBLANES = 8
LANES = 128
MXU_WIDTH = 256

F32 = jnp.float32
BF16 = jnp.bfloat16


def _params(*semantics):
    return pltpu.CompilerParams(dimension_semantics=semantics, vmem_limit_bytes=VMEM_LIMIT_BYTES)


def _dot(a, b):
    return jnp.dot(a.astype(BF16), b.astype(BF16), preferred_element_type=F32)


def _dot_nt(a, b):
    return lax.dot_general(a.astype(BF16), b.astype(BF16), (((1,), (1,)), ((), ())), preferred_element_type=F32)


def _dot_tn(a, b):
    return lax.dot_general(a.astype(BF16), b.astype(BF16), (((0,), (0,)), ((), ())), preferred_element_type=F32)


def _dot_split(m01, y):
    y_hi = y.astype(BF16)
    y_lo = (y - y_hi.astype(F32)).astype(BF16)
    m = m01.astype(BF16)
    return jnp.dot(m, y_hi, preferred_element_type=F32) + jnp.dot(m, y_lo, preferred_element_type=F32)


def _softplus(x):
    return jnp.maximum(x, 0.0) + jnp.log1p(jnp.exp(-jnp.abs(x)))


def _log_sigmoid(x):
    return -_softplus(-x)


def _rmsnorm_rows(x, g):
    return x * lax.rsqrt(jnp.mean(x * x, axis=-1, keepdims=True) + EPS) * g


def _iota(shape, dim):
    return lax.broadcasted_iota(jnp.int32, shape, dim)


def _lane_column(x, lane_index):
    return jnp.sum(jnp.where(_iota(x.shape, 1) == lane_index, x, 0.0), axis=1, keepdims=True)


def _row_tile(n, target):
    t = min(n, target)
    assert n % t == 0
    return t


def _route(xn, wr_ref, br_ref):
    logits = _dot(xn, wr_ref[...]) + br_ref[...]
    lane = _iota(logits.shape, 1)
    big = jnp.int32(LANES)
    is_g = jnp.logical_and(lane >= N_EXPERTS, lane < N_EXPERTS + N_GROUPS)
    gl = jnp.where(is_g, logits, -jnp.inf)
    gmax = jnp.max(gl, axis=1, keepdims=True)
    gsum = jnp.sum(jnp.where(is_g, jnp.exp(gl - gmax), 0.0), axis=1, keepdims=True)
    p_g = 1.0 / gsum
    g_idx = jnp.min(jnp.where(gl == gmax, lane, big), axis=1, keepdims=True) - N_EXPERTS
    sel = jnp.logical_and(lane < N_EXPERTS, (lane >> 2) == g_idx)
    el = jnp.where(sel, logits, -jnp.inf)
    emax = jnp.max(el, axis=1, keepdims=True)
    eexp = jnp.where(sel, jnp.exp(el - emax), 0.0)
    ep = eexp / jnp.sum(eexp, axis=1, keepdims=True)
    cand = jnp.where(sel, ep, -1.0)
    v1 = jnp.max(cand, axis=1, keepdims=True)
    idx1 = jnp.min(jnp.where(cand == v1, lane, big), axis=1, keepdims=True)
    cand2 = jnp.where(lane == idx1, -1.0, cand)
    v2 = jnp.max(cand2, axis=1, keepdims=True)
    idx2 = jnp.min(jnp.where(cand2 == v2, lane, big), axis=1, keepdims=True)
    tot = v1 + v2
    return jnp.where(lane == idx1, v1 / tot * p_g, 0.0) + jnp.where(lane == idx2, v2 / tot * p_g, 0.0)


def _linear_kernel(*refs, n_lhs, has_norm, has_res, has_router, has_decay):
    lhs_refs = refs[:n_lhs]
    pos = n_lhs
    g_ref = refs[pos] if has_norm else None
    pos += int(has_norm)
    w_ref = refs[pos]
    pos += 1
    res_ref = refs[pos] if has_res else None
    pos += int(has_res)
    if has_router:
        fg_ref, wr_ref, br_ref = refs[pos:pos + 3]
        pos += 3
    if has_decay:
        walr_ref, wa2_ref, ba2_ref = refs[pos:pos + 3]
        pos += 3
    o_ref = refs[pos]
    pos += 1
    if has_router:
        xn_ref, gates_ref = refs[pos:pos + 2]
        pos += 2
    if has_decay:
        lg_ref = refs[pos]
        pos += 1
    lhs_scr = refs[pos]

    @pl.when(pl.program_id(1) == 0)
    def _():
        off = 0
        for a_ref in lhs_refs:
            a = a_ref[...]
            if has_norm:
                a = _rmsnorm_rows(a, g_ref[...])
            lhs_scr[:, off:off + a.shape[1]] = a.astype(BF16)
            off += a.shape[1]
        if has_decay:
            alr = jnp.dot(lhs_scr[...], walr_ref[...], preferred_element_type=F32)
            lg_ref[...] = _log_sigmoid(_dot(alr, wa2_ref[...]) + ba2_ref[...]) * (1.0 / GLA_TAU)

    acc = jnp.dot(lhs_scr[...], w_ref[...].astype(BF16), preferred_element_type=F32)
    if has_res:
        acc = res_ref[...] + acc
    o_ref[...] = acc
    if has_router:
        xn = _rmsnorm_rows(acc, fg_ref[...])
        xn_ref[...] = xn.astype(BF16)
        gates_ref[...] = _route(xn, wr_ref, br_ref)


def fused_linear(lhs_list, w, *, n_out, gain=None, res=None, router=None, decay=None, tm=1024, tn=1024,
                 name="linear"):
    n = lhs_list[0].shape[0]
    ks = [a.shape[1] for a in lhs_list]
    ktot = sum(ks)
    assert w.shape[0] == ktot
    tm = _row_tile(n, tm)
    tn = _row_tile(n_out, tn)
    resident = tn == n_out == w.shape[1]
    assert router is None or resident
    in_specs = [pl.BlockSpec((tm, k), lambda i, j: (i, 0)) for k in ks]
    args = list(lhs_list)
    if gain is not None:
        in_specs.append(pl.BlockSpec((1, ktot), lambda i, j: (0, 0)))
        args.append(gain.reshape(1, ktot))
    in_specs.append(pl.BlockSpec((ktot, tn), lambda i, j: (0, j), pipeline_mode=pl.Buffered(1) if resident else None))
    args.append(w)
    if res is not None:
        in_specs.append(pl.BlockSpec((tm, tn), lambda i, j: (i, j)))
        args.append(res)
    out_specs = [pl.BlockSpec((tm, tn), lambda i, j: (i, j))]
    out_shape = [jax.ShapeDtypeStruct((n, n_out), F32)]
    if router is not None:
        fgain, wr, br = router
        in_specs += [pl.BlockSpec((1, n_out), lambda i, j: (0, 0)), pl.BlockSpec((n_out, LANES), lambda i, j: (0, 0)),
                     pl.BlockSpec((1, LANES), lambda i, j: (0, 0))]
        args += [fgain.reshape(1, n_out), wr, br]
        out_specs += [pl.BlockSpec((tm, n_out), lambda i, j: (i, 0)), pl.BlockSpec((tm, LANES), lambda i, j: (i, 0))]
        out_shape += [jax.ShapeDtypeStruct((n, n_out), BF16), jax.ShapeDtypeStruct((n, LANES), F32)]
    if decay is not None:
        w_alr, wa2, ba2 = decay
        hk = wa2.shape[1]
        in_specs += [pl.BlockSpec((ktot, LANES), lambda i, j: (0, 0)), pl.BlockSpec((LANES, hk), lambda i, j: (0, 0)),
                     pl.BlockSpec((1, hk), lambda i, j: (0, 0))]
        args += [w_alr, wa2, ba2.reshape(1, hk)]
        out_specs.append(pl.BlockSpec((tm, hk), lambda i, j: (i, 0)))
        out_shape.append(jax.ShapeDtypeStruct((n, hk), F32))
    kern = functools.partial(_linear_kernel, n_lhs=len(lhs_list), has_norm=gain is not None, has_res=res is not None,
                             has_router=router is not None, has_decay=decay is not None)
    outs = pl.pallas_call(
        kern,
        grid=(n // tm, n_out // tn),
        in_specs=in_specs,
        out_specs=out_specs,
        out_shape=out_shape,
        scratch_shapes=[pltpu.VMEM((tm, ktot), BF16)],
        compiler_params=_params("parallel", "arbitrary"),
        name=name,
    )(*args)
    return outs if len(outs) > 1 else outs[0]


def _causal_conv_short(x, prev, w_ref, b_ref):
    r = x.shape[0]
    pos = _iota((r, 1), 0) & (SUBLANES - 1)
    acc = b_ref[...] + x * w_ref[CONV_W - 1:CONV_W, :]
    for j in range(1, CONV_W):
        shifted = jnp.where(pos < j, pltpu.roll(prev, (j - SUBLANES) % r, 0), pltpu.roll(x, j, 0))
        acc = acc + shifted * w_ref[CONV_W - 1 - j:CONV_W - j, :]
    return acc


def _causal_conv_chunk(x, hist_scr, w_ref, b_ref):
    r = x.shape[0]
    acc = b_ref[...] + x * w_ref[CONV_W - 1:CONV_W, :]
    head = x[0:SUBLANES]
    hist = hist_scr[...]
    acc_head = b_ref[...] + head * w_ref[CONV_W - 1:CONV_W, :]
    pos = _iota((SUBLANES, 1), 0)
    for j in range(1, CONV_W):
        tap = w_ref[CONV_W - 1 - j:CONV_W - j, :]
        acc = acc + pltpu.roll(x, j, 0) * tap
        acc_head = acc_head + jnp.where(pos < j, pltpu.roll(hist, j, 0), pltpu.roll(head, j, 0)) * tap
    hist_scr[...] = x[r - SUBLANES:]
    return jnp.concatenate([acc_head, acc[SUBLANES:]], axis=0)


def _lru_kernel(xa_ref, ga_ref, prev_ref, h0_ref, cw_ref, cb_ref, wa_ref, ba_ref, wx_ref, bx_ref, lam_ref,
                ya_ref, hl_ref, prev_scr, h_scr, a_scr, b_scr, hs_scr, *, nb, tc):
    c = pl.program_id(1)

    @pl.when(c == 0)
    def _():
        if nb == 1:
            prev_scr[...] = prev_ref[...]
        h_scr[...] = h0_ref[...]

    x = xa_ref[...]
    if nb == 1:
        xc = _causal_conv_chunk(x, prev_scr, cw_ref, cb_ref)
    else:
        xc = _causal_conv_short(x, prev_ref[...], cw_ref, cb_ref)
    nblk = x.shape[1] // MXU_WIDTH
    xcb = xc.astype(BF16)
    r_parts, i_parts = [], []
    for t in range(nblk):
        sl = slice(t * MXU_WIDTH, (t + 1) * MXU_WIDTH)
        r_parts.append(jnp.dot(xcb[:, sl], wa_ref[t], preferred_element_type=F32))
        i_parts.append(jnp.dot(xcb[:, sl], wx_ref[t], preferred_element_type=F32))
    r_gate = jax.nn.sigmoid(jnp.concatenate(r_parts, axis=1) + ba_ref[...])
    i_gate = jax.nn.sigmoid(jnp.concatenate(i_parts, axis=1) + bx_ref[...])
    log_a = (-LRU_C) * r_gate * _softplus(-lam_ref[...])
    a = jnp.exp(log_a)
    a_scr[...] = a
    b_scr[...] = jnp.sqrt(-jnp.tanh(log_a) * (a * a + 1.0)) * (i_gate * xc)

    def seq_body(s, carry):
        def tile_body(tl, h):
            base = pl.multiple_of(s * tc + tl * SUBLANES, SUBLANES)
            for i in range(SUBLANES):
                h = a_scr[pl.ds(base + i, 1), :] * h + b_scr[pl.ds(base + i, 1), :]
                hs_scr[pl.ds(base + i, 1), :] = h
            return h

        h_scr[s] = lax.fori_loop(0, tc // SUBLANES, tile_body, h_scr[s])
        return carry

    lax.fori_loop(0, nb, seq_body, 0, unroll=min(nb, 4))
    ya_ref[...] = (hs_scr[...] * jax.nn.gelu(ga_ref[...])).astype(ya_ref.dtype)

    @pl.when(c == pl.num_programs(1) - 1)
    def _():
        hl_ref[...] = h_scr[...]


def lru_branch(proj, prev8, h0, conv_w, conv_b, wa_t, ba, wx_t, bx, lam, *, nseq, t, nb, tc):
    n = proj.shape[0]
    w = h0.shape[1]
    nchunk = t // tc
    assert nseq % nb == 0 and t % tc == 0 and (nb == 1 or tc == t == SUBLANES)
    r = nb * tc
    nblk = w // MXU_WIDTH
    rows = lambda s, c: s * nchunk + c
    full2 = lambda s, c: (0, 0)
    full3 = lambda s, c: (0, 0, 0)
    kern = functools.partial(_lru_kernel, nb=nb, tc=tc)
    ya, hl = pl.pallas_call(
        kern,
        grid=(nseq // nb, nchunk),
        in_specs=[
            pl.BlockSpec((r, w), lambda s, c: (rows(s, c), 0)),
            pl.BlockSpec((r, w), lambda s, c: (rows(s, c), 1)),
            pl.BlockSpec((nb * SUBLANES, w), lambda s, c: (s, 0)),
            pl.BlockSpec((nb, 1, w), lambda s, c: (s, 0, 0)),
            pl.BlockSpec((CONV_W, w), full2),
            pl.BlockSpec((1, w), full2),
            pl.BlockSpec((nblk, MXU_WIDTH, MXU_WIDTH), full3),
            pl.BlockSpec((1, w), full2),
            pl.BlockSpec((nblk, MXU_WIDTH, MXU_WIDTH), full3),
            pl.BlockSpec((1, w), full2),
            pl.BlockSpec((1, w), full2),
        ],
        out_specs=[
            pl.BlockSpec((r, w), lambda s, c: (rows(s, c), 0)),
            pl.BlockSpec((nb, 1, w), lambda s, c: (s, 0, 0)),
        ],
        out_shape=[jax.ShapeDtypeStruct((n, w), BF16), jax.ShapeDtypeStruct((nseq, 1, w), F32)],
        scratch_shapes=[
            pltpu.VMEM((SUBLANES, w), F32),
            pltpu.VMEM((nb, 1, w), F32),
            pltpu.VMEM((r, w), F32),
            pltpu.VMEM((r, w), F32),
            pltpu.VMEM((r, w), F32),
        ],
        compiler_params=_params("parallel", "arbitrary"),
        name="lru_branch",
    )(proj, proj, prev8, h0.reshape(nseq, 1, w), conv_w, conv_b.reshape(1, w), wa_t, ba.reshape(1, w), wx_t,
      bx.reshape(1, w), lam.reshape(1, w))
    return ya, hl.reshape(nseq, w)


def _gla_cumdecay(lg, run_rows):
    r = lg.shape[0]
    row = _iota((r, r), 0)
    col = _iota((r, r), 1)
    shift = run_rows.bit_length() - 1
    tri = jnp.logical_and(row >= col, (row >> shift) == (col >> shift))
    return _dot_split(tri.astype(F32), lg)


def _gla_near_att(qs, k, bc, sub):
    r = qs.shape[0]
    row = _iota((r, r), 0)
    col = _iota((r, r), 1)
    posr = _iota((r, 1), 0) & (sub - 1)
    att = jnp.zeros((r, r), F32)
    for d in range(sub):
        kd = k if d == 0 else pltpu.roll(k, d, 0)
        bcd = bc if d == 0 else pltpu.roll(bc, d, 0)
        valid = posr >= d
        prod = qs * kd * jnp.exp(jnp.where(valid, bc - bcd, 0.0))
        diag = jnp.sum(jnp.where(valid, prod, 0.0), axis=1, keepdims=True)
        att = att + jnp.where(col == row - d, diag, 0.0)
    return att


def _gla_far_att(qs, k, bc, sub):
    l, dk = qs.shape
    att = jnp.zeros((l, l), F32)
    for j in range(l // sub - 1):
        lo, hi = j * sub, (j + 1) * sub
        e_j = bc[hi - 1:hi, :]
        kp = k[lo:hi] * jnp.exp(e_j - bc[lo:hi])
        qp = qs[hi:] * jnp.exp(bc[hi:] - e_j)
        k_rows = [jnp.zeros((lo, dk), F32)] * (lo > 0) + [kp, jnp.zeros((l - hi, dk), F32)]
        att = att + _dot_nt(jnp.concatenate([jnp.zeros((hi, dk), F32), qp], axis=0), jnp.concatenate(k_rows, axis=0))
    return att


def _as_column(row_vec):
    d = row_vec.shape[1]
    eye = _iota((d, d), 0) == _iota((d, d), 1)
    return jnp.sum(jnp.where(eye, row_vec, 0.0), axis=1, keepdims=True)


def _gla_kernel(q_ref, k_ref, v_ref, g_ref, lg_ref, s0_ref, gn_ref, yb_ref, sn_ref, *scratch, nb, chunk, sub):
    c = pl.program_id(2)
    scale = q_ref.shape[1] ** -0.5

    def finish(o, g):
        on = o * lax.rsqrt(jnp.mean(o * o, axis=-1, keepdims=True) + EPS) * gn_ref[...]
        return on * jax.nn.silu(g)

    if nb == 1:
        (s_scr,) = scratch

        @pl.when(c == 0)
        def _():
            s_scr[...] = s0_ref[0, 0]

        def body(i, carry):
            rows = pl.ds(pl.multiple_of(i * chunk, chunk), chunk)
            qs = q_ref[rows, :] * scale
            k = k_ref[rows, :]
            v = v_ref[rows, :]
            bc = _gla_cumdecay(lg_ref[rows, :], chunk)
            s_state = s_scr[...]
            o = _dot(qs * jnp.exp(bc), s_state) + _dot(_gla_near_att(qs, k, bc, sub) + _gla_far_att(qs, k, bc, sub), v)
            bl = bc[chunk - 1:chunk, :]
            s_scr[...] = _as_column(jnp.exp(bl)) * s_state + _dot_tn(k * jnp.exp(bl - bc), v)
            yb_ref[rows, :] = finish(o, g_ref[rows, :]).astype(yb_ref.dtype)
            return carry

        lax.fori_loop(0, q_ref.shape[0] // chunk, body, 0, unroll=True)

        @pl.when(c == pl.num_programs(2) - 1)
        def _():
            sn_ref[0, 0] = s_scr[...]
    else:
        qe_scr, kd_scr, eb_scr, o_scr = scratch
        r = q_ref.shape[0]
        qs = q_ref[...] * scale
        k = k_ref[...]
        bc = _gla_cumdecay(lg_ref[...], chunk)
        o_scr[...] = _dot(_gla_near_att(qs, k, bc, sub), v_ref[...])
        qe_scr[...] = qs * jnp.exp(bc)
        row = _iota((r, r), 0)
        col = _iota((r, r), 1)
        last = (col == (row | (chunk - 1))).astype(F32)
        bl = _dot_split(last, bc)
        kd_scr[...] = k * jnp.exp(bl - bc)
        eb_scr[...] = jnp.exp(bl)

        def body(j, carry):
            rows = pl.ds(pl.multiple_of(j * chunk, chunk), chunk)
            s_state = s0_ref[j, 0]
            o_scr[rows, :] += _dot(qe_scr[rows, :], s_state)
            decay = _as_column(eb_scr[pl.ds(pl.multiple_of(j * chunk, chunk), 1), :])
            sn_ref[j, 0] = decay * s_state + _dot_tn(kd_scr[rows, :], v_ref[rows, :])
            return carry

        lax.fori_loop(0, nb, body, 0, unroll=min(nb, 8))
        yb_ref[...] = finish(o_scr[...], g_ref[...]).astype(yb_ref.dtype)


def gla_branch(proj, lg, s0, gnorm, *, nseq, t, nb, rows_blk, chunk, sub, q_off, k_off, v_off, g_off):
    n = proj.shape[0]
    _, nh, dk, dv = s0.shape
    ntb = t // rows_blk if nb == 1 else 1
    r = rows_blk if nb == 1 else nb * t
    assert (nb == 1 and t % rows_blk == 0 and rows_blk % chunk == 0) or (chunk == sub == t and nseq % nb == 0)
    rows = lambda s, h, c: s * ntb + c
    kern = functools.partial(_gla_kernel, nb=nb, chunk=chunk, sub=sub)
    if nb == 1:
        scratch = [pltpu.VMEM((dk, dv), F32)]
    else:
        scratch = [pltpu.VMEM((r, dk), F32), pltpu.VMEM((r, dk), F32), pltpu.VMEM((r, dk), F32), pltpu.VMEM((r, dv), F32)]
    yb, sn = pl.pallas_call(
        kern,
        grid=(nseq // nb, nh, ntb),
        in_specs=[
            pl.BlockSpec((r, dk), lambda s, h, c: (rows(s, h, c), q_off // dk + h)),
            pl.BlockSpec((r, dk), lambda s, h, c: (rows(s, h, c), k_off // dk + h)),
            pl.BlockSpec((r, dv), lambda s, h, c: (rows(s, h, c), v_off // dv + h)),
            pl.BlockSpec((r, dv), lambda s, h, c: (rows(s, h, c), g_off // dv + h)),
            pl.BlockSpec((r, dk), lambda s, h, c: (rows(s, h, c), h)),
            pl.BlockSpec((nb, 1, dk, dv), lambda s, h, c: (s, h, 0, 0)),
            pl.BlockSpec((1, dv), lambda s, h, c: (0, h)),
        ],
        out_specs=[
            pl.BlockSpec((r, dv), lambda s, h, c: (rows(s, h, c), h)),
            pl.BlockSpec((nb, 1, dk, dv), lambda s, h, c: (s, h, 0, 0)),
        ],
        out_shape=[jax.ShapeDtypeStruct((n, nh * dv), BF16), jax.ShapeDtypeStruct(s0.shape, F32)],
        scratch_shapes=scratch,
        compiler_params=_params("parallel", "parallel", "arbitrary"),
        name="gla_branch",
    )(proj, proj, proj, proj, lg, s0, gnorm.reshape(1, nh * dv))
    return yb, sn


def _mlstm_pre_kernel(xm_ref, prev_ref, cw_ref, cb_ref, wq_ref, wk_ref, wv_ref, wg_ref, bg_ref,
                      q_ref, k_ref, v_ref, xc_ref, gate_ref, prev_scr, *, nb, tc, n_heads):
    c = pl.program_id(1)

    x = xm_ref[...]
    if nb == 1:
        @pl.when(c == 0)
        def _():
            prev_scr[...] = prev_ref[...]

        xc = jax.nn.silu(_causal_conv_chunk(x, prev_scr, cw_ref, cb_ref))
    else:
        xc = jax.nn.silu(_causal_conv_short(x, prev_ref[...], cw_ref, cb_ref))
    xc_ref[...] = xc
    di = x.shape[1]
    xcb = xc.astype(BF16)
    xb = x.astype(BF16)
    for t in range(di // MXU_WIDTH):
        sl = slice(t * MXU_WIDTH, (t + 1) * MXU_WIDTH)
        q_ref[:, sl] = jnp.dot(xcb[:, sl], wq_ref[t], preferred_element_type=F32).astype(q_ref.dtype)
        k_ref[:, sl] = jnp.dot(xcb[:, sl], wk_ref[t], preferred_element_type=F32).astype(k_ref.dtype)
        v_ref[:, sl] = jnp.dot(xb[:, sl], wv_ref[t], preferred_element_type=F32).astype(v_ref.dtype)
    pre = (_dot(q_ref[...], wg_ref[0:di, :]) + _dot(k_ref[...], wg_ref[di:2 * di, :])
           + _dot(v_ref[...], wg_ref[2 * di:3 * di, :]) + bg_ref[...])
    gate_ref[...] = jnp.where(_iota(pre.shape, 1) < n_heads, pre, _log_sigmoid(pre))


def mlstm_pre(up, prev8, conv_w, conv_b, wq_t, wk_t, wv_t, wg, bg, *, nseq, t, nb, tc, n_heads, qkv_dtype):
    n = up.shape[0]
    di = conv_w.shape[1]
    nchunk = t // tc
    assert nseq % nb == 0 and t % tc == 0 and (nb == 1 or tc == t == SUBLANES)
    r = nb * tc
    nblk = di // MXU_WIDTH
    rows = lambda s, c: (s * nchunk + c, 0)
    full2 = lambda s, c: (0, 0)
    full3 = lambda s, c: (0, 0, 0)
    wide = jax.ShapeDtypeStruct((n, di), F32)
    kern = functools.partial(_mlstm_pre_kernel, nb=nb, tc=tc, n_heads=n_heads)
    return pl.pallas_call(
        kern,
        grid=(nseq // nb, nchunk),
        in_specs=[
            pl.BlockSpec((r, di), rows),
            pl.BlockSpec((nb * SUBLANES, di), lambda s, c: (s, 0)),
            pl.BlockSpec((CONV_W, di), full2),
            pl.BlockSpec((1, di), full2),
            pl.BlockSpec((nblk, MXU_WIDTH, MXU_WIDTH), full3),
            pl.BlockSpec((nblk, MXU_WIDTH, MXU_WIDTH), full3),
            pl.BlockSpec((nblk, MXU_WIDTH, MXU_WIDTH), full3),
            pl.BlockSpec((3 * di, LANES), full2),
            pl.BlockSpec((1, LANES), full2),
        ],
        out_specs=[pl.BlockSpec((r, di), rows)] * 4 + [pl.BlockSpec((r, LANES), rows)],
        out_shape=[jax.ShapeDtypeStruct((n, di), qkv_dtype)] * 3 + [wide, jax.ShapeDtypeStruct((n, LANES), F32)],
        scratch_shapes=[pltpu.VMEM((SUBLANES, di), F32)],
        compiler_params=_params("parallel", "arbitrary"),
        name="mlstm_pre",
    )(up, prev8, conv_w, conv_b.reshape(1, di), wq_t, wk_t, wv_t, wg, bg)


def _mlstm_chunk(q, k, v, igc, lfc, xc, z, skip, ng, m_prev, n_prev, c_load, c_store):
    l, dh = q.shape
    qb = q.astype(BF16)
    q = q.astype(F32)
    row = _iota((l, l), 0)
    col = _iota((l, l), 1)
    tril = row >= col
    f_b = jnp.broadcast_to(lfc, (l, l))
    i_b = jnp.broadcast_to(igc, (l, l))
    f_col = _dot_split(tril.astype(F32), f_b)
    row_term = _dot_split(jnp.ones((l, l), F32),
                          jnp.where(row == col, i_b, 0.0) - jnp.where(row <= col, f_b, 0.0))
    dm = jnp.where(tril, f_col + row_term, -jnp.inf)
    fcum = f_col[:, 0:1]
    prev = m_prev + fcum
    mt = jnp.maximum(prev, jnp.max(dm, axis=1, keepdims=True))
    wprev = jnp.exp(prev - mt)
    smat = _dot_nt(qb, k) * jnp.exp(dm - mt)
    den = wprev * jnp.sum(q * n_prev, axis=1, keepdims=True) + jnp.sum(smat, axis=1, keepdims=True)
    inv = 1.0 / jnp.maximum(jnp.abs(den), jnp.exp(-mt))
    f_last = fcum[l - 1:l, :]
    m_last = mt[l - 1:l, :]
    w_c = jnp.exp(m_prev + f_last - m_last)
    kw = k * jnp.exp(f_last - fcum + igc - m_last)
    sb = smat.astype(BF16)
    kwb = kw.astype(BF16)
    parts = []
    for t in range(dh // MXU_WIDTH):
        cols = slice(t * MXU_WIDTH, (t + 1) * MXU_WIDTH)
        c_blk = c_load(cols)
        vb = v[:, cols].astype(BF16)
        num = (wprev * jnp.dot(qb, c_blk.astype(BF16), preferred_element_type=F32)
               + jnp.dot(sb, vb, preferred_element_type=F32))
        parts.append(num * inv)
        c_store(cols, w_c * c_blk + _dot_tn(kwb, vb))
    n_new = w_c * n_prev + jnp.sum(kw, axis=0, keepdims=True)
    hh = jnp.concatenate(parts, axis=1)
    hc = hh - jnp.mean(hh, axis=-1, keepdims=True)
    hn = hc * lax.rsqrt(jnp.mean(hc * hc, axis=-1, keepdims=True) + EPS) * ng
    return (hn + skip * xc) * jax.nn.silu(z), n_new, m_last


def _mlstm_seq_kernel(*refs, chunk, n_heads, zero_state):
    q_ref, k_ref, v_ref, gate_ref, xc_ref, z_ref = refs[:6]
    pos = 6
    if not zero_state:
        c0_ref, n0_ref, m0_ref = refs[pos:pos + 3]
        pos += 3
    skip_ref, ng_ref, out_ref, cn_ref, nn_ref, mn_ref, c_scr, n_scr, m_scr = refs[pos:pos + 9]
    head = pl.program_id(1)
    c = pl.program_id(2)

    @pl.when(c == 0)
    def _():
        if zero_state:
            c_scr[...] = jnp.zeros_like(c_scr)
            n_scr[...] = jnp.zeros_like(n_scr)
            m_scr[...] = jnp.zeros_like(m_scr)
        else:
            c_scr[...] = c0_ref[0, 0]
            n_scr[...] = n0_ref[0]
            m_scr[...] = m0_ref[0]

    kscale = q_ref.shape[1] ** -0.5

    def c_store(cols, value):
        c_scr[:, cols] = value

    def body(i, carry):
        rows = pl.ds(pl.multiple_of(i * chunk, chunk), chunk)
        gates = gate_ref[rows, :]
        out, n_new, m_last = _mlstm_chunk(
            q_ref[rows, :], k_ref[rows, :].astype(F32) * kscale, v_ref[rows, :], _lane_column(gates, head),
            _lane_column(gates, head + n_heads), xc_ref[rows, :], z_ref[rows, :], skip_ref[...], ng_ref[...],
            m_scr[:, 0:1], n_scr[...], lambda cols: c_scr[:, cols], c_store)
        out_ref[rows, :] = out
        n_scr[...] = n_new
        m_scr[...] = jnp.broadcast_to(m_last, m_scr.shape)
        return carry

    lax.fori_loop(0, q_ref.shape[0] // chunk, body, 0)

    @pl.when(c == pl.num_programs(2) - 1)
    def _():
        cn_ref[0, 0] = c_scr[...]
        nn_ref[0] = n_scr[...]
        mn_ref[0] = m_scr[...]


def _mlstm_step_kernel(q_ref, k_ref, v_ref, gate_ref, xc_ref, z_ref, c0_ref, n0_ref, m0_ref, skip_ref, ng_ref,
                       out_ref, cn_ref, nn_ref, mn_ref, *, hb, n_heads, head_block=None):
    hblk = pl.program_id(1) if head_block is None else head_block
    dh = c0_ref.shape[2]
    kscale = dh ** -0.5
    gates = gate_ref[...]
    for hh in range(hb):
        cols_h = slice(hh * dh, (hh + 1) * dh)
        head = hblk * hb + hh

        def c_store(cols, value, hh=hh):
            cn_ref[0, hh, :, cols] = value

        out, n_new, m_last = _mlstm_chunk(
            q_ref[:, cols_h], k_ref[:, cols_h].astype(F32) * kscale, v_ref[:, cols_h], _lane_column(gates, head),
            _lane_column(gates, head + n_heads), xc_ref[:, cols_h], z_ref[:, cols_h], skip_ref[:, cols_h],
            ng_ref[:, cols_h], m0_ref[hh][:, 0:1], n0_ref[hh], lambda cols, hh=hh: c0_ref[0, hh, :, cols], c_store)
        out_ref[:, cols_h] = out
        nn_ref[hh] = n_new
        mn_ref[hh] = jnp.broadcast_to(m_last, (1, LANES))


def mlstm_recurrence(q, k, v, gates, xc, up, c0, n0, m0, skip, norm_g, *, nseq, t, nh, rows_blk, chunk, hb):
    n, di = q.shape
    dh = di // nh
    zero_state = c0 is None
    out_shape = [
        jax.ShapeDtypeStruct((n, di), F32),
        jax.ShapeDtypeStruct((nseq, nh, dh, dh), F32),
        jax.ShapeDtypeStruct((nseq * nh, 1, dh), F32),
        jax.ShapeDtypeStruct((nseq * nh, 1, LANES), F32),
    ]
    state_args = []
    if not zero_state:
        state_args = [c0, n0.reshape(nseq * nh, 1, dh),
                      jnp.broadcast_to(m0.reshape(nseq * nh, 1, 1), (nseq * nh, 1, LANES))]
    if t == chunk and not zero_state:
        assert nh % hb == 0
        nhb = nh // hb
        wide = lambda s, h: (s, h)
        per_head = lambda s, h: (s * nhb + h, 0, 0)
        mat = lambda s, h: (s, h, 0, 0)
        out, cn, nn, mn = pl.pallas_call(
            functools.partial(_mlstm_step_kernel, hb=hb, n_heads=nh),
            grid=(nseq, nhb),
            in_specs=[pl.BlockSpec((t, hb * dh), wide)] * 3
            + [pl.BlockSpec((t, LANES), lambda s, h: (s, 0)), pl.BlockSpec((t, hb * dh), wide),
               pl.BlockSpec((t, hb * dh), lambda s, h: (s, nhb + h)), pl.BlockSpec((1, hb, dh, dh), mat),
               pl.BlockSpec((hb, 1, dh), per_head), pl.BlockSpec((hb, 1, LANES), per_head),
               pl.BlockSpec((1, hb * dh), lambda s, h: (0, h)), pl.BlockSpec((1, hb * dh), lambda s, h: (0, h))],
            out_specs=[pl.BlockSpec((t, hb * dh), wide), pl.BlockSpec((1, hb, dh, dh), mat),
                       pl.BlockSpec((hb, 1, dh), per_head), pl.BlockSpec((hb, 1, LANES), per_head)],
            out_shape=out_shape,
            compiler_params=_params("parallel", "arbitrary"),
            name="mlstm_step",
        )(q, k, v, gates, xc, up, *state_args, skip.reshape(1, di), norm_g.reshape(1, di))
    else:
        ntb = t // rows_blk
        assert t % rows_blk == 0 and rows_blk % chunk == 0
        rows = lambda s, h, c: (s * ntb + c, h)
        per_head = lambda s, h, c: (s * nh + h, 0, 0)
        mat = lambda s, h, c: (s, h, 0, 0)
        state_specs = [] if zero_state else [pl.BlockSpec((1, 1, dh, dh), mat), pl.BlockSpec((1, 1, dh), per_head),
                                             pl.BlockSpec((1, 1, LANES), per_head)]
        out, cn, nn, mn = pl.pallas_call(
            functools.partial(_mlstm_seq_kernel, chunk=chunk, n_heads=nh, zero_state=zero_state),
            grid=(nseq, nh, ntb),
            in_specs=[pl.BlockSpec((rows_blk, dh), rows)] * 3
            + [pl.BlockSpec((rows_blk, LANES), lambda s, h, c: (s * ntb + c, 0)), pl.BlockSpec((rows_blk, dh), rows),
               pl.BlockSpec((rows_blk, dh), lambda s, h, c: (s * ntb + c, nh + h))]
            + state_specs
            + [pl.BlockSpec((1, dh), lambda s, h, c: (0, h)), pl.BlockSpec((1, dh), lambda s, h, c: (0, h))],
            out_specs=[pl.BlockSpec((rows_blk, dh), rows), pl.BlockSpec((1, 1, dh, dh), mat),
                       pl.BlockSpec((1, 1, dh), per_head), pl.BlockSpec((1, 1, LANES), per_head)],
            out_shape=out_shape,
            scratch_shapes=[pltpu.VMEM((dh, dh), F32), pltpu.VMEM((1, dh), F32), pltpu.VMEM((1, LANES), F32)],
            compiler_params=_params("parallel", "parallel", "arbitrary"),
            name="mlstm_seq",
        )(q, k, v, gates, xc, up, *state_args, skip.reshape(1, di), norm_g.reshape(1, di))
    return out, cn, nn.reshape(nseq, nh, dh), mn[:, 0, 0].reshape(nseq, nh)


def _mlstm_dual_kernel(ql_ref, kl_ref, vl_ref, gl_ref, xcl_ref, zl_ref, skipl_ref, ngl_ref,
                       qs_ref, ks_ref, vs_ref, gs_ref, xcs_ref, zs_ref, c0_ref, n0_ref, m0_ref, skips_ref, ngs_ref,
                       outl_ref, cnl_ref, nnl_ref, mnl_ref, outs_ref, cns_ref, nns_ref, mns_ref, n_scr, m_scr,
                       *, hb, n_heads, chunks_per_seq):
    g = pl.program_id(0)
    c = g % chunks_per_seq
    head_l = (g // chunks_per_seq) % n_heads
    dh = ql_ref.shape[1]
    kscale = dh ** -0.5

    @pl.when(c == 0)
    def _():
        cnl_ref[...] = jnp.zeros_like(cnl_ref)
        n_scr[...] = jnp.zeros_like(n_scr)
        m_scr[...] = jnp.zeros_like(m_scr)

    def cl_store(cols, value):
        cnl_ref[0, 0, :, cols] = value

    gl = gl_ref[...]
    out, n_new, m_last = _mlstm_chunk(
        ql_ref[...], kl_ref[...].astype(F32) * kscale, vl_ref[...], _lane_column(gl, head_l),
        _lane_column(gl, head_l + n_heads),
        xcl_ref[...], zl_ref[...], skipl_ref[...], ngl_ref[...], m_scr[:, 0:1], n_scr[...],
        lambda cols: cnl_ref[0, 0, :, cols], cl_store)
    outl_ref[...] = out.astype(outl_ref.dtype)
    n_scr[...] = n_new
    m_scr[...] = jnp.broadcast_to(m_last, m_scr.shape)

    @pl.when(c == chunks_per_seq - 1)
    def _():
        nnl_ref[0] = n_scr[...]
        mnl_ref[0] = m_scr[...]

    _mlstm_step_kernel(qs_ref, ks_ref, vs_ref, gs_ref, xcs_ref, zs_ref, c0_ref, n0_ref, m0_ref, skips_ref, ngs_ref,
                       outs_ref, cns_ref, nns_ref, mns_ref, hb=hb, n_heads=n_heads, head_block=g % (n_heads // hb))


def dual_chunk_rows(nseq_l, t_l, nseq_s, nh, hb):
    steps = nseq_s * (nh // hb)
    total = nseq_l * t_l * nh
    if nh % hb or total % steps:
        return None
    chunk = total // steps
    ok = chunk % SUBLANES == 0 and t_l % chunk == 0 and 64 <= chunk <= M_CHUNK
    return chunk if ok else None


def mlstm_recurrence_dual(long_in, short_in, c0, n0, m0, skip, norm_g, *, nseq_l, t_l, nseq_s, t_s, nh, hb, chunk):
    di = long_in[0].shape[1]
    dh = di // nh
    nhb = nh // hb
    cps = t_l // chunk
    steps = nseq_s * nhb
    assert steps == nseq_l * nh * cps

    def rows_l(g):
        return (g // (nh * cps)) * cps + g % cps

    head_l = lambda g: (g // cps) % nh
    tile_l = lambda g: (rows_l(g), head_l(g))
    per_head_l = lambda g: (g // cps, 0, 0)
    tile_s = lambda g: (g // nhb, g % nhb)
    per_head_s = lambda g: (g, 0, 0)
    mat_s = lambda g: (g // nhb, g % nhb, 0, 0)
    n_l, n_s = long_in[0].shape[0], short_in[0].shape[0]
    in_specs = (
        [pl.BlockSpec((chunk, dh), tile_l)] * 3
        + [pl.BlockSpec((chunk, LANES), lambda g: (rows_l(g), 0)), pl.BlockSpec((chunk, dh), tile_l),
           pl.BlockSpec((chunk, dh), lambda g: (rows_l(g), nh + head_l(g))),
           pl.BlockSpec((1, dh), lambda g: (0, head_l(g))), pl.BlockSpec((1, dh), lambda g: (0, head_l(g)))]
        + [pl.BlockSpec((t_s, hb * dh), tile_s)] * 3
        + [pl.BlockSpec((t_s, LANES), lambda g: (g // nhb, 0)), pl.BlockSpec((t_s, hb * dh), tile_s),
           pl.BlockSpec((t_s, hb * dh), lambda g: (g // nhb, nhb + g % nhb)), pl.BlockSpec((1, hb, dh, dh), mat_s),
           pl.BlockSpec((hb, 1, dh), per_head_s), pl.BlockSpec((hb, 1, LANES), per_head_s),
           pl.BlockSpec((1, hb * dh), lambda g: (0, g % nhb)), pl.BlockSpec((1, hb * dh), lambda g: (0, g % nhb))])
    out_specs = [
        pl.BlockSpec((chunk, dh), tile_l), pl.BlockSpec((1, 1, dh, dh), lambda g: (g // (nh * cps), head_l(g), 0, 0)),
        pl.BlockSpec((1, 1, dh), per_head_l), pl.BlockSpec((1, 1, LANES), per_head_l),
        pl.BlockSpec((t_s, hb * dh), tile_s), pl.BlockSpec((1, hb, dh, dh), mat_s),
        pl.BlockSpec((hb, 1, dh), per_head_s), pl.BlockSpec((hb, 1, LANES), per_head_s)]
    out_shape = [
        jax.ShapeDtypeStruct((n_l, di), BF16), jax.ShapeDtypeStruct((nseq_l, nh, dh, dh), F32),
        jax.ShapeDtypeStruct((nseq_l * nh, 1, dh), F32), jax.ShapeDtypeStruct((nseq_l * nh, 1, LANES), F32),
        jax.ShapeDtypeStruct((n_s, di), F32), jax.ShapeDtypeStruct((nseq_s, nh, dh, dh), F32),
        jax.ShapeDtypeStruct((nseq_s * nh, 1, dh), F32), jax.ShapeDtypeStruct((nseq_s * nh, 1, LANES), F32)]
    skip2, ng2 = skip.reshape(1, di), norm_g.reshape(1, di)
    res = pl.pallas_call(
        functools.partial(_mlstm_dual_kernel, hb=hb, n_heads=nh, chunks_per_seq=cps),
        grid=(steps,),
        in_specs=in_specs,
        out_specs=out_specs,
        out_shape=out_shape,
        scratch_shapes=[pltpu.VMEM((1, dh), F32), pltpu.VMEM((1, LANES), F32)],
        compiler_params=_params("arbitrary"),
        name="mlstm_dual",
    )(*long_in, skip2, ng2, *short_in, c0, n0.reshape(nseq_s * nh, 1, dh),
      jnp.broadcast_to(m0.reshape(nseq_s * nh, 1, 1), (nseq_s * nh, 1, LANES)), skip2, ng2)
    unpack = lambda o, cn, nn, mn, nseq: (o, cn, nn.reshape(nseq, nh, dh), mn[:, 0, 0].reshape(nseq, nh))
    return unpack(*res[:4], nseq_l), unpack(*res[4:], nseq_s)


def _moe_expert_kernel(*refs, has_final, eb):
    xn_ref, gates_ref, wg_ref, wu_ref, wd_ref, x_ref = refs[:6]
    fg_ref = refs[6] if has_final else None
    o_ref = refs[6 + int(has_final)]
    e = pl.program_id(1)

    @pl.when(e == 0)
    def _():
        o_ref[...] = x_ref[...]

    xn = xn_ref[...]
    gates = gates_ref[...]
    update = None
    for k in range(eb):
        hg = jnp.dot(xn, wg_ref[0, k].astype(BF16), preferred_element_type=F32)
        hu = jnp.dot(xn, wu_ref[0, k].astype(BF16), preferred_element_type=F32)
        h = jax.nn.silu(hg) * hu * _lane_column(gates, e * eb + k)
        part = _dot(h, wd_ref[0, k])
        update = part if update is None else update + part
    o_ref[...] += update

    if has_final:
        @pl.when(e == pl.num_programs(1) - 1)
        def _():
            o_ref[...] = _rmsnorm_rows(o_ref[...], fg_ref[...])


def moe_experts(xn, gates, w_gate, w_up, w_down, layer, x, final_gain=None, tm=1024, eb=1):
    n, d = x.shape
    _, ne, _, f = w_gate.shape
    tm = _row_tile(n, tm)
    assert ne % eb == 0
    in_specs = [
        pl.BlockSpec((tm, d), lambda i, e: (i, 0)),
        pl.BlockSpec((tm, LANES), lambda i, e: (i, 0)),
        pl.BlockSpec((1, eb, d, f), lambda i, e: (layer, e, 0, 0)),
        pl.BlockSpec((1, eb, d, f), lambda i, e: (layer, e, 0, 0)),
        pl.BlockSpec((1, eb, f, d), lambda i, e: (layer, e, 0, 0)),
        pl.BlockSpec((tm, d), lambda i, e: (i, 0), pipeline_mode=pl.Buffered(1)),
    ]
    args = [xn, gates, w_gate, w_up, w_down, x]
    if final_gain is not None:
        in_specs.append(pl.BlockSpec((1, d), lambda i, e: (0, 0)))
        args.append(final_gain.reshape(1, d))
    return pl.pallas_call(
        functools.partial(_moe_expert_kernel, has_final=final_gain is not None, eb=eb),
        grid=(n // tm, ne // eb),
        in_specs=in_specs,
        out_specs=pl.BlockSpec((tm, d), lambda i, e: (i, 0)),
        out_shape=jax.ShapeDtypeStruct((n, d), F32),
        compiler_params=_params("parallel", "arbitrary"),
        name="moe_experts",
    )(*args)


def _blockdiag_tiles(w):
    nblocks, bi, bo = w.shape
    per = MXU_WIDTH // bi
    ntiles = nblocks // per
    rows_of_tile = w.reshape(ntiles, MXU_WIDTH, bo)
    spread = jnp.broadcast_to(rows_of_tile[:, :, None, :], (ntiles, MXU_WIDTH, per, bo)).reshape(
        ntiles, MXU_WIDTH, MXU_WIDTH)
    on_diag = (_iota((MXU_WIDTH, MXU_WIDTH), 0) // bi) == (_iota((MXU_WIDTH, MXU_WIDTH), 1) // bo)
    return jnp.where(on_diag, spread, 0.0).astype(BF16)


def _pad_cols(w, width=LANES):
    return jnp.pad(w, ((0, 0), (0, width - w.shape[1])))


def _history_tiles(buf):
    nseq, hist, ch = buf.shape
    return jnp.pad(buf, ((0, 0), (SUBLANES - hist, 0), (0, 0))).reshape(nseq * SUBLANES, ch)


def _block_plan(nseq, t):
    long_seq = t > SUBLANES
    return dict(
        lru=dict(nb=1 if long_seq else min(nseq, 64), tc=min(t, 256)),
        gla=dict(nb=1 if long_seq else min(nseq, 16), rows_blk=min(t, 512), chunk=min(t, GLA_CHUNK), sub=min(t, GLA_SUB)),
        pre=dict(nb=1 if long_seq else min(nseq, 16), tc=min(t, 128)),
        rec=dict(rows_blk=min(t, M_CHUNK), chunk=min(t, M_CHUNK), hb=2),
    )


_STATE_KEYS = ("lru_conv", "lru_h", "gla_s", "m_conv", "m_c", "m_n", "m_m")


class _Group:
    def __init__(self, x3, states):
        self.nseq, self.t, self.d = x3.shape
        self.x = x3.reshape(self.nseq * self.t, self.d)
        self.st = dict(zip(_STATE_KEYS, states))
        self.plan = _block_plan(self.nseq, self.t)
        self.outs = {k: [] for k in _STATE_KEYS}
        self.xn = self.gates = None

    def result(self):
        return (self.x.reshape(self.nseq, self.t, self.d),) + tuple(jnp.stack(self.outs[k]) for k in _STATE_KEYS)


def _trunks(groups,
            norm_mix_g, norm_ffn_g, norm_final_g,
            l0_w_in, l0_lru_conv_w, l0_lru_conv_b, l0_lru_wa, l0_lru_ba, l0_lru_wx, l0_lru_bx, l0_lru_lam,
            l0_gla_wa2, l0_gla_ba2, l0_gla_norm_g, l0_w_out,
            l1_w_up, l1_conv_w, l1_conv_b, l1_wq, l1_wk, l1_wv, l1_w_ig, l1_b_ig, l1_w_fg, l1_b_fg, l1_skip,
            l1_norm_g, l1_w_down,
            moe_w_rg, moe_b_rg, moe_w_re, moe_b_re, moe_w_gate, moe_w_up, moe_w_down):
    depth = norm_mix_g.shape[0]
    hist = CONV_W - 1
    for layer in range(depth):
        j = layer // 2
        wr = _pad_cols(jnp.concatenate([moe_w_re[layer], moe_w_rg[layer]], axis=1)).astype(BF16)
        br = _pad_cols(jnp.concatenate([moe_b_re[layer], moe_b_rg[layer]])[None, :])
        router = (norm_ffn_g[layer], wr, br)
        if layer % 2 == 0:
            w_in = l0_w_in[j].astype(BF16)
            w_out = l0_w_out[j].astype(BF16)
            wa_t, wx_t = _blockdiag_tiles(l0_lru_wa[j]), _blockdiag_tiles(l0_lru_wx[j])
            rank = l0_gla_wa2.shape[1]
            wa2 = jnp.pad(l0_gla_wa2[j], ((0, LANES - rank), (0, 0))).astype(BF16)
            for g in groups:
                w = g.st["lru_h"].shape[-1]
                _, _, nh, dk, dv = g.st["gla_s"].shape
                main = 2 * w + 2 * nh * dk + 2 * nh * dv
                proj, lg = fused_linear([g.x], w_in, n_out=main, gain=norm_mix_g[layer], name="linear_in",
                                        decay=(_pad_cols(w_in[:, main:main + rank]), wa2, l0_gla_ba2[j]))
                ya, h_last = lru_branch(
                    proj, _history_tiles(g.st["lru_conv"][j]), g.st["lru_h"][j], l0_lru_conv_w[j], l0_lru_conv_b[j],
                    wa_t, l0_lru_ba[j], wx_t, l0_lru_bx[j], l0_lru_lam[j], nseq=g.nseq, t=g.t, **g.plan["lru"])
                yb, s_new = gla_branch(
                    proj, lg, g.st["gla_s"][j], l0_gla_norm_g[j], nseq=g.nseq, t=g.t, **g.plan["gla"], q_off=2 * w,
                    k_off=2 * w + nh * dk, v_off=2 * w + 2 * nh * dk, g_off=2 * w + 2 * nh * dk + nh * dv)
                g.x, g.xn, g.gates = fused_linear([ya, yb], w_out, n_out=g.d, res=g.x, router=router, tm=512,
                                                  tn=g.d, name="linear_out")
                g.outs["lru_conv"].append(proj.reshape(g.nseq, g.t, main)[:, g.t - hist:, :w])
                g.outs["lru_h"].append(h_last)
                g.outs["gla_s"].append(s_new)
        else:
            w_up = l1_w_up[j].astype(BF16)
            w_down = l1_w_down[j].astype(BF16)
            wq_t, wk_t, wv_t = (_blockdiag_tiles(m[j]) for m in (l1_wq, l1_wk, l1_wv))
            wg = _pad_cols(jnp.concatenate([l1_w_ig[j], l1_w_fg[j]], axis=1)).astype(BF16)
            bg = _pad_cols(jnp.concatenate([l1_b_ig[j], l1_b_fg[j]])[None, :])
            fronts = []
            for g in groups:
                _, _, nh, dh = g.st["m_n"].shape
                up = fused_linear([g.x], w_up, n_out=2 * nh * dh, gain=norm_mix_g[layer], name="linear_up")
                q, k, v, xc, gates_m = mlstm_pre(
                    up, _history_tiles(g.st["m_conv"][j]), l1_conv_w[j], l1_conv_b[j], wq_t, wk_t, wv_t, wg, bg,
                    nseq=g.nseq, t=g.t, n_heads=nh, qkv_dtype=BF16 if g.t > SUBLANES else F32, **g.plan["pre"])
                fronts.append((q, k, v, gates_m, xc, up))
                g.outs["m_conv"].append(up.reshape(g.nseq, g.t, 2 * nh * dh)[:, g.t - hist:, :nh * dh])
            recs = _mlstm_recurrences(groups, fronts, j, l1_skip[j], l1_norm_g[j])
            for g, (hout, c_new, n_new, m_new) in zip(groups, recs):
                g.x, g.xn, g.gates = fused_linear([hout], w_down, n_out=g.d, res=g.x, router=router, tm=256,
                                                  tn=g.d, name="linear_down")
                g.outs["m_c"].append(c_new)
                g.outs["m_n"].append(n_new)
                g.outs["m_m"].append(m_new)
        for g in groups:
            g.x = moe_experts(g.xn, g.gates, moe_w_gate, moe_w_up, moe_w_down, layer, g.x,
                              final_gain=norm_final_g if layer == depth - 1 else None)
    return [g.result() for g in groups]


def _mlstm_recurrences(groups, fronts, j, skip, norm_g):
    nh = groups[0].st["m_n"].shape[2]
    hb = groups[0].plan["rec"]["hb"]
    if len(groups) == 2:
        for il, i_s in ((0, 1), (1, 0)):
            gl, gs = groups[il], groups[i_s]
            fits = gl.st["m_c"] is None and gs.st["m_c"] is not None and gs.t == SUBLANES and gl.t > SUBLANES
            chunk = dual_chunk_rows(gl.nseq, gl.t, gs.nseq, nh, hb) if fits else None
            if chunk is not None:
                rl, rs = mlstm_recurrence_dual(
                    fronts[il], fronts[i_s], gs.st["m_c"][j], gs.st["m_n"][j], gs.st["m_m"][j], skip, norm_g,
                    nseq_l=gl.nseq, t_l=gl.t, nseq_s=gs.nseq, t_s=gs.t, nh=nh, hb=hb, chunk=chunk)
                return [rl, rs] if il == 0 else [rs, rl]
    return [mlstm_recurrence(*f, None if g.st["m_c"] is None else g.st["m_c"][j], g.st["m_n"][j], g.st["m_m"][j],
                             skip, norm_g, nseq=g.nseq, t=g.t, nh=nh, **g.plan["rec"])
            for g, f in zip(groups, fronts)]


def _trunk(x3, *states_and_weights):
    return _trunks([_Group(x3, states_and_weights[:7])], *states_and_weights[7:])[0]


def kernel(x_prompt, x_sample, state_lru_conv, state_lru_h, state_gla_S, state_mlstm_conv, state_mlstm_C,
           state_mlstm_n, state_mlstm_m, norm_mix_g, norm_ffn_g, norm_final_g, l0_w_in, l0_lru_conv_w,
           l0_lru_conv_b, l0_lru_wa, l0_lru_ba, l0_lru_wx, l0_lru_bx, l0_lru_lam, l0_gla_wa2, l0_gla_ba2,
           l0_gla_norm_g, l0_w_out, l1_w_up, l1_conv_w, l1_conv_b, l1_wq, l1_wk, l1_wv, l1_w_ig, l1_b_ig,
           l1_w_fg, l1_b_fg, l1_skip, l1_norm_g, l1_w_down, moe_w_rg, moe_b_rg, moe_w_re, moe_b_re, moe_w_gate,
           moe_w_up, moe_w_down):
    weights = (norm_mix_g, norm_ffn_g, norm_final_g, l0_w_in, l0_lru_conv_w, l0_lru_conv_b, l0_lru_wa, l0_lru_ba,
               l0_lru_wx, l0_lru_bx, l0_lru_lam, l0_gla_wa2, l0_gla_ba2, l0_gla_norm_g, l0_w_out, l1_w_up,
               l1_conv_w, l1_conv_b, l1_wq, l1_wk, l1_wv, l1_w_ig, l1_b_ig, l1_w_fg, l1_b_fg, l1_skip, l1_norm_g,
               l1_w_down, moe_w_rg, moe_b_rg, moe_w_re, moe_b_re, moe_w_gate, moe_w_up, moe_w_down)
    states = (state_lru_conv, state_lru_h, state_gla_S, state_mlstm_conv, state_mlstm_C, state_mlstm_n,
              state_mlstm_m)
    bp = x_prompt.shape[0]
    zero_states = tuple(None if s is state_mlstm_C else jnp.zeros((s.shape[0], bp) + s.shape[2:], s.dtype)
                        for s in states)
    prompt, sample = _trunks([_Group(x_prompt, zero_states), _Group(x_sample, states)], *weights)
    return (prompt[0], sample[0]) + prompt[1:] + sample[1:]
```
